```python
import math
import jax, jax.numpy as jnp
from jax import lax
import numpy as np

D_MODEL = 2048
BATCH = 8
SEQ = 2048
DEPTH = 2

GRID_W = 64
CTX_LEN = 256
EXPAND = 2
D_INNER = EXPAND * D_MODEL
N_DIR = 2
N_MIXERS = 2
S5_GROUP = 16
S5_GROUPS = D_INNER // S5_GROUP
S5_STATE = 64
S5_CHUNK = 128
S5_DT_MIN = 1e-3
S5_DT_MAX = 1e-1
RWKV_HEAD = 64
RWKV_HEADS = D_INNER // RWKV_HEAD
LORA_DECAY = 96
LORA_AAA = 96
N_LERP = 6
RMS_EPS = 1e-6
GN_EPS = 64e-5
NORM_EPS = 1e-12

kernel_name = "hybrid_s5_rwkv7_prefix_dit"

F32 = jnp.float32


def rmsnorm(x, g):
    xf = x.astype(F32)
    y = xf * lax.rsqrt(jnp.mean(xf * xf, axis=-1, keepdims=True) + RMS_EPS)
    return (y * g.astype(F32)).astype(x.dtype)


def modulation(cond, ada_w, ada_b):
    m = (jax.nn.silu(cond) @ ada_w + ada_b).reshape(cond.shape[0], 1, 3 * D_MODEL)
    return jnp.split(m, 3, axis=-1)


def s5_discretize(a_re, a_im, log_step, b_re, b_im):
    a_re, a_im = a_re.astype(F32), a_im.astype(F32)
    b_re, b_im = b_re.astype(F32), b_im.astype(F32)
    dt = jnp.exp(log_step.astype(F32))[:, None]
    mag = jnp.exp(dt * a_re)
    ab_re = mag * jnp.cos(dt * a_im)
    ab_im = mag * jnp.sin(dt * a_im)
    den = a_re * a_re + a_im * a_im
    nr, ni = ab_re - 1.0, ab_im
    f_re = ((nr * a_re + ni * a_im) / den)[..., None]
    f_im = ((ni * a_re - nr * a_im) / den)[..., None]
    bb_re = f_re * b_re - f_im * b_im
    bb_im = f_re * b_im + f_im * b_re
    return ab_re, ab_im, bb_re, bb_im


def _complex_affine_combine(e1, e2):
    a1r, a1i, b1r, b1i = e1
    a2r, a2i, b2r, b2i = e2
    return (a2r * a1r - a2i * a1i, a2r * a1i + a2i * a1r,
            a2r * b1r - a2i * b1i + b2r, a2r * b1i + a2i * b1r + b2i)


def s5_chunked_scan(u, h0_re, h0_im, disc, c_re, c_im):
    ab_re, ab_im, bb_re, bb_im = disc
    c_re, c_im = c_re.astype(F32), c_im.astype(F32)
    b, l = u.shape[0], u.shape[1]
    n_chunks = l // S5_CHUNK
    uc = u.reshape(b, n_chunks, S5_CHUNK, S5_GROUPS, S5_GROUP).transpose(1, 2, 0, 3, 4)
    a_re = jnp.broadcast_to(ab_re, (S5_CHUNK, b, S5_GROUPS, S5_STATE))
    a_im = jnp.broadcast_to(ab_im, (S5_CHUNK, b, S5_GROUPS, S5_STATE))

    def chunk_step(carry, u_blk):
        h_re, h_im = carry
        bu_re = jnp.einsum('tbgi,gni->tbgn', u_blk, bb_re)
        bu_im = jnp.einsum('tbgi,gni->tbgn', u_blk, bb_im)
        ac_re, ac_im, hl_re, hl_im = lax.associative_scan(
            _complex_affine_combine, (a_re, a_im, bu_re, bu_im), axis=0)
        hs_re = hl_re + ac_re * h_re - ac_im * h_im
        hs_im = hl_im + ac_re * h_im + ac_im * h_re
        y = (jnp.einsum('tbgn,gin->tbgi', hs_re, c_re)
             - jnp.einsum('tbgn,gin->tbgi', hs_im, c_im))
        return (hs_re[-1], hs_im[-1]), y

    (h_re, h_im), ys = lax.scan(chunk_step, (h0_re, h0_im), uc)
    y = ys.transpose(2, 0, 1, 3, 4).reshape(b, l, S5_GROUPS, S5_GROUP)
    return y, h_re, h_im


def s5_direction(u, h0_re, h0_im, disc, c_re, c_im, reverse):
    b, l, _ = u.shape
    uu = u.astype(F32).reshape(b, l, S5_GROUPS, S5_GROUP)
    if reverse:
        uu = jnp.flip(uu, axis=1)
    y, h_re, h_im = s5_chunked_scan(uu, h0_re, h0_im, disc, c_re, c_im)
    if reverse:
        y = jnp.flip(y, axis=1)
    return y.reshape(b, l, D_INNER), h_re, h_im


def s5_output(y_ssm, u, z, d_skip, glu_w, glu_b, out_w):
    y = jax.nn.gelu(y_ssm + d_skip.astype(F32) * u.astype(F32)).astype(u.dtype)
    y = y * jax.nn.sigmoid(y @ glu_w + glu_b)
    return (y * jax.nn.silu(z)) @ out_w


def s5_mixer(h_lat, h_ctx, need_ctx_out, in_w, a_re, a_im, log_step, b_re, b_im,
             c_re, c_im, d_skip, glu_w, glu_b, out_w):
    u_lat, z_lat = jnp.split(h_lat @ in_w, 2, axis=-1)
    if need_ctx_out:
        u_ctx, z_ctx = jnp.split(h_ctx @ in_w, 2, axis=-1)
    else:
        u_ctx = h_ctx @ in_w[:, :D_INNER]
    h0 = jnp.zeros((h_ctx.shape[0], S5_GROUPS, S5_STATE), F32)
    ys_lat, ys_ctx = [], []
    for d in range(N_DIR):
        disc = s5_discretize(a_re[d], a_im[d], log_step[d], b_re[d], b_im[d])
        rev = d == 1
        y_c, hc_re, hc_im = s5_direction(u_ctx, h0, h0, disc, c_re[d], c_im[d], rev)
        y_l, _, _ = s5_direction(u_lat, hc_re, hc_im, disc, c_re[d], c_im[d], rev)
        ys_lat.append(y_l)
        ys_ctx.append(y_c)
    o_lat = s5_output(ys_lat[0] + ys_lat[1], u_lat, z_lat, d_skip, glu_w, glu_b, out_w)
    o_ctx = None
    if need_ctx_out:
        o_ctx = s5_output(ys_ctx[0] + ys_ctx[1], u_ctx, z_ctx, d_skip, glu_w, glu_b, out_w)
    return o_lat, o_ctx


def q_shift(h):
    b, l, d = h.shape
    rows = l // GRID_W
    g = h.reshape(b, rows, GRID_W, d)
    q = d // 4
    left = jnp.pad(g[:, :, :-1, :q], ((0, 0), (0, 0), (1, 0), (0, 0)))
    right = jnp.pad(g[:, :, 1:, q:2 * q], ((0, 0), (0, 0), (0, 1), (0, 0)))
    up = jnp.pad(g[:, :-1, :, 2 * q:3 * q], ((0, 0), (1, 0), (0, 0), (0, 0)))
    down = jnp.pad(g[:, 1:, :, 3 * q:], ((0, 0), (0, 1), (0, 0), (0, 0)))
    return jnp.concatenate([left, right, up, down], axis=-1).reshape(b, l, d)


def seq_shift(h):
    half = h.shape[-1] // 2
    prev = jnp.pad(h[:, :-1, :half], ((0, 0), (1, 0), (0, 0)))
    nxt = jnp.pad(h[:, 1:, half:], ((0, 0), (0, 1), (0, 0)))
    return jnp.concatenate([prev, nxt], axis=-1)


def rwkv_stream(h, shifted, with_gate, mu, in_w, w0, w1, w2, a0, a1, a2, k_k, k_a):
    b, l, _ = h.shape
    delta = shifted - h
    n_proj = 4 if with_gate else 3
    proj = [(h + delta * mu[i]) @ in_w[i] for i in range(n_proj)]
    z = proj[3] if with_gate else None
    xw = h + delta * mu[4]
    xa = h + delta * mu[5]

    def heads(t):
        return t.astype(F32).reshape(b, l, RWKV_HEADS, RWKV_HEAD)

    r, k, v = heads(proj[0]), heads(proj[1]), heads(proj[2])
    kk = k * k_k.astype(F32).reshape(RWKV_HEADS, RWKV_HEAD)
    kk = kk / jnp.maximum(jnp.sqrt(jnp.sum(kk * kk, axis=-1, keepdims=True)), NORM_EPS)
    k_a_h = k_a.astype(F32).reshape(RWKV_HEADS, RWKV_HEAD)
    dirs = []
    for d in range(N_DIR):
        w_log = -jax.nn.softplus(-heads(w0[d] + jnp.tanh(xw @ w1[d]) @ w2[d])) - 0.5
        decay = jnp.exp(-jnp.exp(w_log))
        a_rate = jax.nn.sigmoid(heads(a0[d] + (xa @ a1[d]) @ a2[d]))
        k_d = k * (1.0 + (a_rate - 1.0) * k_a_h)
        dirs.append((decay, k_d, -kk, kk * a_rate))
    return r, v, z, dirs


def rwkv_scan(r, decay, k, v, a, bvec, s0):
    def step(s, inp):
        r_t, w_t, k_t, v_t, a_t, b_t = inp
        sa = jnp.einsum('bhvk,bhk->bhv', s, a_t)
        s = (s * w_t[:, :, None, :] + sa[..., None] * b_t[:, :, None, :]
             + v_t[..., None] * k_t[:, :, None, :])
        return s, jnp.einsum('bhvk,bhk->bhv', s, r_t)

    xs = tuple(jnp.moveaxis(t, 1, 0) for t in (r, decay, k, v, a, bvec))
    s, ys = lax.scan(step, s0, xs)
    return jnp.moveaxis(ys, 0, 1), s


def rwkv_direction(r, v, dir_terms, s0, reverse):
    decay, k_d, a_vec, b_vec = dir_terms
    seqs = (r, decay, k_d, v, a_vec, b_vec)
    if reverse:
        seqs = tuple(jnp.flip(t, axis=1) for t in seqs)
    y, s = rwkv_scan(*seqs, s0)
    if reverse:
        y = jnp.flip(y, axis=1)
    return y, s


def rwkv_output(y_sum, r, k_sum, v, z, r_k, ln_w, ln_b, out_w):
    b, l = y_sum.shape[0], y_sum.shape[1]
    mu_ = jnp.mean(y_sum, axis=-1, keepdims=True)
    var = jnp.mean(jnp.square(y_sum - mu_), axis=-1, keepdims=True)
    yn = ((y_sum - mu_) * lax.rsqrt(var + GN_EPS)).reshape(b, l, D_INNER)
    yn = yn * ln_w.astype(F32) + ln_b.astype(F32)
    bonus = jnp.sum(r * k_sum * r_k.astype(F32), axis=-1, keepdims=True) * v
    y = (yn + bonus.reshape(b, l, D_INNER)).astype(z.dtype)
    return (y * jax.nn.silu(z)) @ out_w


def rwkv_mixer(h_lat, h_ctx, need_ctx_out, mu, in_w, w0, w1, w2, a0, a1, a2,
               k_k, k_a, r_k, ln_w, ln_b, out_w):
    lora = (mu, in_w, w0, w1, w2, a0, a1, a2, k_k, k_a)
    r_l, v_l, z_l, dirs_l = rwkv_stream(h_lat, q_shift(h_lat), True, *lora)
    r_c, v_c, z_c, dirs_c = rwkv_stream(h_ctx, seq_shift(h_ctx), need_ctx_out, *lora)
    s0 = jnp.zeros((h_ctx.shape[0], RWKV_HEADS, RWKV_HEAD, RWKV_HEAD), F32)
    ys_lat, ys_ctx = [], []
    for d in range(N_DIR):
        rev = d == 1
        y_c, s_c = rwkv_direction(r_c, v_c, dirs_c[d], s0, rev)
        y_l, _ = rwkv_direction(r_l, v_l, dirs_l[d], s_c, rev)
        ys_lat.append(y_l)
        ys_ctx.append(y_c)
    o_lat = rwkv_output(ys_lat[0] + ys_lat[1], r_l, dirs_l[0][1] + dirs_l[1][1], v_l, z_l,
                        r_k, ln_w, ln_b, out_w)
    o_ctx = None
    if need_ctx_out:
        o_ctx = rwkv_output(ys_ctx[0] + ys_ctx[1], r_c, dirs_c[0][1] + dirs_c[1][1], v_c,
                            z_c, r_k, ln_w, ln_b, out_w)
    return o_lat, o_ctx


def setup_inputs(seed: int = 0) -> dict:
    key = jax.random.key(seed)
    ks = iter(jax.random.split(key, 64))

    def nrm(shape, std):
        return jax.random.normal(next(ks), shape, F32) * std

    def gain(shape):
        return 1.0 + nrm(shape, 0.02)

    D, E, G, N, GC = D_MODEL, D_INNER, S5_GROUPS, S5_STATE, S5_GROUP
    p = {}
    p["x"] = nrm((BATCH, SEQ, D), 1.0)
    p["c"] = nrm((BATCH, D), 1.0)
    p["ctx"] = nrm((BATCH, CTX_LEN, D), 1.0)
    p["c_ctx"] = nrm((D,), 1.0)
    p["l0_norm_g"] = gain((D,))
    p["l0_ada_w"] = nrm((D, 3 * D), 0.5 * D ** -0.5)
    p["l0_ada_b"] = nrm((3 * D,), 0.01)
    p["l0_in_w"] = nrm((D, 2 * E), D ** -0.5)
    n_idx = jnp.arange(N, dtype=F32)
    p["l0_a_re"] = -0.5 + nrm((N_DIR, G, N), 0.01)
    p["l0_a_im"] = math.pi * n_idx + nrm((N_DIR, G, N), 0.01)
    p["l0_log_step"] = jax.random.uniform(next(ks), (N_DIR, G), F32,
                                          math.log(S5_DT_MIN), math.log(S5_DT_MAX))
    p["l0_b_re"] = nrm((N_DIR, G, N, GC), (2.0 * GC) ** -0.5)
    p["l0_b_im"] = nrm((N_DIR, G, N, GC), (2.0 * GC) ** -0.5)
    p["l0_c_re"] = nrm((N_DIR, G, GC, N), (2.0 * N) ** -0.5)
    p["l0_c_im"] = nrm((N_DIR, G, GC, N), (2.0 * N) ** -0.5)
    p["l0_d"] = nrm((E,), 1.0)
    p["l0_glu_w"] = nrm((E, E), E ** -0.5)
    p["l0_glu_b"] = nrm((E,), 0.01)
    p["l0_out_w"] = nrm((E, D), E ** -0.5)
    p["l1_norm_g"] = gain((D,))
    p["l1_ada_w"] = nrm((D, 3 * D), 0.5 * D ** -0.5)
    p["l1_ada_b"] = nrm((3 * D,), 0.01)
    p["l1_mu"] = jax.random.uniform(next(ks), (N_LERP, D), F32)
    p["l1_in_w"] = nrm((4, D, E), D ** -0.5)
    p["l1_w0"] = jax.random.uniform(next(ks), (N_DIR, E), F32, -6.0, 0.0)
    p["l1_w1"] = nrm((N_DIR, D, LORA_DECAY), D ** -0.5)
    p["l1_w2"] = nrm((N_DIR, LORA_DECAY, E), 0.1 * LORA_DECAY ** -0.5)
    p["l1_a0"] = nrm((N_DIR, E), 0.1)
    p["l1_a1"] = nrm((N_DIR, D, LORA_AAA), D ** -0.5)
    p["l1_a2"] = nrm((N_DIR, LORA_AAA, E), 0.1 * LORA_AAA ** -0.5)
    p["l1_k_k"] = 0.85 + nrm((E,), 0.02)
    p["l1_k_a"] = gain((E,))
    p["l1_r_k"] = nrm((RWKV_HEADS, RWKV_HEAD), 0.1)
    p["l1_ln_w"] = gain((E,))
    p["l1_ln_b"] = nrm((E,), 0.01)
    p["l1_out_w"] = nrm((E, D), E ** -0.5)
    p["final_norm_g"] = gain((D,))
    return p


def reference(x, c, ctx, c_ctx,
              l0_norm_g, l0_ada_w, l0_ada_b, l0_in_w, l0_a_re, l0_a_im, l0_log_step,
              l0_b_re, l0_b_im, l0_c_re, l0_c_im, l0_d, l0_glu_w, l0_glu_b, l0_out_w,
              l1_norm_g, l1_ada_w, l1_ada_b, l1_mu, l1_in_w, l1_w0, l1_w1, l1_w2,
              l1_a0, l1_a1, l1_a2, l1_k_k, l1_k_a, l1_r_k, l1_ln_w, l1_ln_b, l1_out_w,
              final_norm_g):
    mixers = (s5_mixer, rwkv_mixer)
    layers = (
        ((l0_norm_g, l0_ada_w, l0_ada_b),
         (l0_in_w, l0_a_re, l0_a_im, l0_log_step, l0_b_re, l0_b_im, l0_c_re, l0_c_im,
          l0_d, l0_glu_w, l0_glu_b, l0_out_w)),
        ((l1_norm_g, l1_ada_w, l1_ada_b),
         (l1_mu, l1_in_w, l1_w0, l1_w1, l1_w2, l1_a0, l1_a1, l1_a2, l1_k_k, l1_k_a,
          l1_r_k, l1_ln_w, l1_ln_b, l1_out_w)),
    )
    x_lat, x_ctx = x, ctx
    for i in range(DEPTH):
        (norm_g, ada_w, ada_b), mixer_params = layers[i]
        mixer = mixers[i % N_MIXERS]
        need_ctx_out = i < DEPTH - 1
        sh_l, sc_l, g_l = modulation(c, ada_w, ada_b)
        sh_c, sc_c, g_c = modulation(c_ctx[None], ada_w, ada_b)
        h_lat = rmsnorm(x_lat, norm_g) * (1.0 + sc_l) + sh_l
        h_ctx = rmsnorm(x_ctx, norm_g) * (1.0 + sc_c) + sh_c
        o_lat, o_ctx = mixer(h_lat, h_ctx, need_ctx_out, *mixer_params)
        x_lat = x_lat + g_l * o_lat
        if need_ctx_out:
            x_ctx = x_ctx + g_c * o_ctx
    return rmsnorm(x_lat, final_norm_g)
```

```python
import functools
import math

import jax
import jax.numpy as jnp
from jax import lax
from jax.experimental import pallas as pl
from jax.experimental.pallas import tpu as pltpu

F32 = jnp.float32
BF16 = jnp.bfloat16

GRID_W = 64
S5_GROUP = 16
S5_STATE = 64
S5_T = 16
RWKV_HEAD = 64
RWKV_CHUNK = 64
RMS_EPS = 1e-6
GN_EPS = 64e-5
NORM_EPS = 1e-12

VMEM_LIMIT_BYTES = 56 * 1024 * 1024


def _dot(a, b):
    return jnp.dot(a.astype(BF16), b.astype(BF16), preferred_element_type=F32)


def _dot_nt(a, b):
    return lax.dot_general(a.astype(BF16), b.astype(BF16), (((1,), (1,)), ((), ())),
                           preferred_element_type=F32)


def _dot_tn(a, b):
    return lax.dot_general(a.astype(BF16), b.astype(BF16), (((0,), (0,)), ((), ())),
                           preferred_element_type=F32)


def _split2(x):
    hi = x.astype(BF16)
    lo = (x - hi.astype(F32)).astype(BF16)
    return hi, lo


def _split3(x):
    p1 = x.astype(BF16)
    r1 = x - p1.astype(F32)
    p2 = r1.astype(BF16)
    p3 = (r1 - p2.astype(F32)).astype(BF16)
    return p1, p2, p3


def _mm_kernel(a_ref, b_ref, o_ref):
    o_ref[...] = jnp.dot(a_ref[...], b_ref[...],
                         preferred_element_type=F32).astype(o_ref.dtype)


def _mm_bias_kernel(a_ref, b_ref, bias_ref, o_ref):
    acc = jnp.dot(a_ref[...], b_ref[...], preferred_element_type=F32)
    o_ref[...] = (acc + bias_ref[...]).astype(o_ref.dtype)


def matmul(a, b, bias=None, *, out_dtype, tm, tn):
    m, k = a.shape
    k2, n = b.shape
    assert k == k2 and m % tm == 0 and n % tn == 0, (a.shape, b.shape, tm, tn)
    a = a.astype(BF16)
    b = b.astype(BF16)
    in_specs = [pl.BlockSpec((tm, k), lambda j, i: (i, 0)),
                pl.BlockSpec((k, tn), lambda j, i: (0, j))]
    args = [a, b]
    kern = _mm_kernel
    if bias is not None:
        in_specs.append(pl.BlockSpec((1, tn), lambda j, i: (0, j)))
        args.append(bias.astype(F32).reshape(1, n))
        kern = _mm_bias_kernel
    return pl.pallas_call(
        kern,
        grid=(n // tn, m // tm),
        in_specs=in_specs,
        out_specs=pl.BlockSpec((tm, tn), lambda j, i: (i, j)),
        out_shape=jax.ShapeDtypeStruct((m, n), out_dtype),
        compiler_params=pltpu.CompilerParams(
            dimension_semantics=("arbitrary", "arbitrary"),
            vmem_limit_bytes=VMEM_LIMIT_BYTES),
    )(*args)


def _pick(total, pref):
    t = min(pref, total)
    while total % t:
        t //= 2
    return t


def _s5_prep_kernel(are_ref, aim_ref, ls_ref, bre_ref, bim_ref, cre_ref, cim_ref,
                    w1_ref, ot_ref, ar_ref, ai_ref, *, gb):
    T = S5_T
    GC = S5_GROUP
    lane = lax.broadcasted_iota(jnp.int32, (1, 128), 1)
    fwd = lane < S5_STATE
    s_col = lax.broadcasted_iota(jnp.int32, (T, 1), 0).astype(F32)
    ri = lax.broadcasted_iota(jnp.int32, (T * GC, T * GC), 0) // GC
    ci = lax.broadcasted_iota(jnp.int32, (T * GC, T * GC), 1) // GC
    lane2 = lax.broadcasted_iota(jnp.int32, (1, 256), 1) % 128
    fwd2 = lane2 < S5_STATE
    half = float(T // 2)

    for g in range(gb):
        a_re = are_ref[g]
        a_im = aim_ref[g]
        dt = jnp.exp(ls_ref[g])
        x_re = dt * a_re
        x_im = dt * a_im

        def lam_pow(p):
            mag = jnp.exp(p * x_re)
            ang = p * x_im
            return mag * jnp.cos(ang), mag * jnp.sin(ang)

        l_re, l_im = lam_pow(jnp.ones((1, 1), F32))
        den = a_re * a_re + a_im * a_im
        nr, ni = l_re - 1.0, l_im
        f_re = (nr * a_re + ni * a_im) / den
        f_im = (ni * a_re - nr * a_im) / den
        b_re = bre_ref[g]
        b_im = bim_ref[g]
        bb_re = f_re * b_re - f_im * b_im
        bb_im = f_re * b_im + f_im * b_re
        c_re = cre_ref[g]
        c_im = cim_ref[g]

        def times_pow(v_re, v_im, p):
            p_re, p_im = lam_pow(p)
            o_re = v_re[None, :, :] * p_re[:, None, :] - v_im[None, :, :] * p_im[:, None, :]
            o_im = v_re[None, :, :] * p_im[:, None, :] + v_im[None, :, :] * p_re[:, None, :]
            return o_re.reshape(T * GC, 128), o_im.reshape(T * GC, 128)

        pb = jnp.where(fwd, half - s_col, s_col - half)
        pc = -pb
        bl_re, bl_im = times_pow(bb_re, bb_im, pb)
        cl_re, cl_im = times_pow(c_re, c_im, pc)
        lhs = jnp.concatenate([bl_re, bl_im], axis=1)
        rhs = jnp.concatenate([cl_re, -cl_im], axis=1)
        r_hi, r_lo = _split2(rhs)

        def dot3(l):
            l_hi, l_lo = _split2(l)
            dn = (((1,), (1,)), ((), ()))
            return (lax.dot_general(l_hi, r_hi, dn, preferred_element_type=F32)
                    + lax.dot_general(l_hi, r_lo, dn, preferred_element_type=F32)
                    + lax.dot_general(l_lo, r_hi, dn, preferred_element_type=F32))

        t_f = dot3(jnp.where(fwd2, lhs, 0.0))
        t_b = dot3(jnp.where(fwd2, 0.0, lhs))
        toep = jnp.where(ci >= ri, t_f, 0.0) + jnp.where(ri >= ci, t_b, 0.0)

        ps = jnp.where(fwd, (T - 1.0) - s_col, s_col)
        s_re, s_im = times_pow(bb_re, bb_im, ps)
        w1_ref[g] = jnp.concatenate([toep, s_re, s_im], axis=1).astype(w1_ref.dtype)

        po = jnp.where(fwd, s_col + 1.0, T - s_col)
        o_re, o_im = times_pow(c_re, c_im, po)
        ot_ref[g] = jnp.concatenate([o_re, -o_im], axis=1).astype(ot_ref.dtype)

        pw_re, pw_im = lam_pow(jnp.full((1, 1), float(T), F32))
        ar_ref[g] = pw_re
        ai_ref[g] = pw_im


def _s5_kernel(u_ref, w1_ref, ot_ref, ar_ref, ai_ref, y_ref,
               p_ref, hf_re, hf_im, hb_re, hb_im, *, nb, n_ctx, n_chunks):
    p_ref[...] = jnp.dot(u_ref[0], w1_ref[0], preferred_element_type=F32)
    ar = jnp.broadcast_to(ar_ref[0], (nb, 128))
    ai = jnp.broadcast_to(ai_ref[0], (nb, 128))
    fwd = lax.broadcasted_iota(jnp.int32, (nb, 128), 1) < S5_STATE

    def step(i, carry):
        h_re, h_im = carry
        cb = jnp.where(i < n_ctx, n_ctx - 1 - i, n_chunks - 1 - (i - n_ctx))
        rf = pl.multiple_of(i * nb, nb)
        rb = pl.multiple_of(cb * nb, nb)
        hf_re[pl.ds(rf, nb), :] = h_re
        hf_im[pl.ds(rf, nb), :] = h_im
        hb_re[pl.ds(rb, nb), :] = h_re
        hb_im[pl.ds(rb, nb), :] = h_im
        x_re = jnp.where(fwd, p_ref[pl.ds(rf, nb), 256:384], p_ref[pl.ds(rb, nb), 256:384])
        x_im = jnp.where(fwd, p_ref[pl.ds(rf, nb), 384:512], p_ref[pl.ds(rb, nb), 384:512])
        n_re = ar * h_re - ai * h_im + x_re
        n_im = ar * h_im + ai * h_re + x_im
        return n_re, n_im

    zero = jnp.zeros((nb, 128), F32)
    lax.fori_loop(0, n_chunks, step, (zero, zero))

    rows = p_ref.shape[0]
    fwd_all = lax.broadcasted_iota(jnp.int32, (rows, 128), 1) < S5_STATE
    h_all = jnp.concatenate([jnp.where(fwd_all, hf_re[...], hb_re[...]),
                             jnp.where(fwd_all, hf_im[...], hb_im[...])], axis=1)
    y = p_ref[:, 0:256] + _dot_nt(h_all, ot_ref[0])
    y_ref[0] = y.astype(y_ref.dtype)


def s5_operators(a_re, a_im, log_step, b_re, b_im, c_re, c_im):
    n_dir, g, n = a_re.shape
    gb = 8

    def lanes(x):
        return jnp.concatenate([x[0], x[1]], axis=-1).astype(F32)

    are = lanes(a_re).reshape(g, 1, 2 * n)
    aim = lanes(a_im).reshape(g, 1, 2 * n)
    ls = lanes(jnp.broadcast_to(log_step[:, :, None], (n_dir, g, n))).reshape(g, 1, 2 * n)
    bre = lanes(jnp.swapaxes(b_re, -1, -2))
    bim = lanes(jnp.swapaxes(b_im, -1, -2))
    cre = lanes(c_re)
    cim = lanes(c_im)
    row = pl.BlockSpec((gb, 1, 2 * n), lambda i: (i, 0, 0))
    mat = pl.BlockSpec((gb, S5_GROUP, 2 * n), lambda i: (i, 0, 0))
    tg = S5_T * S5_GROUP
    return pl.pallas_call(
        functools.partial(_s5_prep_kernel, gb=gb),
        grid=(g // gb,),
        in_specs=[row, row, row, mat, mat, mat, mat],
        out_specs=[pl.BlockSpec((gb, tg, tg + 4 * n), lambda i: (i, 0, 0)),
                   pl.BlockSpec((gb, tg, 4 * n), lambda i: (i, 0, 0)),
                   row, row],
        out_shape=[jax.ShapeDtypeStruct((g, tg, tg + 4 * n), BF16),
                   jax.ShapeDtypeStruct((g, tg, 4 * n), BF16),
                   jax.ShapeDtypeStruct((g, 1, 2 * n), F32),
                   jax.ShapeDtypeStruct((g, 1, 2 * n), F32)],
        compiler_params=pltpu.CompilerParams(dimension_semantics=("arbitrary",),
                                             vmem_limit_bytes=VMEM_LIMIT_BYTES),
    )(are, aim, ls, bre, bim, cre, cim)


def s5_scan(u_g, w1, ot, ar, ai, *, nb, n_ctx, n_chunks):
    g, rows, tg = u_g.shape
    assert rows == n_chunks * nb and tg == S5_T * S5_GROUP
    blk = lambda w: pl.BlockSpec((1,) + w, lambda i: (i, 0, 0))
    return pl.pallas_call(
        functools.partial(_s5_kernel, nb=nb, n_ctx=n_ctx, n_chunks=n_chunks),
        grid=(g,),
        in_specs=[blk((rows, tg)), blk(w1.shape[1:]), blk(ot.shape[1:]),
                  blk((1, 128)), blk((1, 128))],
        out_specs=blk((rows, tg)),
        out_shape=jax.ShapeDtypeStruct((g, rows, tg), BF16),
        scratch_shapes=[pltpu.VMEM((rows, w1.shape[2]), F32)]
                       + [pltpu.VMEM((rows, 128), F32)] * 4,
        compiler_params=pltpu.CompilerParams(dimension_semantics=("arbitrary",),
                                             vmem_limit_bytes=VMEM_LIMIT_BYTES),
    )(u_g, w1, ot, ar, ai)


def _rwkv_kernel(r_ref, k_ref, v_ref, w_ref, a_ref, kk_ref, ka_ref, o_ref, s_ref,
                 *, n_ctx, n_chunks, n_pairs):
    C = RWKV_CHUNK
    pw = n_pairs * 128
    ri = lax.broadcasted_iota(jnp.int32, (128, 128), 0)
    ci = lax.broadcasted_iota(jnp.int32, (128, 128), 1)
    t_i = ri % C
    s_i = ci % C
    same_blk = (t_i // 16) == (s_i // 16)
    lane_head = lax.broadcasted_iota(jnp.int32, (C, 128), 1) // RWKV_HEAD
    tr = lax.broadcasted_iota(jnp.int32, (C, C), 0)
    tc = lax.broadcasted_iota(jnp.int32, (C, C), 1)
    hr = lax.broadcasted_iota(jnp.int32, (pw, pw), 0) // RWKV_HEAD
    hc = lax.broadcasted_iota(jnp.int32, (pw, pw), 1) // RWKV_HEAD
    head_ones = (hr == hc).astype(BF16)
    k_k = kk_ref[...]
    k_a = ka_ref[...]

    def stack(x):
        return jnp.concatenate([jnp.where(lane_head == 0, x, 0.0),
                                jnp.where(lane_head == 1, x, 0.0)], axis=0)

    for d in range(2):
        if d == 0:
            strict, incl, cum = s_i < t_i, s_i <= t_i, tc <= tr
            last = C - 1
        else:
            strict, incl, cum = s_i > t_i, s_i >= t_i, tc >= tr
            last = 0
        diag_m = strict & same_blk
        off_m = strict & jnp.logical_not(same_blk)
        cum_m = cum.astype(BF16)
        s_ref[...] = jnp.zeros(s_ref.shape, F32)

        def chunk_body(i, carry, d=d, strict=strict, incl=incl, diag_m=diag_m, off_m=off_m,
                       cum_m=cum_m, last=last):
            if d == 0:
                cidx = i
            else:
                cidx = jnp.where(i < n_ctx, n_ctx - 1 - i, n_chunks - 1 - (i - n_ctx))
            row0 = pl.multiple_of(cidx * C, C)
            rows = pl.ds(row0, C)
            r = r_ref[0, rows, :].astype(F32)
            k = k_ref[0, rows, :].astype(F32)
            v = v_ref[0, rows, :].astype(F32)
            wp = w_ref[d, 0, rows, :]
            ap = a_ref[d, 0, rows, :]
            sp = jnp.maximum(-wp, 0.0) + jnp.log(1.0 + jnp.exp(-jnp.abs(wp)))
            lw = -jnp.exp(-sp - 0.5)
            a_rate = 1.0 / (1.0 + jnp.exp(-ap))
            kkr = k * k_k
            sq_hi, sq_lo = _split2(kkr * kkr)
            ssq = (jnp.dot(sq_hi, head_ones, preferred_element_type=F32)
                   + jnp.dot(sq_lo, head_ones, preferred_element_type=F32))
            kk = kkr / jnp.maximum(jnp.sqrt(ssq), NORM_EPS)
            kd = k * (1.0 + (a_rate - 1.0) * k_a)
            lc = sum(jnp.dot(cum_m, p, preferred_element_type=F32) for p in _split3(lw))
            l_mid = lc[C // 2:C // 2 + 1, :]
            l_end = lc[last:last + 1, :]
            dl = lc - l_mid
            e_in = jnp.exp(dl)
            e_ex = jnp.exp(dl - lw)
            e_iv = jnp.exp(-dl)
            c_row = jnp.exp(l_mid)
            e2 = jnp.exp(l_end - l_mid)
            wc = jnp.exp(l_end)
            rt = r * e_in
            at = -kk * e_ex
            bt = kk * a_rate * e_iv
            kt = kd * e_iv

            for p in range(n_pairs):
                ls = slice(p * 128, (p + 1) * 128)
                a2, r2, b2, k2, v2 = (stack(x[:, ls]) for x in (at, rt, bt, kt, v))
                a4 = _dot_nt(jnp.concatenate([a2, r2], axis=0),
                             jnp.concatenate([b2, k2], axis=0))
                ab = a4[:2 * C, :2 * C]
                ak = jnp.where(strict, a4[:2 * C, 2 * C:], 0.0)
                rb = jnp.where(incl, a4[2 * C:, :2 * C], 0.0)
                rk = jnp.where(incl, a4[2 * C:, 2 * C:], 0.0)
                ad = jnp.where(diag_m, ab, 0.0)
                ao = jnp.where(off_m, ab, 0.0)
                p2 = _dot(ad, ad)
                p4 = _dot(p2, p2)
                p8 = _dot(p4, p4)
                q1 = ad + p2 + _dot(ad, p2)
                q2 = p4 + p8 + _dot(p4, p8)
                dm = q1 + q2 + _dot(q1, q2)
                n1 = ao + _dot(dm, ao)
                n2 = _dot(n1, n1)
                n3 = _dot(n1, n2)
                ns = n1 + n2 + n3
                tm = ns + dm + _dot(ns, dm)
                kv = _dot(jnp.concatenate([ak, rk], axis=0), v2)
                x0 = jnp.concatenate([a2, kv[:2 * C]], axis=1)
                x1 = x0 + _dot(tm, x0)
                z = _dot(rb, x1)
                c_p, e2_p, wc_p = c_row[:, ls], e2[:, ls], wc[:, ls]
                ca = x1[:, :128] * c_p
                vp = x1[:, 128:]
                cr = (r2 + z[:, :128]) * c_p
                yp = z[:, 128:] + kv[2 * C:]
                lt = jnp.concatenate(
                    [jnp.concatenate([ca, vp], axis=1),
                     jnp.concatenate([jnp.zeros_like(v2), v2], axis=1)], axis=0)
                rt2 = jnp.concatenate([b2 * e2_p, k2 * e2_p], axis=0)
                mh = _dot_tn(lt, rt2)
                s_old = s_ref[p]
                y2 = _dot_nt(cr, s_old) + yp
                s_ref[p] = s_old * wc_p + _dot(s_old, mh[:128]) + mh[128:]
                y = y2[:C] + y2[C:]

                @pl.when(cidx >= n_ctx)
                def _(y=y, ls=ls):
                    orow = pl.multiple_of((cidx - n_ctx) * C, C)
                    if d == 0:
                        o_ref[0, pl.ds(orow, C), ls] = y
                    else:
                        o_ref[0, pl.ds(orow, C), ls] += y
            return carry

        lax.fori_loop(0, n_chunks, chunk_body, 0)


def rwkv_scan(r, k, v, wpre, apre, k_k, k_a, *, n_ctx_tokens):
    b, l, e = r.shape
    n_pairs = 2
    pw = n_pairs * 128
    n_chunks = l // RWKV_CHUNK
    n_ctx = n_ctx_tokens // RWKV_CHUNK
    l_out = l - n_ctx_tokens
    seq = pl.BlockSpec((1, l, pw), lambda i, j: (i, 0, j))
    seq2 = pl.BlockSpec((2, 1, l, pw), lambda i, j: (0, i, 0, j))
    vec = pl.BlockSpec((1, pw), lambda i, j: (0, j))
    return pl.pallas_call(
        functools.partial(_rwkv_kernel, n_ctx=n_ctx, n_chunks=n_chunks, n_pairs=n_pairs),
        grid=(b, e // pw),
        in_specs=[seq, seq, seq, seq2, seq2, vec, vec],
        out_specs=pl.BlockSpec((1, l_out, pw), lambda i, j: (i, 0, j)),
        out_shape=jax.ShapeDtypeStruct((b, l_out, e), F32),
        scratch_shapes=[pltpu.VMEM((n_pairs, 128, 128), F32)],
        compiler_params=pltpu.CompilerParams(dimension_semantics=("arbitrary", "arbitrary"),
                                             vmem_limit_bytes=VMEM_LIMIT_BYTES),
    )(r, k, v, wpre, apre, k_k.astype(F32).reshape(1, e), k_a.astype(F32).reshape(1, e))


def _rmsnorm(x, g):
    xf = x.astype(F32)
    y = xf * lax.rsqrt(jnp.mean(xf * xf, axis=-1, keepdims=True) + RMS_EPS)
    return y * g.astype(F32)


def _modulation(c, c_ctx, ada_w, ada_b):
    d = c.shape[-1]
    cond = jnp.concatenate([c, c_ctx[None]], axis=0)
    rows = cond.shape[0]
    pad = (-rows) % 16
    act = jnp.pad(jax.nn.silu(cond), ((0, pad), (0, 0)))
    m = matmul(act, ada_w, ada_b, out_dtype=F32, tm=rows + pad, tn=_pick(3 * d, 1024))
    m = m[:rows]
    lat = [t[:, None, :] for t in jnp.split(m[:rows - 1], 3, axis=-1)]
    ctx = [t[None] for t in jnp.split(m[rows - 1:], 3, axis=-1)]
    return lat, ctx


def _q_shift(h):
    b, l, d = h.shape
    rows = l // GRID_W
    g = h.reshape(b, rows, GRID_W, d)
    q = d // 4
    left = jnp.pad(g[:, :, :-1, :q], ((0, 0), (0, 0), (1, 0), (0, 0)))
    right = jnp.pad(g[:, :, 1:, q:2 * q], ((0, 0), (0, 0), (0, 1), (0, 0)))
    up = jnp.pad(g[:, :-1, :, 2 * q:3 * q], ((0, 0), (1, 0), (0, 0), (0, 0)))
    down = jnp.pad(g[:, 1:, :, 3 * q:], ((0, 0), (0, 1), (0, 0), (0, 0)))
    return jnp.concatenate([left, right, up, down], axis=-1).reshape(b, l, d)


def _seq_shift(h):
    half = h.shape[-1] // 2
    prev = jnp.pad(h[:, :-1, :half], ((0, 0), (1, 0), (0, 0)))
    nxt = jnp.pad(h[:, 1:, half:], ((0, 0), (0, 1), (0, 0)))
    return jnp.concatenate([prev, nxt], axis=-1)


def _s5_layer(x_lat, x_ctx, c, c_ctx, norm_g, ada_w, ada_b, in_w, a_re, a_im, log_step,
              b_re, b_im, c_re, c_im, d_skip, glu_w, glu_b, out_w):
    b, l_lat, d = x_lat.shape
    l_ctx = x_ctx.shape[1]
    l = l_ctx + l_lat
    e = in_w.shape[1] // 2
    g = e // S5_GROUP
    (sh_l, sc_l, g_l), (sh_c, sc_c, g_c) = _modulation(c, c_ctx, ada_w, ada_b)
    h_lat = _rmsnorm(x_lat, norm_g) * (1.0 + sc_l) + sh_l
    h_ctx = _rmsnorm(x_ctx, norm_g) * (1.0 + sc_c) + sh_c
    h = jnp.concatenate([h_ctx, h_lat], axis=1).astype(BF16).reshape(b * l, d)
    m = b * l
    tm = _pick(m, 1024)
    uz = matmul(h, in_w, out_dtype=BF16, tm=tm, tn=_pick(2 * e, 1024))
    u, z = uz[:, :e], uz[:, e:]

    n_chunks = l // S5_T
    u_g = (u.reshape(b, n_chunks, S5_T, g, S5_GROUP).transpose(3, 1, 0, 2, 4)
           .reshape(g, n_chunks * b, S5_T * S5_GROUP))
    w1, ot, ar, ai = s5_operators(a_re, a_im, log_step, b_re, b_im, c_re, c_im)
    y_g = s5_scan(u_g, w1, ot, ar, ai, nb=b, n_ctx=l_ctx // S5_T, n_chunks=n_chunks)
    y_ssm = (y_g.reshape(g, n_chunks, b, S5_T, S5_GROUP).transpose(2, 1, 3, 0, 4)
             .reshape(m, e))

    y1 = jax.nn.gelu(y_ssm.astype(F32) + d_skip.astype(F32) * u.astype(F32)).astype(BF16)
    gl = matmul(y1, glu_w, glu_b, out_dtype=F32, tm=tm, tn=_pick(e, 1024))
    y2 = (y1.astype(F32) * jax.nn.sigmoid(gl) * jax.nn.silu(z.astype(F32))).astype(BF16)
    o = matmul(y2, out_w, out_dtype=F32, tm=tm, tn=_pick(d, 1024)).reshape(b, l, d)
    x_ctx = x_ctx + g_c * o[:, :l_ctx]
    x_lat = x_lat + g_l * o[:, l_ctx:]
    return x_lat, x_ctx


def _rwkv_layer(x_lat, x_ctx, c, c_ctx, norm_g, ada_w, ada_b, mu, in_w, w0, w1, w2, a0, a1, a2,
                k_k, k_a, r_k, ln_w, ln_b, out_w):
    b, l_lat, d = x_lat.shape
    l_ctx = x_ctx.shape[1]
    l = l_ctx + l_lat
    e = in_w.shape[2]
    heads = e // RWKV_HEAD
    m = b * l
    (sh_l, sc_l, g_l), (sh_c, sc_c, _) = _modulation(c, c_ctx, ada_w, ada_b)
    h_lat = _rmsnorm(x_lat, norm_g) * (1.0 + sc_l) + sh_l
    h_ctx = _rmsnorm(x_ctx, norm_g) * (1.0 + sc_c) + sh_c
    h = jnp.concatenate([h_ctx, h_lat], axis=1)
    delta = jnp.concatenate([_seq_shift(h_ctx), _q_shift(h_lat)], axis=1) - h
    mix = [(h + delta * mu[i]).astype(BF16).reshape(m, d) for i in range(mu.shape[0])]

    tm = _pick(m, 1024)
    tn = _pick(e, 1024)
    r, k, v, z = (matmul(mix[i], in_w[i], out_dtype=BF16, tm=tm, tn=tn) for i in range(4))

    n_dir, _, lora = w1.shape
    lp = (-lora) % 128

    def lora_pre(xm, l1, l2, l0, act):
        l1c = jnp.concatenate([jnp.pad(l1[i], ((0, 0), (0, lp))) for i in range(n_dir)], axis=1)
        t = act(matmul(xm, l1c, out_dtype=F32, tm=tm, tn=l1c.shape[1]))
        outs = []
        for i in range(n_dir):
            ti = t[:, i * (lora + lp):(i + 1) * (lora + lp)]
            outs.append(matmul(ti, jnp.pad(l2[i], ((0, lp), (0, 0))), l0[i],
                               out_dtype=F32, tm=tm, tn=tn))
        return jnp.stack(outs).reshape(n_dir, b, l, e)

    wpre = lora_pre(mix[4], w1, w2, w0, jnp.tanh)
    apre = lora_pre(mix[5], a1, a2, a0, lambda t: t)

    y_sum = rwkv_scan(r.reshape(b, l, e), k.reshape(b, l, e), v.reshape(b, l, e), wpre, apre,
                      k_k, k_a, n_ctx_tokens=l_ctx)

    def lat_heads(t):
        return t.reshape(b, l, e)[:, l_ctx:].astype(F32).reshape(b, l_lat, heads, RWKV_HEAD)

    y4 = y_sum.reshape(b, l_lat, heads, RWKV_HEAD)
    mean = jnp.mean(y4, axis=-1, keepdims=True)
    var = jnp.mean(jnp.square(y4 - mean), axis=-1, keepdims=True)
    yn = ((y4 - mean) * lax.rsqrt(var + GN_EPS)).reshape(b, l_lat, e)
    yn = yn * ln_w.astype(F32) + ln_b.astype(F32)
    ar_sum = (jax.nn.sigmoid(apre[0]) + jax.nn.sigmoid(apre[1]))[:, l_ctx:]
    k_lat = k.reshape(b, l, e)[:, l_ctx:].astype(F32)
    k_sum = (k_lat * (2.0 + (ar_sum - 2.0) * k_a.astype(F32))).reshape(b, l_lat, heads, RWKV_HEAD)
    bonus = jnp.sum(lat_heads(r) * k_sum * r_k.astype(F32), axis=-1, keepdims=True) * lat_heads(v)
    z_lat = z.reshape(b, l, e)[:, l_ctx:].astype(F32)
    yg = ((yn + bonus.reshape(b, l_lat, e)) * jax.nn.silu(z_lat)).astype(BF16)
    o = matmul(yg.reshape(b * l_lat, e), out_w, out_dtype=F32,
               tm=_pick(b * l_lat, 1024), tn=_pick(d, 1024)).reshape(b, l_lat, d)
    return x_lat + g_l * o


def kernel(x, c, ctx, c_ctx, l0_norm_g, l0_ada_w, l0_ada_b, l0_in_w, l0_a_re, l0_a_im, l0_log_step, l0_b_re, l0_b_im, l0_c_re, l0_c_im, l0_d, l0_glu_w, l0_glu_b, l0_out_w, l1_norm_g, l1_ada_w, l1_ada_b, l1_mu, l1_in_w, l1_w0, l1_w1, l1_w2, l1_a0, l1_a1, l1_a2, l1_k_k, l1_k_a, l1_r_k, l1_ln_w, l1_ln_b, l1_out_w, final_norm_g):
    x_lat, x_ctx = _s5_layer(x, ctx, c, c_ctx, l0_norm_g, l0_ada_w, l0_ada_b, l0_in_w, l0_a_re,
                             l0_a_im, l0_log_step, l0_b_re, l0_b_im, l0_c_re, l0_c_im, l0_d,
                             l0_glu_w, l0_glu_b, l0_out_w)
    x_lat = _rwkv_layer(x_lat, x_ctx, c, c_ctx, l1_norm_g, l1_ada_w, l1_ada_b, l1_mu, l1_in_w,
                        l1_w0, l1_w1, l1_w2, l1_a0, l1_a1, l1_a2, l1_k_k, l1_k_a, l1_r_k,
                        l1_ln_w, l1_ln_b, l1_out_w)
    return _rmsnorm(x_lat, final_norm_g).astype(x.dtype)
```

```python
import functools

import jax
import jax.numpy as jnp
from jax import lax
from jax.experimental import pallas as pl
from jax.experimental.pallas import tpu as pltpu

F32 = jnp.float32
BF16 = jnp.bfloat16

GRID_W = 64
S5_GROUP = 16
S5_STATE = 64
S5_T = 16
RWKV_HEAD = 64
RWKV_CHUNK = 64
RMS_EPS = 1e-6
GN_EPS = 64e-5
NORM_EPS = 1e-12

VMEM_LIMIT_BYTES = 56 * 1024 * 1024


def _dot(a, b):
    return jnp.dot(a.astype(BF16), b.astype(BF16), preferred_element_type=F32)


def _dot_nt(a, b):
    return lax.dot_general(a.astype(BF16), b.astype(BF16), (((1,), (1,)), ((), ())),
                           preferred_element_type=F32)


def _dot_tn(a, b):
    return lax.dot_general(a.astype(BF16), b.astype(BF16), (((0,), (0,)), ((), ())),
                           preferred_element_type=F32)


def _split2(x):
    hi = x.astype(BF16)
    lo = (x - hi.astype(F32)).astype(BF16)
    return hi, lo


def _split3(x):
    p1 = x.astype(BF16)
    r1 = x - p1.astype(F32)
    p2 = r1.astype(BF16)
    p3 = (r1 - p2.astype(F32)).astype(BF16)
    return p1, p2, p3


def _mm_kernel(a_ref, b_ref, o_ref):
    o_ref[...] = jnp.dot(a_ref[...], b_ref[...],
                         preferred_element_type=F32).astype(o_ref.dtype)


def _mm_bias_kernel(a_ref, b_ref, bias_ref, o_ref):
    acc = jnp.dot(a_ref[...], b_ref[...], preferred_element_type=F32)
    o_ref[...] = (acc + bias_ref[...]).astype(o_ref.dtype)


def matmul(a, b, bias=None, *, out_dtype, tm, tn, name):
    m, k = a.shape
    k2, n = b.shape
    assert k == k2 and m % tm == 0 and n % tn == 0, (a.shape, b.shape, tm, tn)
    a = a.astype(BF16)
    b = b.astype(BF16)
    in_specs = [pl.BlockSpec((tm, k), lambda j, i: (i, 0)),
                pl.BlockSpec((k, tn), lambda j, i: (0, j))]
    args = [a, b]
    kern = _mm_kernel
    if bias is not None:
        in_specs.append(pl.BlockSpec((1, tn), lambda j, i: (0, j)))
        args.append(bias.astype(F32).reshape(1, n))
        kern = _mm_bias_kernel
    return pl.pallas_call(
        kern,
        grid=(n // tn, m // tm),
        in_specs=in_specs,
        out_specs=pl.BlockSpec((tm, tn), lambda j, i: (i, j)),
        out_shape=jax.ShapeDtypeStruct((m, n), out_dtype),
        name=name,
        compiler_params=pltpu.CompilerParams(
            dimension_semantics=("arbitrary", "arbitrary"),
            vmem_limit_bytes=VMEM_LIMIT_BYTES),
    )(*args)


def _pick(total, pref):
    t = min(pref, total)
    while total % t:
        t //= 2
    return t


def _s5_prep_kernel(are_ref, aim_ref, ls_ref, bre_ref, bim_ref, cre_ref, cim_ref,
                    w1_ref, ot_ref, ar_ref, ai_ref, *, gb):
    T = S5_T
    GC = S5_GROUP
    lane = lax.broadcasted_iota(jnp.int32, (1, 128), 1)
    fwd = lane < S5_STATE
    s_col = lax.broadcasted_iota(jnp.int32, (T, 1), 0).astype(F32)
    ri = lax.broadcasted_iota(jnp.int32, (T * GC, T * GC), 0) // GC
    ci = lax.broadcasted_iota(jnp.int32, (T * GC, T * GC), 1) // GC
    lane2 = lax.broadcasted_iota(jnp.int32, (1, 256), 1) % 128
    fwd2 = lane2 < S5_STATE
    half = float(T // 2)

    for g in range(gb):
        a_re = are_ref[g]
        a_im = aim_ref[g]
        dt = jnp.exp(ls_ref[g])
        x_re = dt * a_re
        x_im = dt * a_im

        def lam_pow(p):
            mag = jnp.exp(p * x_re)
            ang = p * x_im
            return mag * jnp.cos(ang), mag * jnp.sin(ang)

        l_re, l_im = lam_pow(jnp.ones((1, 1), F32))
        den = a_re * a_re + a_im * a_im
        nr, ni = l_re - 1.0, l_im
        f_re = (nr * a_re + ni * a_im) / den
        f_im = (ni * a_re - nr * a_im) / den
        b_re = bre_ref[g]
        b_im = bim_ref[g]
        bb_re = f_re * b_re - f_im * b_im
        bb_im = f_re * b_im + f_im * b_re
        c_re = cre_ref[g]
        c_im = cim_ref[g]

        def times_pow(v_re, v_im, p):
            p_re, p_im = lam_pow(p)
            o_re = v_re[None, :, :] * p_re[:, None, :] - v_im[None, :, :] * p_im[:, None, :]
            o_im = v_re[None, :, :] * p_im[:, None, :] + v_im[None, :, :] * p_re[:, None, :]
            return o_re.reshape(T * GC, 128), o_im.reshape(T * GC, 128)

        pb = jnp.where(fwd, half - s_col, s_col - half)
        pc = -pb
        bl_re, bl_im = times_pow(bb_re, bb_im, pb)
        cl_re, cl_im = times_pow(c_re, c_im, pc)
        lhs = jnp.concatenate([bl_re, bl_im], axis=1)
        rhs = jnp.concatenate([cl_re, -cl_im], axis=1)
        r_hi, r_lo = _split2(rhs)

        def dot3(l):
            l_hi, l_lo = _split2(l)
            dn = (((1,), (1,)), ((), ()))
            return (lax.dot_general(l_hi, r_hi, dn, preferred_element_type=F32)
                    + lax.dot_general(l_hi, r_lo, dn, preferred_element_type=F32)
                    + lax.dot_general(l_lo, r_hi, dn, preferred_element_type=F32))

        t_f = dot3(jnp.where(fwd2, lhs, 0.0))
        t_b = dot3(jnp.where(fwd2, 0.0, lhs))
        toep = jnp.where(ci >= ri, t_f, 0.0) + jnp.where(ri >= ci, t_b, 0.0)

        ps = jnp.where(fwd, (T - 1.0) - s_col, s_col)
        s_re, s_im = times_pow(bb_re, bb_im, ps)
        w1_ref[g] = jnp.concatenate([toep, s_re, s_im], axis=1).astype(w1_ref.dtype)

        po = jnp.where(fwd, s_col + 1.0, T - s_col)
        o_re, o_im = times_pow(c_re, c_im, po)
        ot_ref[g] = jnp.concatenate([o_re, -o_im], axis=1).astype(ot_ref.dtype)

        pw_re, pw_im = lam_pow(jnp.full((1, 1), float(T), F32))
        ar_ref[g] = pw_re
        ai_ref[g] = pw_im


def _s5_kernel(u_ref, w1_ref, ot_ref, ar_ref, ai_ref, y_ref,
               p_ref, hf_re, hf_im, hb_re, hb_im, *, nb, n_ctx, n_chunks):
    p_ref[...] = jnp.dot(u_ref[0], w1_ref[0], preferred_element_type=F32)
    ar = jnp.broadcast_to(ar_ref[0], (nb, 128))
    ai = jnp.broadcast_to(ai_ref[0], (nb, 128))
    fwd = lax.broadcasted_iota(jnp.int32, (nb, 128), 1) < S5_STATE

    def step(i, carry):
        h_re, h_im = carry
        cb = jnp.where(i < n_ctx, n_ctx - 1 - i, n_chunks - 1 - (i - n_ctx))
        rf = pl.multiple_of(i * nb, nb)
        rb = pl.multiple_of(cb * nb, nb)
        hf_re[pl.ds(rf, nb), :] = h_re
        hf_im[pl.ds(rf, nb), :] = h_im
        hb_re[pl.ds(rb, nb), :] = h_re
        hb_im[pl.ds(rb, nb), :] = h_im
        x_re = jnp.where(fwd, p_ref[pl.ds(rf, nb), 256:384], p_ref[pl.ds(rb, nb), 256:384])
        x_im = jnp.where(fwd, p_ref[pl.ds(rf, nb), 384:512], p_ref[pl.ds(rb, nb), 384:512])
        n_re = ar * h_re - ai * h_im + x_re
        n_im = ar * h_im + ai * h_re + x_im
        return n_re, n_im

    zero = jnp.zeros((nb, 128), F32)
    lax.fori_loop(0, n_chunks, step, (zero, zero))

    rows = p_ref.shape[0]
    fwd_all = lax.broadcasted_iota(jnp.int32, (rows, 128), 1) < S5_STATE
    h_all = jnp.concatenate([jnp.where(fwd_all, hf_re[...], hb_re[...]),
                             jnp.where(fwd_all, hf_im[...], hb_im[...])], axis=1)
    y = p_ref[:, 0:256] + _dot_nt(h_all, ot_ref[0])
    y_ref[0] = y.astype(y_ref.dtype)


def s5_operators(a_re, a_im, log_step, b_re, b_im, c_re, c_im):
    n_dir, g, n = a_re.shape
    gb = 8

    def lanes(x):
        return jnp.concatenate([x[0], x[1]], axis=-1).astype(F32)

    are = lanes(a_re).reshape(g, 1, 2 * n)
    aim = lanes(a_im).reshape(g, 1, 2 * n)
    ls = lanes(jnp.broadcast_to(log_step[:, :, None], (n_dir, g, n))).reshape(g, 1, 2 * n)
    bre = lanes(jnp.swapaxes(b_re, -1, -2))
    bim = lanes(jnp.swapaxes(b_im, -1, -2))
    cre = lanes(c_re)
    cim = lanes(c_im)
    row = pl.BlockSpec((gb, 1, 2 * n), lambda i: (i, 0, 0))
    mat = pl.BlockSpec((gb, S5_GROUP, 2 * n), lambda i: (i, 0, 0))
    tg = S5_T * S5_GROUP
    return pl.pallas_call(
        functools.partial(_s5_prep_kernel, gb=gb),
        grid=(g // gb,),
        in_specs=[row, row, row, mat, mat, mat, mat],
        out_specs=[pl.BlockSpec((gb, tg, tg + 4 * n), lambda i: (i, 0, 0)),
                   pl.BlockSpec((gb, tg, 4 * n), lambda i: (i, 0, 0)),
                   row, row],
        out_shape=[jax.ShapeDtypeStruct((g, tg, tg + 4 * n), BF16),
                   jax.ShapeDtypeStruct((g, tg, 4 * n), BF16),
                   jax.ShapeDtypeStruct((g, 1, 2 * n), F32),
                   jax.ShapeDtypeStruct((g, 1, 2 * n), F32)],
        name="s5_operators",
        compiler_params=pltpu.CompilerParams(dimension_semantics=("arbitrary",),
                                             vmem_limit_bytes=VMEM_LIMIT_BYTES),
    )(are, aim, ls, bre, bim, cre, cim)


def s5_scan(u_g, w1, ot, ar, ai, *, nb, n_ctx, n_chunks):
    g, rows, tg = u_g.shape
    assert rows == n_chunks * nb and tg == S5_T * S5_GROUP
    blk = lambda w: pl.BlockSpec((1,) + w, lambda i: (i, 0, 0))
    return pl.pallas_call(
        functools.partial(_s5_kernel, nb=nb, n_ctx=n_ctx, n_chunks=n_chunks),
        grid=(g,),
        in_specs=[blk((rows, tg)), blk(w1.shape[1:]), blk(ot.shape[1:]),
                  blk((1, 128)), blk((1, 128))],
        out_specs=blk((rows, tg)),
        out_shape=jax.ShapeDtypeStruct((g, rows, tg), BF16),
        scratch_shapes=[pltpu.VMEM((rows, w1.shape[2]), F32)]
                       + [pltpu.VMEM((rows, 128), F32)] * 4,
        name="s5_scan",
        compiler_params=pltpu.CompilerParams(dimension_semantics=("arbitrary",),
                                             vmem_limit_bytes=VMEM_LIMIT_BYTES),
    )(u_g, w1, ot, ar, ai)


def _rwkv_chunk_index(step, n_ctx, n_chunks):
    i = step % n_chunks
    back = jnp.where(i < n_ctx, n_ctx - 1 - i, n_chunks - 1 - (i - n_ctx))
    return jnp.where(step < n_chunks, i, back)


def _rwkv_kernel(r_ref, k_ref, v_ref, w_ref, a_ref, kk_ref, ka_ref, o_ref, s_ref,
                 *, n_ctx, n_chunks, n_pairs):
    C = RWKV_CHUNK
    step = pl.program_id(2)
    i = step % n_chunks
    is_fwd = step < n_chunks
    cidx = _rwkv_chunk_index(step, n_ctx, n_chunks)
    sgn = jnp.where(is_fwd, 1, -1)

    ri = lax.broadcasted_iota(jnp.int32, (128, 128), 0)
    ci = lax.broadcasted_iota(jnp.int32, (128, 128), 1)
    t_i = ri % C
    s_i = ci % C
    before = (s_i - t_i) * sgn
    strict = before < 0
    incl = before <= 0
    same_blk = (t_i // 16) == (s_i // 16)
    diag_m = strict & same_blk
    off_m = strict & jnp.logical_not(same_blk)
    lane_head = lax.broadcasted_iota(jnp.int32, (C, 128), 1) // RWKV_HEAD
    tr = lax.broadcasted_iota(jnp.int32, (C, C), 0)
    tc = lax.broadcasted_iota(jnp.int32, (C, C), 1)
    cum_m = jnp.where((tc - tr) * sgn <= 0, 1.0, 0.0).astype(BF16)
    head_ones = jnp.where((ri // RWKV_HEAD) == (ci // RWKV_HEAD), 1.0, 0.0).astype(BF16)

    @pl.when(i == 0)
    def _():
        s_ref[...] = jnp.zeros(s_ref.shape, F32)

    def stack(x):
        return jnp.concatenate([jnp.where(lane_head == 0, x, 0.0),
                                jnp.where(lane_head == 1, x, 0.0)], axis=0)

    r = r_ref[0].astype(F32)
    k = k_ref[0].astype(F32)
    v = v_ref[0].astype(F32)
    wp = w_ref[0, 0]
    ap = a_ref[0, 0]
    sp = jnp.maximum(-wp, 0.0) + jnp.log(1.0 + jnp.exp(-jnp.abs(wp)))
    lw = -jnp.exp(-sp - 0.5)
    a_rate = 1.0 / (1.0 + jnp.exp(-ap))
    kkr = k * kk_ref[...]
    kd = k * (1.0 + (a_rate - 1.0) * ka_ref[...])
    lc = sum(jnp.dot(cum_m, p, preferred_element_type=F32) for p in _split3(lw))
    l_mid = lc[C // 2:C // 2 + 1, :]
    l_end = jnp.where(is_fwd, lc[C - 1:C, :], lc[0:1, :])
    dl = lc - l_mid
    e_in = jnp.exp(dl)
    e_ex = jnp.exp(dl - lw)
    e_iv = jnp.exp(-dl)
    c_row = jnp.exp(l_mid)
    e2 = jnp.exp(l_end - l_mid)
    wc = jnp.exp(l_end)
    rt = r * e_in
    ar_iv = a_rate * e_iv
    kt = kd * e_iv

    pairs = range(n_pairs)

    def each(f, *lists):
        return [f(*args) for args in zip(*lists)]

    def cols(x):
        return [x[:, p * 128:(p + 1) * 128] for p in pairs]

    sq = each(lambda x: _split2(x * x), cols(kkr))
    ssq = each(lambda s: jnp.dot(s[0], head_ones, preferred_element_type=F32)
               + jnp.dot(s[1], head_ones, preferred_element_type=F32), sq)
    kk = each(lambda x, s: x / jnp.maximum(jnp.sqrt(s), NORM_EPS), cols(kkr), ssq)
    a2 = each(lambda x, e: stack(-x * e), kk, cols(e_ex))
    b2 = each(lambda x, e: stack(x * e), kk, cols(ar_iv))
    r2 = each(stack, cols(rt))
    k2 = each(stack, cols(kt))
    v2 = each(stack, cols(v))
    a4 = each(lambda a, r_, b, k_: _dot_nt(jnp.concatenate([a, r_], axis=0),
                                           jnp.concatenate([b, k_], axis=0)),
              a2, r2, b2, k2)
    ak = each(lambda m: jnp.where(strict, m[:2 * C, 2 * C:], 0.0), a4)
    rb = each(lambda m: jnp.where(incl, m[2 * C:, :2 * C], 0.0), a4)
    rk = each(lambda m: jnp.where(incl, m[2 * C:, 2 * C:], 0.0), a4)
    kv = each(lambda a, r_, v_: _dot(jnp.concatenate([a, r_], axis=0), v_), ak, rk, v2)
    ad = each(lambda m: jnp.where(diag_m, m[:2 * C, :2 * C], 0.0), a4)
    ao = each(lambda m: jnp.where(off_m, m[:2 * C, :2 * C], 0.0), a4)
    p2 = each(_dot, ad, ad)
    p4 = each(_dot, p2, p2)
    p3 = each(_dot, ad, p2)
    p8 = each(_dot, p4, p4)
    q1 = each(lambda a, b, c_: a + b + c_, ad, p2, p3)
    p12 = each(_dot, p4, p8)
    q2 = each(lambda a, b, c_: a + b + c_, p4, p8, p12)
    q12 = each(_dot, q1, q2)
    dm = each(lambda a, b, c_: a + b + c_, q1, q2, q12)
    n1 = each(lambda o, d_: o + _dot(d_, o), ao, dm)
    n2 = each(_dot, n1, n1)
    n3 = each(_dot, n1, n2)
    ns = each(lambda a, b, c_: a + b + c_, n1, n2, n3)
    tm = each(lambda n_, d_: n_ + d_ + _dot(n_, d_), ns, dm)
    x0 = each(lambda a, m: jnp.concatenate([a, m[:2 * C]], axis=1), a2, kv)
    x1 = each(lambda t_, x: x + _dot(t_, x), tm, x0)
    z = each(_dot, rb, x1)
    ca = each(lambda x, c_: x[:, :128] * c_, x1, cols(c_row))
    cr = each(lambda r_, z_, c_: (r_ + z_[:, :128]) * c_, r2, z, cols(c_row))
    yp = each(lambda z_, m: z_[:, 128:] + m[2 * C:], z, kv)
    lt = each(lambda ca_, x, v_: jnp.concatenate(
        [jnp.concatenate([ca_, x[:, 128:]], axis=1),
         jnp.concatenate([jnp.zeros_like(v_), v_], axis=1)], axis=0), ca, x1, v2)
    rt2 = each(lambda b, k_, e: jnp.concatenate([b * e, k_ * e], axis=0), b2, k2, cols(e2))
    mh = each(_dot_tn, lt, rt2)
    s_old = [s_ref[p] for p in pairs]
    y2 = each(lambda c_, s, y_: _dot_nt(c_, s) + y_, cr, s_old, yp)
    s_new = each(lambda s, w_, m: s * w_ + _dot(s, m[:128]) + m[128:], s_old, cols(wc), mh)
    for p in pairs:
        s_ref[p] = s_new[p]
    ys = [y_[:C] + y_[C:] for y_ in y2]

    y = jnp.concatenate(ys, axis=1)
    orow = pl.multiple_of(jnp.maximum(cidx - n_ctx, 0) * C, C)

    @pl.when(jnp.logical_and(cidx >= n_ctx, is_fwd))
    def _():
        o_ref[0, pl.ds(orow, C), :] = y

    @pl.when(jnp.logical_and(cidx >= n_ctx, jnp.logical_not(is_fwd)))
    def _():
        o_ref[0, pl.ds(orow, C), :] += y


def rwkv_scan(r, k, v, wpre, apre, k_k, k_a, *, n_ctx_tokens):
    b, l, e = r.shape
    C = RWKV_CHUNK
    pw = min(e, 1024)
    n_pairs = pw // 128
    n_chunks = l // C
    n_ctx = n_ctx_tokens // C
    l_out = l - n_ctx_tokens
    cix = functools.partial(_rwkv_chunk_index, n_ctx=n_ctx, n_chunks=n_chunks)
    seq = pl.BlockSpec((1, C, pw), lambda i, j, t: (i, cix(t), j))
    seq2 = pl.BlockSpec((1, 1, C, pw), lambda i, j, t: (t // n_chunks, i, cix(t), j))
    vec = pl.BlockSpec((1, pw), lambda i, j, t: (0, j))
    return pl.pallas_call(
        functools.partial(_rwkv_kernel, n_ctx=n_ctx, n_chunks=n_chunks, n_pairs=n_pairs),
        grid=(b, e // pw, 2 * n_chunks),
        in_specs=[seq, seq, seq, seq2, seq2, vec, vec],
        out_specs=pl.BlockSpec((1, l_out, pw), lambda i, j, t: (i, 0, j)),
        out_shape=jax.ShapeDtypeStruct((b, l_out, e), F32),
        scratch_shapes=[pltpu.VMEM((n_pairs, 128, 128), F32)],
        name="rwkv_scan",
        compiler_params=pltpu.CompilerParams(
            dimension_semantics=("arbitrary", "arbitrary", "arbitrary"),
            vmem_limit_bytes=VMEM_LIMIT_BYTES),
    )(r, k, v, wpre, apre, k_k.astype(F32).reshape(1, e), k_a.astype(F32).reshape(1, e))


def _rmsnorm(x, g):
    xf = x.astype(F32)
    y = xf * lax.rsqrt(jnp.mean(xf * xf, axis=-1, keepdims=True) + RMS_EPS)
    return y * g.astype(F32)


def _modulation(c, c_ctx, ada_w, ada_b):
    d = c.shape[-1]
    cond = jnp.concatenate([c, c_ctx[None]], axis=0)
    rows = cond.shape[0]
    pad = (-rows) % 16
    act = jnp.pad(jax.nn.silu(cond), ((0, pad), (0, 0)))
    m = matmul(act, ada_w, ada_b, out_dtype=F32, tm=rows + pad, tn=_pick(3 * d, 1024),
               name="mm_modulation")
    m = m[:rows]
    lat = [t[:, None, :] for t in jnp.split(m[:rows - 1], 3, axis=-1)]
    ctx = [t[None] for t in jnp.split(m[rows - 1:], 3, axis=-1)]
    return lat, ctx


def _q_shift(h):
    b, l, d = h.shape
    rows = l // GRID_W
    g = h.reshape(b, rows, GRID_W, d)
    q = d // 4
    left = jnp.pad(g[:, :, :-1, :q], ((0, 0), (0, 0), (1, 0), (0, 0)))
    right = jnp.pad(g[:, :, 1:, q:2 * q], ((0, 0), (0, 0), (0, 1), (0, 0)))
    up = jnp.pad(g[:, :-1, :, 2 * q:3 * q], ((0, 0), (1, 0), (0, 0), (0, 0)))
    down = jnp.pad(g[:, 1:, :, 3 * q:], ((0, 0), (0, 1), (0, 0), (0, 0)))
    return jnp.concatenate([left, right, up, down], axis=-1).reshape(b, l, d)


def _seq_shift(h):
    half = h.shape[-1] // 2
    prev = jnp.pad(h[:, :-1, :half], ((0, 0), (1, 0), (0, 0)))
    nxt = jnp.pad(h[:, 1:, half:], ((0, 0), (0, 1), (0, 0)))
    return jnp.concatenate([prev, nxt], axis=-1)


def _s5_layer(x_lat, x_ctx, c, c_ctx, norm_g, ada_w, ada_b, in_w, a_re, a_im, log_step,
              b_re, b_im, c_re, c_im, d_skip, glu_w, glu_b, out_w):
    b, l_lat, d = x_lat.shape
    l_ctx = x_ctx.shape[1]
    l = l_ctx + l_lat
    e = in_w.shape[1] // 2
    g = e // S5_GROUP
    (sh_l, sc_l, g_l), (sh_c, sc_c, g_c) = _modulation(c, c_ctx, ada_w, ada_b)
    h_lat = _rmsnorm(x_lat, norm_g) * (1.0 + sc_l) + sh_l
    h_ctx = _rmsnorm(x_ctx, norm_g) * (1.0 + sc_c) + sh_c
    h = jnp.concatenate([h_ctx, h_lat], axis=1).astype(BF16).reshape(b * l, d)
    m = b * l
    tm = _pick(m, 1024)
    uz = matmul(h, in_w, out_dtype=BF16, tm=tm, tn=_pick(2 * e, 1024), name="mm_s5_in")
    u, z = uz[:, :e], uz[:, e:]

    n_chunks = l // S5_T
    u_g = (u.reshape(b, n_chunks, S5_T, g, S5_GROUP).transpose(3, 1, 0, 2, 4)
           .reshape(g, n_chunks * b, S5_T * S5_GROUP))
    w1, ot, ar, ai = s5_operators(a_re, a_im, log_step, b_re, b_im, c_re, c_im)
    y_g = s5_scan(u_g, w1, ot, ar, ai, nb=b, n_ctx=l_ctx // S5_T, n_chunks=n_chunks)
    y_ssm = (y_g.reshape(g, n_chunks, b, S5_T, S5_GROUP).transpose(2, 1, 3, 0, 4)
             .reshape(m, e))

    y1 = jax.nn.gelu(y_ssm.astype(F32) + d_skip.astype(F32) * u.astype(F32)).astype(BF16)
    gl = matmul(y1, glu_w, glu_b, out_dtype=F32, tm=tm, tn=_pick(e, 1024), name="mm_s5_glu")
    y2 = (y1.astype(F32) * jax.nn.sigmoid(gl) * jax.nn.silu(z.astype(F32))).astype(BF16)
    o = matmul(y2, out_w, out_dtype=F32, tm=tm, tn=_pick(d, 1024),
               name="mm_s5_out").reshape(b, l, d)
    x_ctx = x_ctx + g_c * o[:, :l_ctx]
    x_lat = x_lat + g_l * o[:, l_ctx:]
    return x_lat, x_ctx


def _rwkv_layer(x_lat, x_ctx, c, c_ctx, norm_g, ada_w, ada_b, mu, in_w, w0, w1, w2, a0, a1, a2,
                k_k, k_a, r_k, ln_w, ln_b, out_w):
    b, l_lat, d = x_lat.shape
    l_ctx = x_ctx.shape[1]
    l = l_ctx + l_lat
    e = in_w.shape[2]
    heads = e // RWKV_HEAD
    m = b * l
    (sh_l, sc_l, g_l), (sh_c, sc_c, _) = _modulation(c, c_ctx, ada_w, ada_b)
    h_lat = _rmsnorm(x_lat, norm_g) * (1.0 + sc_l) + sh_l
    h_ctx = _rmsnorm(x_ctx, norm_g) * (1.0 + sc_c) + sh_c
    h = jnp.concatenate([h_ctx, h_lat], axis=1)
    delta = jnp.concatenate([_seq_shift(h_ctx), _q_shift(h_lat)], axis=1) - h
    mix = [(h + delta * mu[i]).astype(BF16).reshape(m, d) for i in range(mu.shape[0])]

    tm = _pick(m, 1024)
    tn = _pick(e, 1024)
    r, k, v, z = (matmul(mix[i], in_w[i], out_dtype=BF16, tm=tm, tn=tn, name="mm_rwkv_in")
                  for i in range(4))

    n_dir, _, lora = w1.shape
    lp = (-lora) % 128

    def lora_pre(xm, l1, l2, l0, act):
        l1c = jnp.concatenate([jnp.pad(l1[i], ((0, 0), (0, lp))) for i in range(n_dir)], axis=1)
        t = act(matmul(xm, l1c, out_dtype=F32, tm=tm, tn=l1c.shape[1], name="mm_lora_down"))
        outs = []
        for i in range(n_dir):
            ti = t[:, i * (lora + lp):(i + 1) * (lora + lp)]
            outs.append(matmul(ti, jnp.pad(l2[i], ((0, lp), (0, 0))), l0[i],
                               out_dtype=F32, tm=tm, tn=tn, name="mm_lora_up"))
        return jnp.stack(outs).reshape(n_dir, b, l, e)

    wpre = lora_pre(mix[4], w1, w2, w0, jnp.tanh)
    apre = lora_pre(mix[5], a1, a2, a0, lambda t: t)

    y_sum = rwkv_scan(r.reshape(b, l, e), k.reshape(b, l, e), v.reshape(b, l, e), wpre, apre,
                      k_k, k_a, n_ctx_tokens=l_ctx)

    def lat_heads(t):
        return t.reshape(b, l, e)[:, l_ctx:].astype(F32).reshape(b, l_lat, heads, RWKV_HEAD)

    y4 = y_sum.reshape(b, l_lat, heads, RWKV_HEAD)
    mean = jnp.mean(y4, axis=-1, keepdims=True)
    var = jnp.mean(jnp.square(y4 - mean), axis=-1, keepdims=True)
    yn = ((y4 - mean) * lax.rsqrt(var + GN_EPS)).reshape(b, l_lat, e)
    yn = yn * ln_w.astype(F32) + ln_b.astype(F32)
    ar_sum = (jax.nn.sigmoid(apre[0]) + jax.nn.sigmoid(apre[1]))[:, l_ctx:]
    k_lat = k.reshape(b, l, e)[:, l_ctx:].astype(F32)
    k_sum = (k_lat * (2.0 + (ar_sum - 2.0) * k_a.astype(F32))).reshape(b, l_lat, heads, RWKV_HEAD)
    bonus = jnp.sum(lat_heads(r) * k_sum * r_k.astype(F32), axis=-1, keepdims=True) * lat_heads(v)
    z_lat = z.reshape(b, l, e)[:, l_ctx:].astype(F32)
    yg = ((yn + bonus.reshape(b, l_lat, e)) * jax.nn.silu(z_lat)).astype(BF16)
    o = matmul(yg.reshape(b * l_lat, e), out_w, out_dtype=F32,
               tm=_pick(b * l_lat, 1024), tn=_pick(d, 1024), name="mm_rwkv_out").reshape(b, l_lat, d)
    return x_lat + g_l * o


def kernel(x, c, ctx, c_ctx, l0_norm_g, l0_ada_w, l0_ada_b, l0_in_w, l0_a_re, l0_a_im, l0_log_step, l0_b_re, l0_b_im, l0_c_re, l0_c_im, l0_d, l0_glu_w, l0_glu_b, l0_out_w, l1_norm_g, l1_ada_w, l1_ada_b, l1_mu, l1_in_w, l1_w0, l1_w1, l1_w2, l1_a0, l1_a1, l1_a2, l1_k_k, l1_k_a, l1_r_k, l1_ln_w, l1_ln_b, l1_out_w, final_norm_g):
    x_lat, x_ctx = _s5_layer(x, ctx, c, c_ctx, l0_norm_g, l0_ada_w, l0_ada_b, l0_in_w, l0_a_re,
                             l0_a_im, l0_log_step, l0_b_re, l0_b_im, l0_c_re, l0_c_im, l0_d,
                             l0_glu_w, l0_glu_b, l0_out_w)
    x_lat = _rwkv_layer(x_lat, x_ctx, c, c_ctx, l1_norm_g, l1_ada_w, l1_ada_b, l1_mu, l1_in_w,
                        l1_w0, l1_w1, l1_w2, l1_a0, l1_a1, l1_a2, l1_k_k, l1_k_a, l1_r_k,
                        l1_ln_w, l1_ln_b, l1_out_w)
    return _rmsnorm(x_lat, final_norm_g).astype(x.dtype)
```

```python
import functools

import jax
import jax.numpy as jnp
from jax import lax
from jax.experimental import pallas as pl
from jax.experimental.pallas import tpu as pltpu

F32 = jnp.float32
BF16 = jnp.bfloat16

GRID_W = 64
S5_GROUP = 16
S5_STATE = 64
S5_T = 16
RWKV_HEAD = 64
RWKV_CHUNK = 64
RMS_EPS = 1e-6
GN_EPS = 64e-5
NORM_EPS = 1e-12

VMEM_LIMIT_BYTES = 56 * 1024 * 1024


def _dot(a, b):
    return jnp.dot(a.astype(BF16), b.astype(BF16), preferred_element_type=F32)


def _dot_nt(a, b):
    return lax.dot_general(a.astype(BF16), b.astype(BF16), (((1,), (1,)), ((), ())),
                           preferred_element_type=F32)


def _dot_tn(a, b):
    return lax.dot_general(a.astype(BF16), b.astype(BF16), (((0,), (0,)), ((), ())),
                           preferred_element_type=F32)


def _split2(x):
    hi = x.astype(BF16)
    lo = (x - hi.astype(F32)).astype(BF16)
    return hi, lo


def _split3(x):
    p1 = x.astype(BF16)
    r1 = x - p1.astype(F32)
    p2 = r1.astype(BF16)
    p3 = (r1 - p2.astype(F32)).astype(BF16)
    return p1, p2, p3


def _mm_kernel(a_ref, b_ref, o_ref):
    o_ref[...] = jnp.dot(a_ref[...], b_ref[...],
                         preferred_element_type=F32).astype(o_ref.dtype)


def _mm_bias_kernel(a_ref, b_ref, bias_ref, o_ref):
    acc = jnp.dot(a_ref[...], b_ref[...], preferred_element_type=F32)
    o_ref[...] = (acc + bias_ref[...]).astype(o_ref.dtype)


def matmul(a, b, bias=None, *, out_dtype, tm, tn, name):
    m, k = a.shape
    k2, n = b.shape
    assert k == k2 and m % tm == 0 and n % tn == 0, (a.shape, b.shape, tm, tn)
    a = a.astype(BF16)
    b = b.astype(BF16)
    in_specs = [pl.BlockSpec((tm, k), lambda j, i: (i, 0)),
                pl.BlockSpec((k, tn), lambda j, i: (0, j))]
    args = [a, b]
    kern = _mm_kernel
    if bias is not None:
        in_specs.append(pl.BlockSpec((1, tn), lambda j, i: (0, j)))
        args.append(bias.astype(F32).reshape(1, n))
        kern = _mm_bias_kernel
    return pl.pallas_call(
        kern,
        grid=(n // tn, m // tm),
        in_specs=in_specs,
        out_specs=pl.BlockSpec((tm, tn), lambda j, i: (i, j)),
        out_shape=jax.ShapeDtypeStruct((m, n), out_dtype),
        name=name,
        compiler_params=pltpu.CompilerParams(
            dimension_semantics=("arbitrary", "arbitrary"),
            vmem_limit_bytes=VMEM_LIMIT_BYTES),
    )(*args)


def _pick(total, pref):
    t = min(pref, total)
    while total % t:
        t //= 2
    return t


def _s5_prep_kernel(are_ref, aim_ref, ls_ref, bre_ref, bim_ref, cre_ref, cim_ref,
                    w1_ref, ot_ref, ar_ref, ai_ref, *, gb):
    T = S5_T
    GC = S5_GROUP
    lane = lax.broadcasted_iota(jnp.int32, (1, 128), 1)
    fwd = lane < S5_STATE
    s_col = lax.broadcasted_iota(jnp.int32, (T, 1), 0).astype(F32)
    ri = lax.broadcasted_iota(jnp.int32, (T * GC, T * GC), 0) % T
    ci = lax.broadcasted_iota(jnp.int32, (T * GC, T * GC), 1) % T
    lane2 = lax.broadcasted_iota(jnp.int32, (1, 256), 1) % 128
    fwd2 = lane2 < S5_STATE
    half = float(T // 2)

    for g in range(gb):
        a_re = are_ref[g]
        a_im = aim_ref[g]
        dt = jnp.exp(ls_ref[g])
        x_re = dt * a_re
        x_im = dt * a_im

        def lam_pow(p):
            mag = jnp.exp(p * x_re)
            ang = p * x_im
            return mag * jnp.cos(ang), mag * jnp.sin(ang)

        l_re, l_im = lam_pow(jnp.ones((1, 1), F32))
        den = a_re * a_re + a_im * a_im
        nr, ni = l_re - 1.0, l_im
        f_re = (nr * a_re + ni * a_im) / den
        f_im = (ni * a_re - nr * a_im) / den
        b_re = bre_ref[g]
        b_im = bim_ref[g]
        bb_re = f_re * b_re - f_im * b_im
        bb_im = f_re * b_im + f_im * b_re
        c_re = cre_ref[g]
        c_im = cim_ref[g]

        def times_pow(v_re, v_im, p):
            p_re, p_im = lam_pow(p)
            o_re = v_re[:, None, :] * p_re[None, :, :] - v_im[:, None, :] * p_im[None, :, :]
            o_im = v_re[:, None, :] * p_im[None, :, :] + v_im[:, None, :] * p_re[None, :, :]
            return o_re.reshape(GC * T, 128), o_im.reshape(GC * T, 128)

        pb = jnp.where(fwd, half - s_col, s_col - half)
        pc = -pb
        bl_re, bl_im = times_pow(bb_re, bb_im, pb)
        cl_re, cl_im = times_pow(c_re, c_im, pc)
        lhs = jnp.concatenate([bl_re, bl_im], axis=1)
        rhs = jnp.concatenate([cl_re, -cl_im], axis=1)
        r_hi, r_lo = _split2(rhs)

        def dot3(l):
            l_hi, l_lo = _split2(l)
            dn = (((1,), (1,)), ((), ()))
            return (lax.dot_general(l_hi, r_hi, dn, preferred_element_type=F32)
                    + lax.dot_general(l_hi, r_lo, dn, preferred_element_type=F32)
                    + lax.dot_general(l_lo, r_hi, dn, preferred_element_type=F32))

        t_f = dot3(jnp.where(fwd2, lhs, 0.0))
        t_b = dot3(jnp.where(fwd2, 0.0, lhs))
        toep = jnp.where(ci >= ri, t_f, 0.0) + jnp.where(ri >= ci, t_b, 0.0)

        ps = jnp.where(fwd, (T - 1.0) - s_col, s_col)
        s_re, s_im = times_pow(bb_re, bb_im, ps)
        w1_ref[g] = jnp.concatenate([toep, s_re, s_im], axis=1).astype(w1_ref.dtype)

        po = jnp.where(fwd, s_col + 1.0, T - s_col)
        o_re, o_im = times_pow(c_re, c_im, po)
        ot_ref[g] = jnp.concatenate([o_re, -o_im], axis=1).astype(ot_ref.dtype)

        pw_re, pw_im = lam_pow(jnp.full((1, 1), float(T), F32))
        ar_ref[g] = pw_re
        ai_ref[g] = pw_im


def _s5_kernel(u_ref, w1_ref, ot_ref, ar_ref, ai_ref, y_ref,
               p_ref, hf_re, hf_im, hb_re, hb_im, *, nb, n_ctx, n_chunks):
    p_ref[...] = jnp.dot(u_ref[0], w1_ref[0], preferred_element_type=F32)
    ar = jnp.broadcast_to(ar_ref[0], (nb, 128))
    ai = jnp.broadcast_to(ai_ref[0], (nb, 128))
    fwd = lax.broadcasted_iota(jnp.int32, (nb, 128), 1) < S5_STATE

    def step(i, carry):
        h_re, h_im = carry
        cb = jnp.where(i < n_ctx, n_ctx - 1 - i, n_chunks - 1 - (i - n_ctx))
        rf = pl.multiple_of(i * nb, nb)
        rb = pl.multiple_of(cb * nb, nb)
        hf_re[pl.ds(rf, nb), :] = h_re
        hf_im[pl.ds(rf, nb), :] = h_im
        hb_re[pl.ds(rb, nb), :] = h_re
        hb_im[pl.ds(rb, nb), :] = h_im
        x_re = jnp.where(fwd, p_ref[pl.ds(rf, nb), 256:384], p_ref[pl.ds(rb, nb), 256:384])
        x_im = jnp.where(fwd, p_ref[pl.ds(rf, nb), 384:512], p_ref[pl.ds(rb, nb), 384:512])
        n_re = ar * h_re - ai * h_im + x_re
        n_im = ar * h_im + ai * h_re + x_im
        return n_re, n_im

    zero = jnp.zeros((nb, 128), F32)
    lax.fori_loop(0, n_chunks, step, (zero, zero))

    rows = p_ref.shape[0]
    fwd_all = lax.broadcasted_iota(jnp.int32, (rows, 128), 1) < S5_STATE
    h_all = jnp.concatenate([jnp.where(fwd_all, hf_re[...], hb_re[...]),
                             jnp.where(fwd_all, hf_im[...], hb_im[...])], axis=1)
    y = p_ref[:, 0:256] + _dot_nt(h_all, ot_ref[0])
    y_ref[0] = y.astype(y_ref.dtype)


def s5_operators(a_re, a_im, log_step, b_re, b_im, c_re, c_im):
    n_dir, g, n = a_re.shape
    gb = 8

    def lanes(x):
        return jnp.concatenate([x[0], x[1]], axis=-1).astype(F32)

    are = lanes(a_re).reshape(g, 1, 2 * n)
    aim = lanes(a_im).reshape(g, 1, 2 * n)
    ls = lanes(jnp.broadcast_to(log_step[:, :, None], (n_dir, g, n))).reshape(g, 1, 2 * n)
    bre = lanes(jnp.swapaxes(b_re, -1, -2))
    bim = lanes(jnp.swapaxes(b_im, -1, -2))
    cre = lanes(c_re)
    cim = lanes(c_im)
    row = pl.BlockSpec((gb, 1, 2 * n), lambda i: (i, 0, 0))
    mat = pl.BlockSpec((gb, S5_GROUP, 2 * n), lambda i: (i, 0, 0))
    tg = S5_T * S5_GROUP
    return pl.pallas_call(
        functools.partial(_s5_prep_kernel, gb=gb),
        grid=(g // gb,),
        in_specs=[row, row, row, mat, mat, mat, mat],
        out_specs=[pl.BlockSpec((gb, tg, tg + 4 * n), lambda i: (i, 0, 0)),
                   pl.BlockSpec((gb, tg, 4 * n), lambda i: (i, 0, 0)),
                   row, row],
        out_shape=[jax.ShapeDtypeStruct((g, tg, tg + 4 * n), BF16),
                   jax.ShapeDtypeStruct((g, tg, 4 * n), BF16),
                   jax.ShapeDtypeStruct((g, 1, 2 * n), F32),
                   jax.ShapeDtypeStruct((g, 1, 2 * n), F32)],
        name="s5_operators",
        compiler_params=pltpu.CompilerParams(dimension_semantics=("arbitrary",),
                                             vmem_limit_bytes=VMEM_LIMIT_BYTES),
    )(are, aim, ls, bre, bim, cre, cim)


def s5_scan(u_g, w1, ot, ar, ai, *, nb, n_ctx, n_chunks):
    g, rows, tg = u_g.shape
    assert rows == n_chunks * nb and tg == S5_T * S5_GROUP
    blk = lambda w: pl.BlockSpec((1,) + w, lambda i: (i, 0, 0))
    return pl.pallas_call(
        functools.partial(_s5_kernel, nb=nb, n_ctx=n_ctx, n_chunks=n_chunks),
        grid=(g,),
        in_specs=[blk((rows, tg)), blk(w1.shape[1:]), blk(ot.shape[1:]),
                  blk((1, 128)), blk((1, 128))],
        out_specs=blk((rows, tg)),
        out_shape=jax.ShapeDtypeStruct((g, rows, tg), BF16),
        scratch_shapes=[pltpu.VMEM((rows, w1.shape[2]), F32)]
                       + [pltpu.VMEM((rows, 128), F32)] * 4,
        name="s5_scan",
        compiler_params=pltpu.CompilerParams(dimension_semantics=("arbitrary",),
                                             vmem_limit_bytes=VMEM_LIMIT_BYTES),
    )(u_g, w1, ot, ar, ai)


def _eye(n):
    r = lax.broadcasted_iota(jnp.int32, (n, n), 0)
    c = lax.broadcasted_iota(jnp.int32, (n, n), 1)
    return jnp.where(r == c, 1.0, 0.0).astype(BF16)


def _to_groups_kernel(u_ref, o_ref, *, nb, n_j):
    g = o_ref.shape[0]
    eye = _eye(g)
    for b in range(nb):
        x = u_ref[b]
        rows = jnp.concatenate([x[:, j * g:(j + 1) * g] for j in range(n_j)], axis=0)
        o_ref[:, b, :] = _dot_nt(eye, rows).astype(o_ref.dtype)


def _from_groups_kernel(y_ref, o_ref, *, nb, n_j):
    g = y_ref.shape[0]
    t = o_ref.shape[1]
    eye = _eye(y_ref.shape[2])
    for b in range(nb):
        cols = _dot_nt(eye, y_ref[:, b, :]).astype(o_ref.dtype)
        for i in range(n_j):
            o_ref[b, :, i * g:(i + 1) * g] = cols[i * t:(i + 1) * t, :]


def to_groups(uz, *, n_chunks):
    b, l, e2 = uz.shape
    e = e2 // 2
    g = e // S5_GROUP
    return pl.pallas_call(
        functools.partial(_to_groups_kernel, nb=b, n_j=S5_GROUP),
        grid=(n_chunks,),
        in_specs=[pl.BlockSpec((b, S5_T, e), lambda c: (0, c, 0))],
        out_specs=pl.BlockSpec((g, b, S5_T * S5_GROUP), lambda c: (0, c, 0)),
        out_shape=jax.ShapeDtypeStruct((g, n_chunks * b, S5_T * S5_GROUP), BF16),
        name="s5_to_groups",
        compiler_params=pltpu.CompilerParams(dimension_semantics=("arbitrary",),
                                             vmem_limit_bytes=VMEM_LIMIT_BYTES),
    )(uz)


def from_groups(y_g, *, nb, n_chunks):
    g, rows, tg = y_g.shape
    e = g * S5_GROUP
    return pl.pallas_call(
        functools.partial(_from_groups_kernel, nb=nb, n_j=S5_GROUP),
        grid=(n_chunks,),
        in_specs=[pl.BlockSpec((g, nb, tg), lambda c: (0, c, 0))],
        out_specs=pl.BlockSpec((nb, S5_T, e), lambda c: (0, c, 0)),
        out_shape=jax.ShapeDtypeStruct((nb, n_chunks * S5_T, e), BF16),
        name="s5_from_groups",
        compiler_params=pltpu.CompilerParams(dimension_semantics=("arbitrary",),
                                             vmem_limit_bytes=VMEM_LIMIT_BYTES),
    )(y_g)


def _rwkv_chunk_index(step, n_ctx, n_chunks):
    i = step % n_chunks
    back = jnp.where(i < n_ctx, n_ctx - 1 - i, n_chunks - 1 - (i - n_ctx))
    return jnp.where(step < n_chunks, i, back)


def _rwkv_kernel(r_ref, k_ref, v_ref, z_ref, tw_ref, ta_ref, w2_ref, a2_ref, w0_ref, a0_ref,
                 kk_ref, ka_ref, rk_ref, lnw_ref, lnb_ref, o_ref, s_ref, acc_ref,
                 *, n_ctx, n_chunks, n_pairs):
    C = RWKV_CHUNK
    step = pl.program_id(2)
    i = step % n_chunks
    is_fwd = step < n_chunks
    cidx = _rwkv_chunk_index(step, n_ctx, n_chunks)
    sgn = jnp.where(is_fwd, 1, -1)

    ri = lax.broadcasted_iota(jnp.int32, (128, 128), 0)
    ci = lax.broadcasted_iota(jnp.int32, (128, 128), 1)
    t_i = ri % C
    s_i = ci % C
    before = (s_i - t_i) * sgn
    strict = before < 0
    incl = before <= 0
    same_blk = (t_i // 16) == (s_i // 16)
    diag_m = strict & same_blk
    off_m = strict & jnp.logical_not(same_blk)
    lane_head = lax.broadcasted_iota(jnp.int32, (C, 128), 1) // RWKV_HEAD
    tr = lax.broadcasted_iota(jnp.int32, (C, C), 0)
    tc = lax.broadcasted_iota(jnp.int32, (C, C), 1)
    cum_m = jnp.where((tc - tr) * sgn <= 0, 1.0, 0.0).astype(BF16)
    head_ones = jnp.where((ri // RWKV_HEAD) == (ci // RWKV_HEAD), 1.0, 0.0).astype(BF16)

    @pl.when(i == 0)
    def _():
        s_ref[...] = jnp.zeros(s_ref.shape, F32)

    def stack(x):
        return jnp.concatenate([jnp.where(lane_head == 0, x, 0.0),
                                jnp.where(lane_head == 1, x, 0.0)], axis=0)

    r = r_ref[0].astype(F32)
    k = k_ref[0].astype(F32)
    v = v_ref[0].astype(F32)
    d = jnp.where(is_fwd, 0, 1)
    wp = w0_ref[pl.ds(d, 1), :] + _dot(jnp.tanh(tw_ref[0, 0]), w2_ref[d])
    ap_f = a0_ref[0:1, :] + _dot(ta_ref[0, 0], a2_ref[0])
    ap_b = a0_ref[1:2, :] + _dot(ta_ref[1, 0], a2_ref[1])
    sp = jnp.maximum(-wp, 0.0) + jnp.log(1.0 + jnp.exp(-jnp.abs(wp)))
    lw = -jnp.exp(-sp - 0.5)
    ar_f = 1.0 / (1.0 + jnp.exp(-ap_f))
    ar_b = 1.0 / (1.0 + jnp.exp(-ap_b))
    a_rate = jnp.where(is_fwd, ar_f, ar_b)
    k_a = ka_ref[...]
    kkr = k * kk_ref[...]
    kd = k * (1.0 + (a_rate - 1.0) * k_a)
    lc = sum(jnp.dot(cum_m, p, preferred_element_type=F32) for p in _split3(lw))
    l_mid = lc[C // 2:C // 2 + 1, :]
    l_end = jnp.where(is_fwd, lc[C - 1:C, :], lc[0:1, :])
    dl = lc - l_mid
    e_in = jnp.exp(dl)
    e_ex = jnp.exp(dl - lw)
    e_iv = jnp.exp(-dl)
    c_row = jnp.exp(l_mid)
    e2 = jnp.exp(l_end - l_mid)
    wc = jnp.exp(l_end)
    rt = r * e_in
    ar_iv = a_rate * e_iv
    kt = kd * e_iv

    pairs = range(n_pairs)

    def each(f, *lists):
        return [f(*args) for args in zip(*lists)]

    def cols(x):
        return [x[:, p * 128:(p + 1) * 128] for p in pairs]

    sq = each(lambda x: _split2(x * x), cols(kkr))
    ssq = each(lambda s: jnp.dot(s[0], head_ones, preferred_element_type=F32)
               + jnp.dot(s[1], head_ones, preferred_element_type=F32), sq)
    kk = each(lambda x, s: x / jnp.maximum(jnp.sqrt(s), NORM_EPS), cols(kkr), ssq)
    a2 = each(lambda x, e: stack(-x * e), kk, cols(e_ex))
    b2 = each(lambda x, e: stack(x * e), kk, cols(ar_iv))
    r2 = each(stack, cols(rt))
    k2 = each(stack, cols(kt))
    v2 = each(stack, cols(v))
    a4 = each(lambda a, r_, b, k_: _dot_nt(jnp.concatenate([a, r_], axis=0),
                                           jnp.concatenate([b, k_], axis=0)),
              a2, r2, b2, k2)
    ak = each(lambda m: jnp.where(strict, m[:2 * C, 2 * C:], 0.0), a4)
    rb = each(lambda m: jnp.where(incl, m[2 * C:, :2 * C], 0.0), a4)
    rk = each(lambda m: jnp.where(incl, m[2 * C:, 2 * C:], 0.0), a4)
    akv = each(_dot, ak, v2)
    ad = each(lambda m: jnp.where(diag_m, m[:2 * C, :2 * C], 0.0), a4)
    ao = each(lambda m: jnp.where(off_m, m[:2 * C, :2 * C], 0.0), a4)
    wide = lambda l, *rs: _dot(l, jnp.concatenate(rs, axis=1))
    p2 = each(_dot, ad, ad)
    p43 = each(lambda p, a: wide(p, p, a), p2, ad)
    p4 = [m[:, :128] for m in p43]
    s3 = each(lambda a, p, m: a + p + m[:, 128:], ad, p2, p43)
    p8s = each(lambda p, s: wide(p, p, s), p4, s3)
    p8 = [m[:, :128] for m in p8s]
    s7 = each(lambda s, p, m: s + p + m[:, 128:], s3, p4, p8s)
    dm = each(lambda s, p: s + p + _dot(p, s), s7, p8)
    x0 = each(lambda a, m: jnp.concatenate([a, m], axis=1), a2, akv)
    dx = each(lambda d_, o, x: wide(d_, o, x), dm, ao, x0)
    n1 = each(lambda o, m: o + m[:, :128], ao, dx)
    xa = each(lambda x, m: x + m[:, 128:], x0, dx)
    xc = each(lambda n_, x: x + _dot(n_, x), n1, xa)
    x1 = each(lambda n_, x: x + _dot(n_, _dot(n_, x)), n1, xc)
    zv = each(lambda v_: jnp.concatenate([jnp.zeros_like(v_), v_], axis=1), v2)
    zr = each(lambda rb_, rk_, x, zv_: _dot(jnp.concatenate([rb_, rk_], axis=1),
                                            jnp.concatenate([x, zv_], axis=0)), rb, rk, x1, zv)
    ca = each(lambda x, c_: x[:, :128] * c_, x1, cols(c_row))
    cr = each(lambda r_, z_, c_: (r_ + z_[:, :128]) * c_, r2, zr, cols(c_row))
    yp = [z_[:, 128:] for z_ in zr]
    lt = each(lambda ca_, x, zv_: jnp.concatenate(
        [jnp.concatenate([ca_, x[:, 128:]], axis=1), zv_], axis=0), ca, x1, zv)
    rt2 = each(lambda b, k_, e: jnp.concatenate([b * e, k_ * e], axis=0), b2, k2, cols(e2))
    mh = each(_dot_tn, lt, rt2)
    s_old = [s_ref[p] for p in pairs]
    y2 = each(lambda c_, s, y_: _dot_nt(c_, s) + y_, cr, s_old, yp)
    s_new = each(lambda s, w_, m: s * w_ + _dot(s, m[:128]) + m[128:], s_old, cols(wc), mh)
    for p in pairs:
        s_ref[p] = s_new[p]
    ys = [y_[:C] + y_[C:] for y_ in y2]

    y = jnp.concatenate(ys, axis=1)
    orow = pl.multiple_of(jnp.maximum(cidx - n_ctx, 0) * C, C)

    @pl.when(jnp.logical_and(cidx >= n_ctx, is_fwd))
    def _():
        acc_ref[pl.ds(orow, C), :] = y

    @pl.when(jnp.logical_and(cidx >= n_ctx, jnp.logical_not(is_fwd)))
    def _():
        y_sum = acc_ref[pl.ds(orow, C), :] + y

        def head_sum(x):
            outs = []
            for xp in cols(x):
                hi, lo = _split2(xp)
                outs.append(jnp.dot(hi, head_ones, preferred_element_type=F32)
                            + jnp.dot(lo, head_ones, preferred_element_type=F32))
            return jnp.concatenate(outs, axis=1)

        inv = 1.0 / RWKV_HEAD
        dev = y_sum - head_sum(y_sum) * inv
        yn = dev * lax.rsqrt(head_sum(dev * dev) * inv + GN_EPS)
        yn = yn * lnw_ref[...] + lnb_ref[...]
        k_sum = k * (2.0 + (ar_f + ar_b - 2.0) * k_a)
        bonus = head_sum(r * k_sum * rk_ref[...]) * v
        zg = z_ref[0].astype(F32)
        o_ref[0] = ((yn + bonus) * (zg / (1.0 + jnp.exp(-zg)))).astype(o_ref.dtype)


def rwkv_scan(r, k, v, z, tw, ta, w2, a2, w0, a0, k_k, k_a, r_k, ln_w, ln_b, *, n_ctx_tokens):
    b, l, e = r.shape
    rank = tw.shape[-1]
    C = RWKV_CHUNK
    pw = min(e, 2048)
    n_pairs = pw // 128
    n_chunks = l // C
    n_ctx = n_ctx_tokens // C
    l_out = l - n_ctx_tokens
    cix = functools.partial(_rwkv_chunk_index, n_ctx=n_ctx, n_chunks=n_chunks)

    def out_chunk(t):
        c = cix(t)
        return jnp.where(jnp.logical_and(t >= n_chunks, c >= n_ctx), c, n_chunks - 1) - n_ctx

    seq = pl.BlockSpec((1, C, pw), lambda i, j, t: (i, cix(t), j))
    low_d = pl.BlockSpec((1, 1, C, rank), lambda i, j, t: (t // n_chunks, i, cix(t), 0))
    low_2 = pl.BlockSpec((2, 1, C, rank), lambda i, j, t: (0, i, cix(t), 0))
    up = pl.BlockSpec((2, rank, pw), lambda i, j, t: (0, 0, j))
    vec2 = pl.BlockSpec((2, pw), lambda i, j, t: (0, j))
    vec = pl.BlockSpec((1, pw), lambda i, j, t: (0, j))
    row = lambda x: x.astype(F32).reshape(1, e)
    return pl.pallas_call(
        functools.partial(_rwkv_kernel, n_ctx=n_ctx, n_chunks=n_chunks, n_pairs=n_pairs),
        grid=(b, e // pw, 2 * n_chunks),
        in_specs=[seq, seq, seq, seq, low_d, low_2, up, up, vec2, vec2, vec, vec, vec, vec, vec],
        out_specs=pl.BlockSpec((1, C, pw), lambda i, j, t: (i, out_chunk(t), j)),
        out_shape=jax.ShapeDtypeStruct((b, l_out, e), BF16),
        scratch_shapes=[pltpu.VMEM((n_pairs, 128, 128), F32),
                        pltpu.VMEM((l_out, pw), F32)],
        name="rwkv_scan",
        compiler_params=pltpu.CompilerParams(
            dimension_semantics=("arbitrary", "arbitrary", "arbitrary"),
            vmem_limit_bytes=VMEM_LIMIT_BYTES),
    )(r, k, v, z, tw, ta, w2.astype(BF16), a2.astype(BF16), w0.astype(F32), a0.astype(F32),
      row(k_k), row(k_a), row(r_k), row(ln_w), row(ln_b))


def _rmsnorm(x, g):
    xf = x.astype(F32)
    y = xf * lax.rsqrt(jnp.mean(xf * xf, axis=-1, keepdims=True) + RMS_EPS)
    return y * g.astype(F32)


def _modulation(c, c_ctx, ada_w, ada_b):
    d = c.shape[-1]
    cond = jnp.concatenate([c, c_ctx[None]], axis=0)
    rows = cond.shape[0]
    pad = (-rows) % 16
    act = jnp.pad(jax.nn.silu(cond), ((0, pad), (0, 0)))
    m = matmul(act, ada_w, ada_b, out_dtype=F32, tm=rows + pad, tn=_pick(3 * d, 1024),
               name="mm_modulation")
    m = m[:rows]
    lat = [t[:, None, :] for t in jnp.split(m[:rows - 1], 3, axis=-1)]
    ctx = [t[None] for t in jnp.split(m[rows - 1:], 3, axis=-1)]
    return lat, ctx


def _q_shift(h):
    b, l, d = h.shape
    rows = l // GRID_W
    g = h.reshape(b, rows, GRID_W, d)
    q = d // 4
    left = jnp.pad(g[:, :, :-1, :q], ((0, 0), (0, 0), (1, 0), (0, 0)))
    right = jnp.pad(g[:, :, 1:, q:2 * q], ((0, 0), (0, 0), (0, 1), (0, 0)))
    up = jnp.pad(g[:, :-1, :, 2 * q:3 * q], ((0, 0), (1, 0), (0, 0), (0, 0)))
    down = jnp.pad(g[:, 1:, :, 3 * q:], ((0, 0), (0, 1), (0, 0), (0, 0)))
    return jnp.concatenate([left, right, up, down], axis=-1).reshape(b, l, d)


def _seq_shift(h):
    half = h.shape[-1] // 2
    prev = jnp.pad(h[:, :-1, :half], ((0, 0), (1, 0), (0, 0)))
    nxt = jnp.pad(h[:, 1:, half:], ((0, 0), (0, 1), (0, 0)))
    return jnp.concatenate([prev, nxt], axis=-1)


def _s5_layer(x_lat, x_ctx, c, c_ctx, norm_g, ada_w, ada_b, in_w, a_re, a_im, log_step,
              b_re, b_im, c_re, c_im, d_skip, glu_w, glu_b, out_w):
    b, l_lat, d = x_lat.shape
    l_ctx = x_ctx.shape[1]
    l = l_ctx + l_lat
    e = in_w.shape[1] // 2
    g = e // S5_GROUP
    (sh_l, sc_l, g_l), (sh_c, sc_c, g_c) = _modulation(c, c_ctx, ada_w, ada_b)
    h_lat = _rmsnorm(x_lat, norm_g) * (1.0 + sc_l) + sh_l
    h_ctx = _rmsnorm(x_ctx, norm_g) * (1.0 + sc_c) + sh_c
    h = jnp.concatenate([h_ctx, h_lat], axis=1).astype(BF16).reshape(b * l, d)
    m = b * l
    tm = _pick(m, 1024)
    te = _pick(e, 1024)

    lane = jnp.arange(e)
    old = (lane % g) * S5_GROUP + lane // g
    perm = (lane[:, None] == old[None, :]).astype(BF16)
    in_w_p = matmul(in_w.reshape(2 * d, e), perm, out_dtype=BF16, tm=_pick(2 * d, 1024), tn=te,
                    name="mm_perm_in").reshape(d, 2 * e)
    glu_w_p = matmul(glu_w[old], perm, out_dtype=BF16, tm=te, tn=te, name="mm_perm_glu")
    out_w_p = out_w[old]
    d_p = d_skip.astype(F32)[old]
    glu_b_p = glu_b[old]

    uz = matmul(h, in_w_p, out_dtype=BF16, tm=tm, tn=_pick(2 * e, 1024), name="mm_s5_in")
    u, z = uz[:, :e], uz[:, e:]

    n_chunks = l // S5_T
    u_g = to_groups(uz.reshape(b, l, 2 * e), n_chunks=n_chunks)
    w1, ot, ar, ai = s5_operators(a_re, a_im, log_step, b_re, b_im, c_re, c_im)
    y_g = s5_scan(u_g, w1, ot, ar, ai, nb=b, n_ctx=l_ctx // S5_T, n_chunks=n_chunks)
    y_ssm = from_groups(y_g, nb=b, n_chunks=n_chunks).reshape(m, e)

    y1 = jax.nn.gelu(y_ssm.astype(F32) + d_p * u.astype(F32)).astype(BF16)
    gl = matmul(y1, glu_w_p, glu_b_p, out_dtype=F32, tm=tm, tn=te, name="mm_s5_glu")
    y2 = (y1.astype(F32) * jax.nn.sigmoid(gl) * jax.nn.silu(z.astype(F32))).astype(BF16)
    o = matmul(y2, out_w_p, out_dtype=F32, tm=tm, tn=_pick(d, 1024),
               name="mm_s5_out").reshape(b, l, d)
    x_ctx = x_ctx + g_c * o[:, :l_ctx]
    x_lat = x_lat + g_l * o[:, l_ctx:]
    return x_lat, x_ctx


def _rwkv_layer(x_lat, x_ctx, c, c_ctx, norm_g, ada_w, ada_b, mu, in_w, w0, w1, w2, a0, a1, a2,
                k_k, k_a, r_k, ln_w, ln_b, out_w):
    b, l_lat, d = x_lat.shape
    l_ctx = x_ctx.shape[1]
    l = l_ctx + l_lat
    e = in_w.shape[2]
    heads = e // RWKV_HEAD
    m = b * l
    (sh_l, sc_l, g_l), (sh_c, sc_c, _) = _modulation(c, c_ctx, ada_w, ada_b)
    h_lat = _rmsnorm(x_lat, norm_g) * (1.0 + sc_l) + sh_l
    h_ctx = _rmsnorm(x_ctx, norm_g) * (1.0 + sc_c) + sh_c
    h = jnp.concatenate([h_ctx, h_lat], axis=1)
    delta = jnp.concatenate([_seq_shift(h_ctx), _q_shift(h_lat)], axis=1) - h
    mix = [(h + delta * mu[i]).astype(BF16).reshape(m, d) for i in range(mu.shape[0])]

    tm = _pick(m, 1024)
    tn = _pick(e, 1024)
    r, k, v, z = (matmul(mix[i], in_w[i], out_dtype=BF16, tm=tm, tn=tn, name="mm_rwkv_in")
                  for i in range(4))

    n_dir, _, lora = w1.shape
    lp = (-lora) % 128

    def lora_down(xm, l1):
        l1c = jnp.concatenate([jnp.pad(l1[i], ((0, 0), (0, lp))) for i in range(n_dir)], axis=1)
        t = matmul(xm, l1c, out_dtype=F32, tm=tm, tn=l1c.shape[1], name="mm_lora_down")
        return t.reshape(b, l, n_dir, lora + lp).transpose(2, 0, 1, 3)

    pad_up = lambda l2: jnp.pad(l2, ((0, 0), (0, lp), (0, 0)))
    seq3 = lambda t: t.reshape(b, l, e)
    yg = rwkv_scan(seq3(r), seq3(k), seq3(v), seq3(z), lora_down(mix[4], w1), lora_down(mix[5], a1),
                   pad_up(w2), pad_up(a2), w0, a0, k_k, k_a, r_k.reshape(e), ln_w, ln_b,
                   n_ctx_tokens=l_ctx)
    o = matmul(yg.reshape(b * l_lat, e), out_w, out_dtype=F32,
               tm=_pick(b * l_lat, 1024), tn=_pick(d, 1024), name="mm_rwkv_out").reshape(b, l_lat, d)
    return x_lat + g_l * o


def kernel(x, c, ctx, c_ctx, l0_norm_g, l0_ada_w, l0_ada_b, l0_in_w, l0_a_re, l0_a_im, l0_log_step, l0_b_re, l0_b_im, l0_c_re, l0_c_im, l0_d, l0_glu_w, l0_glu_b, l0_out_w, l1_norm_g, l1_ada_w, l1_ada_b, l1_mu, l1_in_w, l1_w0, l1_w1, l1_w2, l1_a0, l1_a1, l1_a2, l1_k_k, l1_k_a, l1_r_k, l1_ln_w, l1_ln_b, l1_out_w, final_norm_g):
    x_lat, x_ctx = _s5_layer(x, ctx, c, c_ctx, l0_norm_g, l0_ada_w, l0_ada_b, l0_in_w, l0_a_re,
                             l0_a_im, l0_log_step, l0_b_re, l0_b_im, l0_c_re, l0_c_im, l0_d,
                             l0_glu_w, l0_glu_b, l0_out_w)
    x_lat = _rwkv_layer(x_lat, x_ctx, c, c_ctx, l1_norm_g, l1_ada_w, l1_ada_b, l1_mu, l1_in_w,
                        l1_w0, l1_w1, l1_w2, l1_a0, l1_a1, l1_a2, l1_k_k, l1_k_a, l1_r_k,
                        l1_ln_w, l1_ln_b, l1_out_w)
    return _rmsnorm(x_lat, final_norm_g).astype(x.dtype)
```

```python
import functools

import jax
import jax.numpy as jnp
from jax import lax
from jax.experimental import pallas as pl
from jax.experimental.pallas import tpu as pltpu

F32 = jnp.float32
BF16 = jnp.bfloat16

GRID_W = 64
S5_GROUP = 16
S5_STATE = 64
S5_T = 16
RWKV_HEAD = 64
RWKV_CHUNK = 64
RMS_EPS = 1e-6
GN_EPS = 64e-5
NORM_EPS = 1e-12

VMEM_LIMIT_BYTES = 56 * 1024 * 1024


def _dot(a, b):
    return jnp.dot(a.astype(BF16), b.astype(BF16), preferred_element_type=F32)


def _dot_nt(a, b):
    return lax.dot_general(a.astype(BF16), b.astype(BF16), (((1,), (1,)), ((), ())),
                           preferred_element_type=F32)


def _dot_tn(a, b):
    return lax.dot_general(a.astype(BF16), b.astype(BF16), (((0,), (0,)), ((), ())),
                           preferred_element_type=F32)


def _split2(x):
    hi = x.astype(BF16)
    lo = (x - hi.astype(F32)).astype(BF16)
    return hi, lo


def _split3(x):
    p1 = x.astype(BF16)
    r1 = x - p1.astype(F32)
    p2 = r1.astype(BF16)
    p3 = (r1 - p2.astype(F32)).astype(BF16)
    return p1, p2, p3


def _mm_kernel(a_ref, b_ref, o_ref):
    o_ref[...] = jnp.dot(a_ref[...], b_ref[...],
                         preferred_element_type=F32).astype(o_ref.dtype)


def _mm_bias_kernel(a_ref, b_ref, bias_ref, o_ref):
    acc = jnp.dot(a_ref[...], b_ref[...], preferred_element_type=F32)
    o_ref[...] = (acc + bias_ref[...]).astype(o_ref.dtype)


def matmul(a, b, bias=None, *, out_dtype, tm, tn, name):
    m, k = a.shape
    k2, n = b.shape
    assert k == k2 and m % tm == 0 and n % tn == 0, (a.shape, b.shape, tm, tn)
    a = a.astype(BF16)
    b = b.astype(BF16)
    in_specs = [pl.BlockSpec((tm, k), lambda j, i: (i, 0)),
                pl.BlockSpec((k, tn), lambda j, i: (0, j))]
    args = [a, b]
    kern = _mm_kernel
    if bias is not None:
        in_specs.append(pl.BlockSpec((1, tn), lambda j, i: (0, j)))
        args.append(bias.astype(F32).reshape(1, n))
        kern = _mm_bias_kernel
    return pl.pallas_call(
        kern,
        grid=(n // tn, m // tm),
        in_specs=in_specs,
        out_specs=pl.BlockSpec((tm, tn), lambda j, i: (i, j)),
        out_shape=jax.ShapeDtypeStruct((m, n), out_dtype),
        name=name,
        compiler_params=pltpu.CompilerParams(
            dimension_semantics=("arbitrary", "arbitrary"),
            vmem_limit_bytes=VMEM_LIMIT_BYTES),
    )(*args)


def _mm_glu_kernel(a_ref, b_ref, bias_ref, z_ref, o_ref, *, tn):
    j = pl.program_id(0)
    acc = jnp.dot(a_ref[...], b_ref[...], preferred_element_type=F32) + bias_ref[...]
    y = a_ref[:, pl.ds(pl.multiple_of(j * tn, tn), tn)].astype(F32)
    z = z_ref[...].astype(F32)
    o_ref[...] = (y / (1.0 + jnp.exp(-acc)) * (z / (1.0 + jnp.exp(-z)))).astype(o_ref.dtype)


def matmul_glu(y, w, bias, uz, *, tm, tn, name):
    m, e = y.shape
    assert w.shape == (e, e) and uz.shape == (m, 2 * e) and m % tm == 0 and e % tn == 0
    n_tiles = e // tn
    return pl.pallas_call(
        functools.partial(_mm_glu_kernel, tn=tn),
        grid=(n_tiles, m // tm),
        in_specs=[pl.BlockSpec((tm, e), lambda j, i: (i, 0)),
                  pl.BlockSpec((e, tn), lambda j, i: (0, j)),
                  pl.BlockSpec((1, tn), lambda j, i: (0, j)),
                  pl.BlockSpec((tm, tn), lambda j, i: (i, j + n_tiles))],
        out_specs=pl.BlockSpec((tm, tn), lambda j, i: (i, j)),
        out_shape=jax.ShapeDtypeStruct((m, e), BF16),
        name=name,
        compiler_params=pltpu.CompilerParams(
            dimension_semantics=("arbitrary", "arbitrary"),
            vmem_limit_bytes=VMEM_LIMIT_BYTES),
    )(y.astype(BF16), w.astype(BF16), bias.astype(F32).reshape(1, e), uz)


def _pick(total, pref):
    t = min(pref, total)
    while total % t:
        t //= 2
    return t


def _s5_prep_kernel(are_ref, aim_ref, ls_ref, bre_ref, bim_ref, cre_ref, cim_ref,
                    w1_ref, ot_ref, ar_ref, ai_ref, *, gb):
    T = S5_T
    GC = S5_GROUP
    lane = lax.broadcasted_iota(jnp.int32, (1, 128), 1)
    fwd = lane < S5_STATE
    s_col = lax.broadcasted_iota(jnp.int32, (T, 1), 0).astype(F32)
    ri = lax.broadcasted_iota(jnp.int32, (T * GC, T * GC), 0) % T
    ci = lax.broadcasted_iota(jnp.int32, (T * GC, T * GC), 1) % T
    lane2 = lax.broadcasted_iota(jnp.int32, (1, 256), 1) % 128
    fwd2 = lane2 < S5_STATE
    half = float(T // 2)

    for g in range(gb):
        a_re = are_ref[g]
        a_im = aim_ref[g]
        dt = jnp.exp(ls_ref[g])
        x_re = dt * a_re
        x_im = dt * a_im

        def lam_pow(p):
            mag = jnp.exp(p * x_re)
            ang = p * x_im
            return mag * jnp.cos(ang), mag * jnp.sin(ang)

        l_re, l_im = lam_pow(jnp.ones((1, 1), F32))
        den = a_re * a_re + a_im * a_im
        nr, ni = l_re - 1.0, l_im
        f_re = (nr * a_re + ni * a_im) / den
        f_im = (ni * a_re - nr * a_im) / den
        b_re = bre_ref[g]
        b_im = bim_ref[g]
        bb_re = f_re * b_re - f_im * b_im
        bb_im = f_re * b_im + f_im * b_re
        c_re = cre_ref[g]
        c_im = cim_ref[g]

        def times_pow(v_re, v_im, p):
            p_re, p_im = lam_pow(p)
            o_re = v_re[:, None, :] * p_re[None, :, :] - v_im[:, None, :] * p_im[None, :, :]
            o_im = v_re[:, None, :] * p_im[None, :, :] + v_im[:, None, :] * p_re[None, :, :]
            return o_re.reshape(GC * T, 128), o_im.reshape(GC * T, 128)

        pb = jnp.where(fwd, half - s_col, s_col - half)
        pc = -pb
        bl_re, bl_im = times_pow(bb_re, bb_im, pb)
        cl_re, cl_im = times_pow(c_re, c_im, pc)
        lhs = jnp.concatenate([bl_re, bl_im], axis=1)
        rhs = jnp.concatenate([cl_re, -cl_im], axis=1)
        r_hi, r_lo = _split2(rhs)

        def dot3(l):
            l_hi, l_lo = _split2(l)
            dn = (((1,), (1,)), ((), ()))
            return (lax.dot_general(l_hi, r_hi, dn, preferred_element_type=F32)
                    + lax.dot_general(l_hi, r_lo, dn, preferred_element_type=F32)
                    + lax.dot_general(l_lo, r_hi, dn, preferred_element_type=F32))

        t_f = dot3(jnp.where(fwd2, lhs, 0.0))
        t_b = dot3(jnp.where(fwd2, 0.0, lhs))
        toep = jnp.where(ci >= ri, t_f, 0.0) + jnp.where(ri >= ci, t_b, 0.0)

        ps = jnp.where(fwd, (T - 1.0) - s_col, s_col)
        s_re, s_im = times_pow(bb_re, bb_im, ps)
        w1_ref[g] = jnp.concatenate([toep, s_re, s_im], axis=1).astype(w1_ref.dtype)

        po = jnp.where(fwd, s_col + 1.0, T - s_col)
        o_re, o_im = times_pow(c_re, c_im, po)
        ot_ref[g] = jnp.concatenate([o_re, -o_im], axis=1).astype(ot_ref.dtype)

        pw_re, pw_im = lam_pow(jnp.full((1, 1), float(T), F32))
        ar_ref[g] = pw_re
        ai_ref[g] = pw_im


def _s5_kernel(u_ref, w1_ref, ot_ref, ar_ref, ai_ref, y_ref,
               p_ref, hf_re, hf_im, hb_re, hb_im, *, nb, n_ctx, n_chunks):
    gs = u_ref.shape[0]
    for g in range(gs):
        p_ref[g] = jnp.dot(u_ref[g], w1_ref[g], preferred_element_type=F32)
    ar = jnp.broadcast_to(ar_ref[...], (gs, nb, 128))
    ai = jnp.broadcast_to(ai_ref[...], (gs, nb, 128))
    fwd = lax.broadcasted_iota(jnp.int32, (gs, nb, 128), 2) < S5_STATE

    def step(i, carry):
        h_re, h_im = carry
        cb = jnp.where(i < n_ctx, n_ctx - 1 - i, n_chunks - 1 - (i - n_ctx))
        rf = pl.ds(pl.multiple_of(i * nb, nb), nb)
        rb = pl.ds(pl.multiple_of(cb * nb, nb), nb)
        hf_re[:, rf, :] = h_re
        hf_im[:, rf, :] = h_im
        hb_re[:, rb, :] = h_re
        hb_im[:, rb, :] = h_im
        x_re = jnp.where(fwd, p_ref[:, rf, 256:384], p_ref[:, rb, 256:384])
        x_im = jnp.where(fwd, p_ref[:, rf, 384:512], p_ref[:, rb, 384:512])
        n_re = ar * h_re - ai * h_im + x_re
        n_im = ar * h_im + ai * h_re + x_im
        return n_re, n_im

    zero = jnp.zeros((gs, nb, 128), F32)
    lax.fori_loop(0, n_chunks, step, (zero, zero))

    rows = p_ref.shape[1]
    fwd_all = lax.broadcasted_iota(jnp.int32, (rows, 128), 1) < S5_STATE
    for g in range(gs):
        h_all = jnp.concatenate([jnp.where(fwd_all, hf_re[g], hb_re[g]),
                                 jnp.where(fwd_all, hf_im[g], hb_im[g])], axis=1)
        y = p_ref[g, :, 0:256] + _dot_nt(h_all, ot_ref[g])
        y_ref[g] = y.astype(y_ref.dtype)


def s5_operators(a_re, a_im, log_step, b_re, b_im, c_re, c_im):
    n_dir, g, n = a_re.shape
    gb = 8

    def lanes(x):
        return jnp.concatenate([x[0], x[1]], axis=-1).astype(F32)

    are = lanes(a_re).reshape(g, 1, 2 * n)
    aim = lanes(a_im).reshape(g, 1, 2 * n)
    ls = lanes(jnp.broadcast_to(log_step[:, :, None], (n_dir, g, n))).reshape(g, 1, 2 * n)
    bre = lanes(jnp.swapaxes(b_re, -1, -2))
    bim = lanes(jnp.swapaxes(b_im, -1, -2))
    cre = lanes(c_re)
    cim = lanes(c_im)
    row = pl.BlockSpec((gb, 1, 2 * n), lambda i: (i, 0, 0))
    mat = pl.BlockSpec((gb, S5_GROUP, 2 * n), lambda i: (i, 0, 0))
    tg = S5_T * S5_GROUP
    return pl.pallas_call(
        functools.partial(_s5_prep_kernel, gb=gb),
        grid=(g // gb,),
        in_specs=[row, row, row, mat, mat, mat, mat],
        out_specs=[pl.BlockSpec((gb, tg, tg + 4 * n), lambda i: (i, 0, 0)),
                   pl.BlockSpec((gb, tg, 4 * n), lambda i: (i, 0, 0)),
                   row, row],
        out_shape=[jax.ShapeDtypeStruct((g, tg, tg + 4 * n), BF16),
                   jax.ShapeDtypeStruct((g, tg, 4 * n), BF16),
                   jax.ShapeDtypeStruct((g, 1, 2 * n), F32),
                   jax.ShapeDtypeStruct((g, 1, 2 * n), F32)],
        name="s5_operators",
        compiler_params=pltpu.CompilerParams(dimension_semantics=("arbitrary",),
                                             vmem_limit_bytes=VMEM_LIMIT_BYTES),
    )(are, aim, ls, bre, bim, cre, cim)


def s5_scan(u_g, w1, ot, ar, ai, *, nb, n_ctx, n_chunks):
    g, rows, tg = u_g.shape
    assert rows == n_chunks * nb and tg == S5_T * S5_GROUP
    gs = _pick(g, 4)
    blk = lambda w: pl.BlockSpec((gs,) + w, lambda i: (i, 0, 0))
    return pl.pallas_call(
        functools.partial(_s5_kernel, nb=nb, n_ctx=n_ctx, n_chunks=n_chunks),
        grid=(g // gs,),
        in_specs=[blk((rows, tg)), blk(w1.shape[1:]), blk(ot.shape[1:]),
                  blk((1, 128)), blk((1, 128))],
        out_specs=blk((rows, tg)),
        out_shape=jax.ShapeDtypeStruct((g, rows, tg), BF16),
        scratch_shapes=[pltpu.VMEM((gs, rows, w1.shape[2]), F32)]
                       + [pltpu.VMEM((gs, rows, 128), F32)] * 4,
        name="s5_scan",
        compiler_params=pltpu.CompilerParams(dimension_semantics=("arbitrary",),
                                             vmem_limit_bytes=VMEM_LIMIT_BYTES),
    )(u_g, w1, ot, ar, ai)


def _eye(n):
    r = lax.broadcasted_iota(jnp.int32, (n, n), 0)
    c = lax.broadcasted_iota(jnp.int32, (n, n), 1)
    return jnp.where(r == c, 1.0, 0.0).astype(BF16)


def _to_groups_kernel(u_ref, o_ref, *, nb, n_j):
    g = o_ref.shape[0]
    eye = _eye(g)
    for b in range(nb):
        x = u_ref[b]
        rows = jnp.concatenate([x[:, j * g:(j + 1) * g] for j in range(n_j)], axis=0)
        o_ref[:, b, :] = _dot_nt(eye, rows).astype(o_ref.dtype)


def _from_groups_kernel(y_ref, o_ref, *, nb, n_j):
    g = y_ref.shape[0]
    t = o_ref.shape[1]
    eye = _eye(y_ref.shape[2])
    for b in range(nb):
        cols = _dot_nt(eye, y_ref[:, b, :]).astype(o_ref.dtype)
        for i in range(n_j):
            o_ref[b, :, i * g:(i + 1) * g] = cols[i * t:(i + 1) * t, :]


def to_groups(uz, *, n_chunks):
    b, l, e2 = uz.shape
    e = e2 // 2
    g = e // S5_GROUP
    return pl.pallas_call(
        functools.partial(_to_groups_kernel, nb=b, n_j=S5_GROUP),
        grid=(n_chunks,),
        in_specs=[pl.BlockSpec((b, S5_T, e), lambda c: (0, c, 0))],
        out_specs=pl.BlockSpec((g, b, S5_T * S5_GROUP), lambda c: (0, c, 0)),
        out_shape=jax.ShapeDtypeStruct((g, n_chunks * b, S5_T * S5_GROUP), BF16),
        name="s5_to_groups",
        compiler_params=pltpu.CompilerParams(dimension_semantics=("arbitrary",),
                                             vmem_limit_bytes=VMEM_LIMIT_BYTES),
    )(uz)


def from_groups(y_g, *, nb, n_chunks):
    g, rows, tg = y_g.shape
    e = g * S5_GROUP
    return pl.pallas_call(
        functools.partial(_from_groups_kernel, nb=nb, n_j=S5_GROUP),
        grid=(n_chunks,),
        in_specs=[pl.BlockSpec((g, nb, tg), lambda c: (0, c, 0))],
        out_specs=pl.BlockSpec((nb, S5_T, e), lambda c: (0, c, 0)),
        out_shape=jax.ShapeDtypeStruct((nb, n_chunks * S5_T, e), BF16),
        name="s5_from_groups",
        compiler_params=pltpu.CompilerParams(dimension_semantics=("arbitrary",),
                                             vmem_limit_bytes=VMEM_LIMIT_BYTES),
    )(y_g)


def _rwkv_chunk_index(step, n_ctx, n_chunks):
    i = step % n_chunks
    back = jnp.where(i < n_ctx, n_ctx - 1 - i, n_chunks - 1 - (i - n_ctx))
    return jnp.where(step < n_chunks, i, back)


def _rwkv_kernel(r_ref, k_ref, v_ref, z_ref, tw_ref, ta_ref, w2_ref, a2_ref, w0_ref, a0_ref,
                 kk_ref, ka_ref, rk_ref, lnw_ref, lnb_ref, o_ref, s_ref, acc_ref,
                 *, n_ctx, n_chunks, n_pairs):
    C = RWKV_CHUNK
    step = pl.program_id(2)
    i = step % n_chunks
    is_fwd = step < n_chunks
    cidx = _rwkv_chunk_index(step, n_ctx, n_chunks)
    sgn = jnp.where(is_fwd, 1, -1)

    t_i = lax.broadcasted_iota(jnp.int32, (C, 128), 0)
    lane = lax.broadcasted_iota(jnp.int32, (C, 128), 1)
    s_i = lane % C
    lane_head = lane // RWKV_HEAD
    before = (s_i - t_i) * sgn
    strict = before < 0
    incl = before <= 0
    same_blk = (t_i // 16) == (s_i // 16)
    diag_m = strict & same_blk
    off_m = strict & jnp.logical_not(same_blk)
    tr = lax.broadcasted_iota(jnp.int32, (C, C), 0)
    tc = lax.broadcasted_iota(jnp.int32, (C, C), 1)
    cum_m = jnp.where((tc - tr) * sgn <= 0, 1.0, 0.0).astype(BF16)
    ri = lax.broadcasted_iota(jnp.int32, (128, 128), 0)
    ci = lax.broadcasted_iota(jnp.int32, (128, 128), 1)
    same_head = (ri // RWKV_HEAD) == (ci // RWKV_HEAD)
    head_ones = jnp.where(same_head, 1.0, 0.0).astype(BF16)

    @pl.when(i == 0)
    def _():
        s_ref[...] = jnp.zeros(s_ref.shape, F32)

    def stack(x):
        x = x.astype(BF16)
        zero = jnp.zeros_like(x)
        return jnp.concatenate([jnp.where(lane_head == 0, x, zero),
                                jnp.where(lane_head == 1, x, zero)], axis=0)

    def pack(x):
        return jnp.where(lane_head == 0, x[:C], x[C:])

    r = r_ref[0].astype(F32)
    k = k_ref[0].astype(F32)
    v = v_ref[0].astype(F32)
    d = jnp.where(is_fwd, 0, 1)
    wp = w0_ref[pl.ds(d, 1), :] + _dot(jnp.tanh(tw_ref[0, 0]), w2_ref[d])
    ap_f = a0_ref[0:1, :] + _dot(ta_ref[0, 0], a2_ref[0])
    ap_b = a0_ref[1:2, :] + _dot(ta_ref[1, 0], a2_ref[1])
    sp = jnp.maximum(-wp, 0.0) + jnp.log(1.0 + jnp.exp(-jnp.abs(wp)))
    lw = -jnp.exp(-sp - 0.5)
    ar_f = 1.0 / (1.0 + jnp.exp(-ap_f))
    ar_b = 1.0 / (1.0 + jnp.exp(-ap_b))
    a_rate = jnp.where(is_fwd, ar_f, ar_b)
    k_a = ka_ref[...]
    kkr = k * kk_ref[...]
    kd = k * (1.0 + (a_rate - 1.0) * k_a)
    lc = sum(jnp.dot(cum_m, p, preferred_element_type=F32) for p in _split3(lw))
    l_mid = lc[C // 2:C // 2 + 1, :]
    l_end = jnp.where(is_fwd, lc[C - 1:C, :], lc[0:1, :])
    dl = lc - l_mid
    e_in = jnp.exp(dl)
    e_ex = jnp.exp(dl - lw)
    e_iv = jnp.exp(-dl)
    c_row = jnp.exp(l_mid)
    e2 = jnp.exp(l_end - l_mid)
    wc = jnp.exp(l_end)
    rt = r * e_in
    ar_iv = a_rate * e_iv
    kt = kd * e_iv

    pairs = range(n_pairs)

    def each(f, *lists):
        return [f(*args) for args in zip(*lists)]

    def cols(x):
        return [x[:, p * 128:(p + 1) * 128] for p in pairs]

    sq = each(lambda x: _split2(x * x), cols(kkr))
    ssq = each(lambda s: jnp.dot(s[0], head_ones, preferred_element_type=F32)
               + jnp.dot(s[1], head_ones, preferred_element_type=F32), sq)
    kk = each(lambda x, s: x / jnp.maximum(jnp.sqrt(s), NORM_EPS), cols(kkr), ssq)
    mm = lambda l, r_: jnp.dot(l.astype(BF16), r_, preferred_element_type=F32)
    wide = lambda l, *rs: mm(l, jnp.concatenate(rs, axis=1))
    stack2 = lambda x: jnp.concatenate([stack(x[:, :128]), stack(x[:, 128:])], axis=1)

    at = each(lambda x, e: -x * e, kk, cols(e_ex))
    bt = each(lambda x, e: x * e, kk, cols(ar_iv))
    rp, kp, vp_ = cols(rt), cols(kt), cols(v)
    vs = each(stack, vp_)
    a4 = each(lambda a, r_, b, k_: _dot_nt(jnp.concatenate([a, r_], axis=0),
                                           jnp.concatenate([stack(b), stack(k_)], axis=0)),
              at, rp, bt, kp)
    ak = each(lambda m: jnp.where(strict, m[:C, 128:], 0.0), a4)
    rb = each(lambda m: jnp.where(incl, m[C:, :128], 0.0), a4)
    rk = each(lambda m: jnp.where(incl, m[C:, 128:], 0.0), a4)
    akv = each(mm, ak, vs)
    ad = each(lambda m: jnp.where(diag_m, m[:C, :128], 0.0), a4)
    ao = each(lambda m: jnp.where(off_m, m[:C, :128], 0.0), a4)
    ad_s = each(stack, ad)
    p2 = each(mm, ad, ad_s)
    p43 = each(lambda p, a_s: wide(p, stack(p), a_s), p2, ad_s)
    p4 = [m[:, :128] for m in p43]
    s3 = each(lambda a, p, m: a + p + m[:, 128:], ad, p2, p43)
    p8s = each(lambda p, s: wide(p, stack(p), stack(s)), p4, s3)
    p8 = [m[:, :128] for m in p8s]
    s7 = each(lambda s, p, m: s + p + m[:, 128:], s3, p4, p8s)
    dm = each(lambda s, p: s + p + mm(p, stack(s)), s7, p8)
    x0 = each(lambda a, m: jnp.concatenate([a, m], axis=1), at, akv)
    dx = each(lambda d_, o, x: wide(d_, stack(o), stack2(x)), dm, ao, x0)
    n1 = each(lambda o, m: o + m[:, :128], ao, dx)
    xa = each(lambda x, m: x + m[:, 128:], x0, dx)
    xc = each(lambda n_, x: x + mm(n_, stack2(x)), n1, xa)
    w1_ = each(lambda n_, x: mm(n_, stack2(x)), n1, xc)
    x1 = each(lambda n_, x, w_: x + mm(n_, stack2(w_)), n1, xc, w1_)
    zr = each(lambda rb_, rk_, x, v_s: mm(
        jnp.concatenate([rb_, rk_], axis=1),
        jnp.concatenate([stack2(x), jnp.concatenate([jnp.zeros_like(v_s), v_s], axis=1)], axis=0)),
        rb, rk, x1, vs)
    ca = each(lambda x, c_: x[:, :128] * c_, x1, cols(c_row))
    cr = each(lambda r_, z_, c_: (r_ + z_[:, :128]) * c_, rp, zr, cols(c_row))
    yp = [z_[:, 128:] for z_ in zr]
    lt = each(lambda ca_, x, v_: jnp.concatenate(
        [jnp.concatenate([ca_, x[:, 128:]], axis=1),
         jnp.concatenate([jnp.zeros_like(v_), v_], axis=1)], axis=0), ca, x1, vp_)
    rt2 = each(lambda b, k_, e: jnp.concatenate([b * e, k_ * e], axis=0), bt, kp, cols(e2))
    mh = each(_dot_tn, lt, rt2)
    s_old = [s_ref[p] for p in pairs]
    ys = each(lambda c_, s, y_: _dot_nt(c_, stack(s)) + y_, cr, s_old, yp)
    s_new = each(lambda s, w_, m: s * w_ + mm(s, jnp.where(same_head, m[:128], 0.0).astype(BF16))
                 + pack(m[128:]), s_old, cols(wc), mh)
    for p in pairs:
        s_ref[p] = s_new[p]

    y = jnp.concatenate(ys, axis=1)
    orow = pl.multiple_of(jnp.maximum(cidx - n_ctx, 0) * C, C)

    @pl.when(jnp.logical_and(cidx >= n_ctx, is_fwd))
    def _():
        acc_ref[pl.ds(orow, C), :] = y

    @pl.when(jnp.logical_and(cidx >= n_ctx, jnp.logical_not(is_fwd)))
    def _():
        y_sum = acc_ref[pl.ds(orow, C), :] + y

        def head_sum(x):
            outs = []
            for xp in cols(x):
                hi, lo = _split2(xp)
                outs.append(jnp.dot(hi, head_ones, preferred_element_type=F32)
                            + jnp.dot(lo, head_ones, preferred_element_type=F32))
            return jnp.concatenate(outs, axis=1)

        inv = 1.0 / RWKV_HEAD
        dev = y_sum - head_sum(y_sum) * inv
        yn = dev * lax.rsqrt(head_sum(dev * dev) * inv + GN_EPS)
        yn = yn * lnw_ref[...] + lnb_ref[...]
        k_sum = k * (2.0 + (ar_f + ar_b - 2.0) * k_a)
        bonus = head_sum(r * k_sum * rk_ref[...]) * v
        zg = z_ref[0].astype(F32)
        o_ref[0] = ((yn + bonus) * (zg / (1.0 + jnp.exp(-zg)))).astype(o_ref.dtype)


def rwkv_scan(r, k, v, z, tw, ta, w2, a2, w0, a0, k_k, k_a, r_k, ln_w, ln_b, *, n_ctx_tokens):
    b, l, e = r.shape
    rank = tw.shape[-1]
    C = RWKV_CHUNK
    pw = min(e, 2048)
    n_pairs = pw // 128
    n_chunks = l // C
    n_ctx = n_ctx_tokens // C
    l_out = l - n_ctx_tokens
    cix = functools.partial(_rwkv_chunk_index, n_ctx=n_ctx, n_chunks=n_chunks)

    def out_chunk(t):
        c = cix(t)
        return jnp.where(jnp.logical_and(t >= n_chunks, c >= n_ctx), c, n_chunks - 1) - n_ctx

    seq = pl.BlockSpec((1, C, pw), lambda i, j, t: (i, cix(t), j))
    low_d = pl.BlockSpec((1, 1, C, rank), lambda i, j, t: (t // n_chunks, i, cix(t), 0))
    low_2 = pl.BlockSpec((2, 1, C, rank), lambda i, j, t: (0, i, cix(t), 0))
    up = pl.BlockSpec((2, rank, pw), lambda i, j, t: (0, 0, j))
    vec2 = pl.BlockSpec((2, pw), lambda i, j, t: (0, j))
    vec = pl.BlockSpec((1, pw), lambda i, j, t: (0, j))
    row = lambda x: x.astype(F32).reshape(1, e)
    return pl.pallas_call(
        functools.partial(_rwkv_kernel, n_ctx=n_ctx, n_chunks=n_chunks, n_pairs=n_pairs),
        grid=(b, e // pw, 2 * n_chunks),
        in_specs=[seq, seq, seq, seq, low_d, low_2, up, up, vec2, vec2, vec, vec, vec, vec, vec],
        out_specs=pl.BlockSpec((1, C, pw), lambda i, j, t: (i, out_chunk(t), j)),
        out_shape=jax.ShapeDtypeStruct((b, l_out, e), BF16),
        scratch_shapes=[pltpu.VMEM((n_pairs, C, 128), F32),
                        pltpu.VMEM((l_out, pw), F32)],
        name="rwkv_scan",
        compiler_params=pltpu.CompilerParams(
            dimension_semantics=("arbitrary", "arbitrary", "arbitrary"),
            vmem_limit_bytes=VMEM_LIMIT_BYTES),
    )(r, k, v, z, tw, ta, w2.astype(BF16), a2.astype(BF16), w0.astype(F32), a0.astype(F32),
      row(k_k), row(k_a), row(r_k), row(ln_w), row(ln_b))


def _rmsnorm(x, g):
    xf = x.astype(F32)
    y = xf * lax.rsqrt(jnp.mean(xf * xf, axis=-1, keepdims=True) + RMS_EPS)
    return y * g.astype(F32)


def _modulation(c, c_ctx, ada_w, ada_b):
    d = c.shape[-1]
    cond = jnp.concatenate([c, c_ctx[None]], axis=0)
    rows = cond.shape[0]
    pad = (-rows) % 16
    act = jnp.pad(jax.nn.silu(cond), ((0, pad), (0, 0)))
    m = matmul(act, ada_w, ada_b, out_dtype=F32, tm=rows + pad, tn=_pick(3 * d, 1024),
               name="mm_modulation")
    m = m[:rows]
    lat = [t[:, None, :] for t in jnp.split(m[:rows - 1], 3, axis=-1)]
    ctx = [t[None] for t in jnp.split(m[rows - 1:], 3, axis=-1)]
    return lat, ctx


def _q_shift(h):
    b, l, d = h.shape
    rows = l // GRID_W
    g = h.reshape(b, rows, GRID_W, d)
    q = d // 4
    left = jnp.pad(g[:, :, :-1, :q], ((0, 0), (0, 0), (1, 0), (0, 0)))
    right = jnp.pad(g[:, :, 1:, q:2 * q], ((0, 0), (0, 0), (0, 1), (0, 0)))
    up = jnp.pad(g[:, :-1, :, 2 * q:3 * q], ((0, 0), (1, 0), (0, 0), (0, 0)))
    down = jnp.pad(g[:, 1:, :, 3 * q:], ((0, 0), (0, 1), (0, 0), (0, 0)))
    return jnp.concatenate([left, right, up, down], axis=-1).reshape(b, l, d)


def _seq_shift(h):
    half = h.shape[-1] // 2
    prev = jnp.pad(h[:, :-1, :half], ((0, 0), (1, 0), (0, 0)))
    nxt = jnp.pad(h[:, 1:, half:], ((0, 0), (0, 1), (0, 0)))
    return jnp.concatenate([prev, nxt], axis=-1)


def _s5_layer(x_lat, x_ctx, c, c_ctx, norm_g, ada_w, ada_b, in_w, a_re, a_im, log_step,
              b_re, b_im, c_re, c_im, d_skip, glu_w, glu_b, out_w):
    b, l_lat, d = x_lat.shape
    l_ctx = x_ctx.shape[1]
    l = l_ctx + l_lat
    e = in_w.shape[1] // 2
    g = e // S5_GROUP
    (sh_l, sc_l, g_l), (sh_c, sc_c, g_c) = _modulation(c, c_ctx, ada_w, ada_b)
    h_lat = _rmsnorm(x_lat, norm_g) * (1.0 + sc_l) + sh_l
    h_ctx = _rmsnorm(x_ctx, norm_g) * (1.0 + sc_c) + sh_c
    h = jnp.concatenate([h_ctx, h_lat], axis=1).astype(BF16).reshape(b * l, d)
    m = b * l
    tm = _pick(m, 1024)
    te = _pick(e, 1024)

    lane = jnp.arange(e)
    old = (lane % g) * S5_GROUP + lane // g
    perm = (lane[:, None] == old[None, :]).astype(BF16)
    in_w_p = matmul(in_w.reshape(2 * d, e), perm, out_dtype=BF16, tm=_pick(2 * d, 1024), tn=te,
                    name="mm_perm_in").reshape(d, 2 * e)
    glu_w_p = matmul(glu_w[old], perm, out_dtype=BF16, tm=te, tn=te, name="mm_perm_glu")
    out_w_p = out_w[old]
    d_p = d_skip.astype(F32)[old]
    glu_b_p = glu_b[old]

    uz = matmul(h, in_w_p, out_dtype=BF16, tm=tm, tn=_pick(2 * e, 1024), name="mm_s5_in")
    u, z = uz[:, :e], uz[:, e:]

    n_chunks = l // S5_T
    u_g = to_groups(uz.reshape(b, l, 2 * e), n_chunks=n_chunks)
    w1, ot, ar, ai = s5_operators(a_re, a_im, log_step, b_re, b_im, c_re, c_im)
    y_g = s5_scan(u_g, w1, ot, ar, ai, nb=b, n_ctx=l_ctx // S5_T, n_chunks=n_chunks)
    y_ssm = from_groups(y_g, nb=b, n_chunks=n_chunks).reshape(m, e)

    y1 = jax.nn.gelu(y_ssm.astype(F32) + d_p * u.astype(F32)).astype(BF16)
    y2 = matmul_glu(y1, glu_w_p, glu_b_p, uz, tm=tm, tn=te, name="mm_s5_glu")
    o = matmul(y2, out_w_p, out_dtype=F32, tm=tm, tn=_pick(d, 1024),
               name="mm_s5_out").reshape(b, l, d)
    x_ctx = x_ctx + g_c * o[:, :l_ctx]
    x_lat = x_lat + g_l * o[:, l_ctx:]
    return x_lat, x_ctx


def _rwkv_layer(x_lat, x_ctx, c, c_ctx, norm_g, ada_w, ada_b, mu, in_w, w0, w1, w2, a0, a1, a2,
                k_k, k_a, r_k, ln_w, ln_b, out_w):
    b, l_lat, d = x_lat.shape
    l_ctx = x_ctx.shape[1]
    l = l_ctx + l_lat
    e = in_w.shape[2]
    heads = e // RWKV_HEAD
    m = b * l
    (sh_l, sc_l, g_l), (sh_c, sc_c, _) = _modulation(c, c_ctx, ada_w, ada_b)
    h_lat = _rmsnorm(x_lat, norm_g) * (1.0 + sc_l) + sh_l
    h_ctx = _rmsnorm(x_ctx, norm_g) * (1.0 + sc_c) + sh_c
    h = jnp.concatenate([h_ctx, h_lat], axis=1)
    delta = jnp.concatenate([_seq_shift(h_ctx), _q_shift(h_lat)], axis=1) - h
    mix = [(h + delta * mu[i]).astype(BF16).reshape(m, d) for i in range(mu.shape[0])]

    tm = _pick(m, 1024)
    tn = _pick(e, 1024)
    r, k, v, z = (matmul(mix[i], in_w[i], out_dtype=BF16, tm=tm, tn=tn, name="mm_rwkv_in")
                  for i in range(4))

    n_dir, _, lora = w1.shape
    lp = (-lora) % 128

    def lora_down(xm, l1):
        l1c = jnp.concatenate([jnp.pad(l1[i], ((0, 0), (0, lp))) for i in range(n_dir)], axis=1)
        t = matmul(xm, l1c, out_dtype=F32, tm=tm, tn=l1c.shape[1], name="mm_lora_down")
        return t.reshape(b, l, n_dir, lora + lp).transpose(2, 0, 1, 3)

    pad_up = lambda l2: jnp.pad(l2, ((0, 0), (0, lp), (0, 0)))
    seq3 = lambda t: t.reshape(b, l, e)
    yg = rwkv_scan(seq3(r), seq3(k), seq3(v), seq3(z), lora_down(mix[4], w1), lora_down(mix[5], a1),
                   pad_up(w2), pad_up(a2), w0, a0, k_k, k_a, r_k.reshape(e), ln_w, ln_b,
                   n_ctx_tokens=l_ctx)
    o = matmul(yg.reshape(b * l_lat, e), out_w, out_dtype=F32,
               tm=_pick(b * l_lat, 1024), tn=_pick(d, 1024), name="mm_rwkv_out").reshape(b, l_lat, d)
    return x_lat + g_l * o


def kernel(x, c, ctx, c_ctx, l0_norm_g, l0_ada_w, l0_ada_b, l0_in_w, l0_a_re, l0_a_im, l0_log_step, l0_b_re, l0_b_im, l0_c_re, l0_c_im, l0_d, l0_glu_w, l0_glu_b, l0_out_w, l1_norm_g, l1_ada_w, l1_ada_b, l1_mu, l1_in_w, l1_w0, l1_w1, l1_w2, l1_a0, l1_a1, l1_a2, l1_k_k, l1_k_a, l1_r_k, l1_ln_w, l1_ln_b, l1_out_w, final_norm_g):
    x_lat, x_ctx = _s5_layer(x, ctx, c, c_ctx, l0_norm_g, l0_ada_w, l0_ada_b, l0_in_w, l0_a_re,
                             l0_a_im, l0_log_step, l0_b_re, l0_b_im, l0_c_re, l0_c_im, l0_d,
                             l0_glu_w, l0_glu_b, l0_out_w)
    x_lat = _rwkv_layer(x_lat, x_ctx, c, c_ctx, l1_norm_g, l1_ada_w, l1_ada_b, l1_mu, l1_in_w,
                        l1_w0, l1_w1, l1_w2, l1_a0, l1_a1, l1_a2, l1_k_k, l1_k_a, l1_r_k,
                        l1_ln_w, l1_ln_b, l1_out_w)
    return _rmsnorm(x_lat, final_norm_g).astype(x.dtype)
```

```python
import functools

import jax
import jax.numpy as jnp
from jax import lax
from jax.experimental import pallas as pl
from jax.experimental.pallas import tpu as pltpu

F32 = jnp.float32
BF16 = jnp.bfloat16

GRID_W = 64
S5_GROUP = 16
S5_STATE = 64
S5_T = 16
RWKV_HEAD = 64
RWKV_CHUNK = 64
RMS_EPS = 1e-6
GN_EPS = 64e-5
NORM_EPS = 1e-12

VMEM_LIMIT_BYTES = 56 * 1024 * 1024


def _dot(a, b):
    return jnp.dot(a.astype(BF16), b.astype(BF16), preferred_element_type=F32)


def _dot_nt(a, b):
    return lax.dot_general(a.astype(BF16), b.astype(BF16), (((1,), (1,)), ((), ())),
                           preferred_element_type=F32)


def _dot_tn(a, b):
    return lax.dot_general(a.astype(BF16), b.astype(BF16), (((0,), (0,)), ((), ())),
                           preferred_element_type=F32)


def _split2(x):
    hi = x.astype(BF16)
    lo = (x - hi.astype(F32)).astype(BF16)
    return hi, lo


def _split3(x):
    p1 = x.astype(BF16)
    r1 = x - p1.astype(F32)
    p2 = r1.astype(BF16)
    p3 = (r1 - p2.astype(F32)).astype(BF16)
    return p1, p2, p3


def _mm_kernel(a_ref, b_ref, o_ref):
    o_ref[...] = jnp.dot(a_ref[...], b_ref[...],
                         preferred_element_type=F32).astype(o_ref.dtype)


def _mm_bias_kernel(a_ref, b_ref, bias_ref, o_ref):
    acc = jnp.dot(a_ref[...], b_ref[...], preferred_element_type=F32)
    o_ref[...] = (acc + bias_ref[...]).astype(o_ref.dtype)


def matmul(a, b, bias=None, *, out_dtype, tm, tn, name):
    m, k = a.shape
    k2, n = b.shape
    assert k == k2 and m % tm == 0 and n % tn == 0, (a.shape, b.shape, tm, tn)
    a = a.astype(BF16)
    b = b.astype(BF16)
    in_specs = [pl.BlockSpec((tm, k), lambda j, i: (i, 0)),
                pl.BlockSpec((k, tn), lambda j, i: (0, j))]
    args = [a, b]
    kern = _mm_kernel
    if bias is not None:
        in_specs.append(pl.BlockSpec((1, tn), lambda j, i: (0, j)))
        args.append(bias.astype(F32).reshape(1, n))
        kern = _mm_bias_kernel
    return pl.pallas_call(
        kern,
        grid=(n // tn, m // tm),
        in_specs=in_specs,
        out_specs=pl.BlockSpec((tm, tn), lambda j, i: (i, j)),
        out_shape=jax.ShapeDtypeStruct((m, n), out_dtype),
        name=name,
        compiler_params=pltpu.CompilerParams(
            dimension_semantics=("arbitrary", "arbitrary"),
            vmem_limit_bytes=VMEM_LIMIT_BYTES),
    )(*args)


def _mm_glu_kernel(a_ref, b_ref, bias_ref, z_ref, o_ref, *, tn):
    j = pl.program_id(0)
    acc = jnp.dot(a_ref[...], b_ref[...], preferred_element_type=F32) + bias_ref[...]
    y = a_ref[:, pl.ds(pl.multiple_of(j * tn, tn), tn)].astype(F32)
    z = z_ref[...].astype(F32)
    o_ref[...] = (y / (1.0 + jnp.exp(-acc)) * (z / (1.0 + jnp.exp(-z)))).astype(o_ref.dtype)


def matmul_glu(y, w, bias, uz, *, tm, tn, name):
    m, e = y.shape
    assert w.shape == (e, e) and uz.shape == (m, 2 * e) and m % tm == 0 and e % tn == 0
    n_tiles = e // tn
    return pl.pallas_call(
        functools.partial(_mm_glu_kernel, tn=tn),
        grid=(n_tiles, m // tm),
        in_specs=[pl.BlockSpec((tm, e), lambda j, i: (i, 0)),
                  pl.BlockSpec((e, tn), lambda j, i: (0, j)),
                  pl.BlockSpec((1, tn), lambda j, i: (0, j)),
                  pl.BlockSpec((tm, tn), lambda j, i: (i, j + n_tiles))],
        out_specs=pl.BlockSpec((tm, tn), lambda j, i: (i, j)),
        out_shape=jax.ShapeDtypeStruct((m, e), BF16),
        name=name,
        compiler_params=pltpu.CompilerParams(
            dimension_semantics=("arbitrary", "arbitrary"),
            vmem_limit_bytes=VMEM_LIMIT_BYTES),
    )(y.astype(BF16), w.astype(BF16), bias.astype(F32).reshape(1, e), uz)


def _mm_lerp_kernel(h_ref, s_ref, mu_ref, w_ref, o_ref, lhs_ref):
    @pl.when(pl.program_id(2) == 0)
    def _():
        h = h_ref[...].astype(F32)
        lhs_ref[...] = (h + (s_ref[...].astype(F32) - h) * mu_ref[0]).astype(lhs_ref.dtype)

    o_ref[0] = jnp.dot(lhs_ref[...], w_ref[0], preferred_element_type=F32).astype(o_ref.dtype)


def matmul_lerp(h, shifted, mu, w, *, out_dtype, tm, tn, name):
    m, d = h.shape
    p, d2, n = w.shape
    assert d == d2 and mu.shape == (p, d) and m % tm == 0 and n % tn == 0
    row = pl.BlockSpec((tm, d), lambda i, q, j: (i, 0))
    return pl.pallas_call(
        _mm_lerp_kernel,
        grid=(m // tm, p, n // tn),
        in_specs=[row, row,
                  pl.BlockSpec((1, 1, d), lambda i, q, j: (q, 0, 0)),
                  pl.BlockSpec((1, d, tn), lambda i, q, j: (q, 0, j))],
        out_specs=pl.BlockSpec((1, tm, tn), lambda i, q, j: (q, i, j)),
        out_shape=jax.ShapeDtypeStruct((p, m, n), out_dtype),
        scratch_shapes=[pltpu.VMEM((tm, d), BF16)],
        name=name,
        compiler_params=pltpu.CompilerParams(
            dimension_semantics=("arbitrary", "arbitrary", "arbitrary"),
            vmem_limit_bytes=VMEM_LIMIT_BYTES),
    )(h, shifted, mu.astype(F32).reshape(p, 1, d), w.astype(BF16))


def _pick(total, pref):
    t = min(pref, total)
    while total % t:
        t //= 2
    return t


def _s5_prep_kernel(are_ref, aim_ref, ls_ref, bre_ref, bim_ref, cre_ref, cim_ref,
                    w1_ref, ot_ref, ar_ref, ai_ref, *, gb):
    T = S5_T
    GC = S5_GROUP
    lane = lax.broadcasted_iota(jnp.int32, (1, 128), 1)
    fwd = lane < S5_STATE
    s_col = lax.broadcasted_iota(jnp.int32, (T, 1), 0).astype(F32)
    ri = lax.broadcasted_iota(jnp.int32, (T * GC, T * GC), 0) % T
    ci = lax.broadcasted_iota(jnp.int32, (T * GC, T * GC), 1) % T
    lane2 = lax.broadcasted_iota(jnp.int32, (1, 256), 1) % 128
    fwd2 = lane2 < S5_STATE
    half = float(T // 2)

    for g in range(gb):
        a_re = are_ref[g]
        a_im = aim_ref[g]
        dt = jnp.exp(ls_ref[g])
        x_re = dt * a_re
        x_im = dt * a_im

        def lam_pow(p):
            mag = jnp.exp(p * x_re)
            ang = p * x_im
            return mag * jnp.cos(ang), mag * jnp.sin(ang)

        l_re, l_im = lam_pow(jnp.ones((1, 1), F32))
        den = a_re * a_re + a_im * a_im
        nr, ni = l_re - 1.0, l_im
        f_re = (nr * a_re + ni * a_im) / den
        f_im = (ni * a_re - nr * a_im) / den
        b_re = bre_ref[g]
        b_im = bim_ref[g]
        bb_re = f_re * b_re - f_im * b_im
        bb_im = f_re * b_im + f_im * b_re
        c_re = cre_ref[g]
        c_im = cim_ref[g]

        def times_pow(v_re, v_im, p):
            p_re, p_im = lam_pow(p)
            o_re = v_re[:, None, :] * p_re[None, :, :] - v_im[:, None, :] * p_im[None, :, :]
            o_im = v_re[:, None, :] * p_im[None, :, :] + v_im[:, None, :] * p_re[None, :, :]
            return o_re.reshape(GC * T, 128), o_im.reshape(GC * T, 128)

        pb = jnp.where(fwd, half - s_col, s_col - half)
        pc = -pb
        bl_re, bl_im = times_pow(bb_re, bb_im, pb)
        cl_re, cl_im = times_pow(c_re, c_im, pc)
        lhs = jnp.concatenate([bl_re, bl_im], axis=1)
        rhs = jnp.concatenate([cl_re, -cl_im], axis=1)
        r_hi, r_lo = _split2(rhs)

        def dot3(l):
            l_hi, l_lo = _split2(l)
            dn = (((1,), (1,)), ((), ()))
            return (lax.dot_general(l_hi, r_hi, dn, preferred_element_type=F32)
                    + lax.dot_general(l_hi, r_lo, dn, preferred_element_type=F32)
                    + lax.dot_general(l_lo, r_hi, dn, preferred_element_type=F32))

        t_f = dot3(jnp.where(fwd2, lhs, 0.0))
        t_b = dot3(jnp.where(fwd2, 0.0, lhs))
        toep = jnp.where(ci >= ri, t_f, 0.0) + jnp.where(ri >= ci, t_b, 0.0)

        ps = jnp.where(fwd, (T - 1.0) - s_col, s_col)
        s_re, s_im = times_pow(bb_re, bb_im, ps)
        w1_ref[g] = jnp.concatenate([toep, s_re, s_im], axis=1).astype(w1_ref.dtype)

        po = jnp.where(fwd, s_col + 1.0, T - s_col)
        o_re, o_im = times_pow(c_re, c_im, po)
        ot_ref[g] = jnp.concatenate([o_re, -o_im], axis=1).astype(ot_ref.dtype)

        pw_re, pw_im = lam_pow(jnp.full((1, 1), float(T), F32))
        ar_ref[g] = pw_re
        ai_ref[g] = pw_im


def _s5_kernel(u_ref, w1_ref, ot_ref, ar_ref, ai_ref, y_ref,
               p_ref, hf_re, hf_im, hb_re, hb_im, *, nb, n_ctx, n_chunks):
    gs = u_ref.shape[0]
    for g in range(gs):
        p_ref[g] = jnp.dot(u_ref[g], w1_ref[g], preferred_element_type=F32)
    ar = jnp.broadcast_to(ar_ref[...], (gs, nb, 128))
    ai = jnp.broadcast_to(ai_ref[...], (gs, nb, 128))
    fwd = lax.broadcasted_iota(jnp.int32, (gs, nb, 128), 2) < S5_STATE

    def step(i, carry):
        h_re, h_im = carry
        cb = jnp.where(i < n_ctx, n_ctx - 1 - i, n_chunks - 1 - (i - n_ctx))
        rf = pl.ds(pl.multiple_of(i * nb, nb), nb)
        rb = pl.ds(pl.multiple_of(cb * nb, nb), nb)
        hf_re[:, rf, :] = h_re
        hf_im[:, rf, :] = h_im
        hb_re[:, rb, :] = h_re
        hb_im[:, rb, :] = h_im
        x_re = jnp.where(fwd, p_ref[:, rf, 256:384], p_ref[:, rb, 256:384])
        x_im = jnp.where(fwd, p_ref[:, rf, 384:512], p_ref[:, rb, 384:512])
        n_re = ar * h_re - ai * h_im + x_re
        n_im = ar * h_im + ai * h_re + x_im
        return n_re, n_im

    zero = jnp.zeros((gs, nb, 128), F32)
    lax.fori_loop(0, n_chunks, step, (zero, zero))

    rows = p_ref.shape[1]
    fwd_all = lax.broadcasted_iota(jnp.int32, (rows, 128), 1) < S5_STATE
    for g in range(gs):
        h_all = jnp.concatenate([jnp.where(fwd_all, hf_re[g], hb_re[g]),
                                 jnp.where(fwd_all, hf_im[g], hb_im[g])], axis=1)
        y = p_ref[g, :, 0:256] + _dot_nt(h_all, ot_ref[g])
        y_ref[g] = y.astype(y_ref.dtype)


def s5_operators(a_re, a_im, log_step, b_re, b_im, c_re, c_im):
    n_dir, g, n = a_re.shape
    gb = 8

    def lanes(x):
        return jnp.concatenate([x[0], x[1]], axis=-1).astype(F32)

    are = lanes(a_re).reshape(g, 1, 2 * n)
    aim = lanes(a_im).reshape(g, 1, 2 * n)
    ls = lanes(jnp.broadcast_to(log_step[:, :, None], (n_dir, g, n))).reshape(g, 1, 2 * n)
    bre = lanes(jnp.swapaxes(b_re, -1, -2))
    bim = lanes(jnp.swapaxes(b_im, -1, -2))
    cre = lanes(c_re)
    cim = lanes(c_im)
    row = pl.BlockSpec((gb, 1, 2 * n), lambda i: (i, 0, 0))
    mat = pl.BlockSpec((gb, S5_GROUP, 2 * n), lambda i: (i, 0, 0))
    tg = S5_T * S5_GROUP
    return pl.pallas_call(
        functools.partial(_s5_prep_kernel, gb=gb),
        grid=(g // gb,),
        in_specs=[row, row, row, mat, mat, mat, mat],
        out_specs=[pl.BlockSpec((gb, tg, tg + 4 * n), lambda i: (i, 0, 0)),
                   pl.BlockSpec((gb, tg, 4 * n), lambda i: (i, 0, 0)),
                   row, row],
        out_shape=[jax.ShapeDtypeStruct((g, tg, tg + 4 * n), BF16),
                   jax.ShapeDtypeStruct((g, tg, 4 * n), BF16),
                   jax.ShapeDtypeStruct((g, 1, 2 * n), F32),
                   jax.ShapeDtypeStruct((g, 1, 2 * n), F32)],
        name="s5_operators",
        compiler_params=pltpu.CompilerParams(dimension_semantics=("arbitrary",),
                                             vmem_limit_bytes=VMEM_LIMIT_BYTES),
    )(are, aim, ls, bre, bim, cre, cim)


def s5_scan(u_g, w1, ot, ar, ai, *, nb, n_ctx, n_chunks):
    g, rows, tg = u_g.shape
    assert rows == n_chunks * nb and tg == S5_T * S5_GROUP
    gs = _pick(g, 4)
    blk = lambda w: pl.BlockSpec((gs,) + w, lambda i: (i, 0, 0))
    return pl.pallas_call(
        functools.partial(_s5_kernel, nb=nb, n_ctx=n_ctx, n_chunks=n_chunks),
        grid=(g // gs,),
        in_specs=[blk((rows, tg)), blk(w1.shape[1:]), blk(ot.shape[1:]),
                  blk((1, 128)), blk((1, 128))],
        out_specs=blk((rows, tg)),
        out_shape=jax.ShapeDtypeStruct((g, rows, tg), BF16),
        scratch_shapes=[pltpu.VMEM((gs, rows, w1.shape[2]), F32)]
                       + [pltpu.VMEM((gs, rows, 128), F32)] * 4,
        name="s5_scan",
        compiler_params=pltpu.CompilerParams(dimension_semantics=("arbitrary",),
                                             vmem_limit_bytes=VMEM_LIMIT_BYTES),
    )(u_g, w1, ot, ar, ai)


def _eye(n):
    r = lax.broadcasted_iota(jnp.int32, (n, n), 0)
    c = lax.broadcasted_iota(jnp.int32, (n, n), 1)
    return jnp.where(r == c, 1.0, 0.0).astype(BF16)


def _to_groups_kernel(u_ref, o_ref, *, nb, n_j):
    g = o_ref.shape[0]
    eye = _eye(g)
    for b in range(nb):
        x = u_ref[b]
        rows = jnp.concatenate([x[:, j * g:(j + 1) * g] for j in range(n_j)], axis=0)
        o_ref[:, b, :] = _dot_nt(eye, rows).astype(o_ref.dtype)


def _from_groups_kernel(y_ref, o_ref, *, nb, n_j):
    g = y_ref.shape[0]
    t = o_ref.shape[1]
    eye = _eye(y_ref.shape[2])
    for b in range(nb):
        cols = _dot_nt(eye, y_ref[:, b, :]).astype(o_ref.dtype)
        for i in range(n_j):
            o_ref[b, :, i * g:(i + 1) * g] = cols[i * t:(i + 1) * t, :]


def to_groups(uz, *, n_chunks):
    b, l, e2 = uz.shape
    e = e2 // 2
    g = e // S5_GROUP
    return pl.pallas_call(
        functools.partial(_to_groups_kernel, nb=b, n_j=S5_GROUP),
        grid=(n_chunks,),
        in_specs=[pl.BlockSpec((b, S5_T, e), lambda c: (0, c, 0))],
        out_specs=pl.BlockSpec((g, b, S5_T * S5_GROUP), lambda c: (0, c, 0)),
        out_shape=jax.ShapeDtypeStruct((g, n_chunks * b, S5_T * S5_GROUP), BF16),
        name="s5_to_groups",
        compiler_params=pltpu.CompilerParams(dimension_semantics=("arbitrary",),
                                             vmem_limit_bytes=VMEM_LIMIT_BYTES),
    )(uz)


def from_groups(y_g, *, nb, n_chunks):
    g, rows, tg = y_g.shape
    e = g * S5_GROUP
    return pl.pallas_call(
        functools.partial(_from_groups_kernel, nb=nb, n_j=S5_GROUP),
        grid=(n_chunks,),
        in_specs=[pl.BlockSpec((g, nb, tg), lambda c: (0, c, 0))],
        out_specs=pl.BlockSpec((nb, S5_T, e), lambda c: (0, c, 0)),
        out_shape=jax.ShapeDtypeStruct((nb, n_chunks * S5_T, e), BF16),
        name="s5_from_groups",
        compiler_params=pltpu.CompilerParams(dimension_semantics=("arbitrary",),
                                             vmem_limit_bytes=VMEM_LIMIT_BYTES),
    )(y_g)


def _rwkv_chunk_index(step, n_ctx, n_chunks):
    i = step % n_chunks
    back = jnp.where(i < n_ctx, n_ctx - 1 - i, n_chunks - 1 - (i - n_ctx))
    return jnp.where(step < n_chunks, i, back)


def _rwkv_kernel(r_ref, k_ref, v_ref, z_ref, tw_ref, ta_ref, w2_ref, a2_ref, w0_ref, a0_ref,
                 kk_ref, ka_ref, rk_ref, lnw_ref, lnb_ref, o_ref, s_ref, acc_ref,
                 *, n_ctx, n_chunks, n_pairs):
    C = RWKV_CHUNK
    step = pl.program_id(2)
    i = step % n_chunks
    is_fwd = step < n_chunks
    cidx = _rwkv_chunk_index(step, n_ctx, n_chunks)
    sgn = jnp.where(is_fwd, 1, -1)

    t_i = lax.broadcasted_iota(jnp.int32, (C, 128), 0)
    lane = lax.broadcasted_iota(jnp.int32, (C, 128), 1)
    s_i = lane % C
    lane_head = lane // RWKV_HEAD
    before = (s_i - t_i) * sgn
    strict = before < 0
    incl = before <= 0
    same_blk = (t_i // 16) == (s_i // 16)
    diag_m = strict & same_blk
    off_m = strict & jnp.logical_not(same_blk)
    tr = lax.broadcasted_iota(jnp.int32, (C, C), 0)
    tc = lax.broadcasted_iota(jnp.int32, (C, C), 1)
    cum_m = jnp.where((tc - tr) * sgn <= 0, 1.0, 0.0).astype(BF16)
    ri = lax.broadcasted_iota(jnp.int32, (128, 128), 0)
    ci = lax.broadcasted_iota(jnp.int32, (128, 128), 1)
    same_head = (ri // RWKV_HEAD) == (ci // RWKV_HEAD)
    head_ones = jnp.where(same_head, 1.0, 0.0).astype(BF16)

    @pl.when(i == 0)
    def _():
        s_ref[...] = jnp.zeros(s_ref.shape, F32)

    def stack(x):
        x = x.astype(BF16)
        zero = jnp.zeros_like(x)
        return jnp.concatenate([jnp.where(lane_head == 0, x, zero),
                                jnp.where(lane_head == 1, x, zero)], axis=0)

    def pack(x):
        return jnp.where(lane_head == 0, x[:C], x[C:])

    r = r_ref[0].astype(F32)
    k = k_ref[0].astype(F32)
    v = v_ref[0].astype(F32)
    d = jnp.where(is_fwd, 0, 1)
    wp = w0_ref[pl.ds(d, 1), :] + _dot(jnp.tanh(tw_ref[0, 0]), w2_ref[d])
    ap = a0_ref[pl.ds(d, 1), :] + _dot(ta_ref[d, 0], a2_ref[d])
    sp = jnp.maximum(-wp, 0.0) + jnp.log(1.0 + jnp.exp(-jnp.abs(wp)))
    lw = -jnp.exp(-sp - 0.5)
    a_rate = 1.0 / (1.0 + jnp.exp(-ap))
    k_a = ka_ref[...]
    kkr = k * kk_ref[...]
    kd = k * (1.0 + (a_rate - 1.0) * k_a)
    lc = sum(jnp.dot(cum_m, p, preferred_element_type=F32) for p in _split3(lw))
    l_mid = lc[C // 2:C // 2 + 1, :]
    l_end = jnp.where(is_fwd, lc[C - 1:C, :], lc[0:1, :])
    dl = lc - l_mid
    e_in = jnp.exp(dl)
    e_ex = jnp.exp(dl - lw)
    e_iv = jnp.exp(-dl)
    c_row = jnp.exp(l_mid)
    e2 = jnp.exp(l_end - l_mid)
    wc = jnp.exp(l_end)
    rt = r * e_in
    ar_iv = a_rate * e_iv
    kt = kd * e_iv

    pairs = range(n_pairs)

    def each(f, *lists):
        return [f(*args) for args in zip(*lists)]

    def cols(x):
        return [x[:, p * 128:(p + 1) * 128] for p in pairs]

    sq = each(lambda x: _split2(x * x), cols(kkr))
    ssq = each(lambda s: jnp.dot(s[0], head_ones, preferred_element_type=F32)
               + jnp.dot(s[1], head_ones, preferred_element_type=F32), sq)
    kk = each(lambda x, s: x / jnp.maximum(jnp.sqrt(s), NORM_EPS), cols(kkr), ssq)
    mm = lambda l, r_: jnp.dot(l.astype(BF16), r_, preferred_element_type=F32)
    wide = lambda l, *rs: mm(l, jnp.concatenate(rs, axis=1))
    stack2 = lambda x: jnp.concatenate([stack(x[:, :128]), stack(x[:, 128:])], axis=1)

    at = each(lambda x, e: -x * e, kk, cols(e_ex))
    bt = each(lambda x, e: x * e, kk, cols(ar_iv))
    rp, kp, vp_ = cols(rt), cols(kt), cols(v)
    vs = each(stack, vp_)
    a4 = each(lambda a, r_, b, k_: _dot_nt(jnp.concatenate([a, r_], axis=0),
                                           jnp.concatenate([stack(b), stack(k_)], axis=0)),
              at, rp, bt, kp)
    ak = each(lambda m: jnp.where(strict, m[:C, 128:], 0.0), a4)
    rb = each(lambda m: jnp.where(incl, m[C:, :128], 0.0), a4)
    rk = each(lambda m: jnp.where(incl, m[C:, 128:], 0.0), a4)
    akv = each(mm, ak, vs)
    ad = each(lambda m: jnp.where(diag_m, m[:C, :128], 0.0), a4)
    ao = each(lambda m: jnp.where(off_m, m[:C, :128], 0.0), a4)
    ad_s = each(stack, ad)
    p2 = each(mm, ad, ad_s)
    p43 = each(lambda p, a_s: wide(p, stack(p), a_s), p2, ad_s)
    p4 = [m[:, :128] for m in p43]
    s3 = each(lambda a, p, m: a + p + m[:, 128:], ad, p2, p43)
    p8s = each(lambda p, s: wide(p, stack(p), stack(s)), p4, s3)
    p8 = [m[:, :128] for m in p8s]
    s7 = each(lambda s, p, m: s + p + m[:, 128:], s3, p4, p8s)
    dm = each(lambda s, p: s + p + mm(p, stack(s)), s7, p8)
    x0 = each(lambda a, m: jnp.concatenate([a, m], axis=1), at, akv)
    dx = each(lambda d_, o, x: wide(d_, stack(o), stack2(x)), dm, ao, x0)
    n1 = each(lambda o, m: o + m[:, :128], ao, dx)
    xa = each(lambda x, m: x + m[:, 128:], x0, dx)
    xc = each(lambda n_, x: x + mm(n_, stack2(x)), n1, xa)
    w1_ = each(lambda n_, x: mm(n_, stack2(x)), n1, xc)
    x1 = each(lambda n_, x, w_: x + mm(n_, stack2(w_)), n1, xc, w1_)
    zr = each(lambda rb_, rk_, x, v_s: mm(
        jnp.concatenate([rb_, rk_], axis=1),
        jnp.concatenate([stack2(x), jnp.concatenate([jnp.zeros_like(v_s), v_s], axis=1)], axis=0)),
        rb, rk, x1, vs)
    ca = each(lambda x, c_: x[:, :128] * c_, x1, cols(c_row))
    cr = each(lambda r_, z_, c_: (r_ + z_[:, :128]) * c_, rp, zr, cols(c_row))
    yp = [z_[:, 128:] for z_ in zr]
    lt = each(lambda ca_, x, v_: jnp.concatenate(
        [jnp.concatenate([ca_, x[:, 128:]], axis=1),
         jnp.concatenate([jnp.zeros_like(v_), v_], axis=1)], axis=0), ca, x1, vp_)
    rt2 = each(lambda b, k_, e: jnp.concatenate([b * e, k_ * e], axis=0), bt, kp, cols(e2))
    mh = each(_dot_tn, lt, rt2)
    s_old = [s_ref[p] for p in pairs]
    ys = each(lambda c_, s, y_: _dot_nt(c_, stack(s)) + y_, cr, s_old, yp)
    s_new = each(lambda s, w_, m: s * w_ + mm(s, jnp.where(same_head, m[:128], 0.0).astype(BF16))
                 + pack(m[128:]), s_old, cols(wc), mh)
    for p in pairs:
        s_ref[p] = s_new[p]

    y = jnp.concatenate(ys, axis=1)
    orow = pl.multiple_of(jnp.maximum(cidx - n_ctx, 0) * C, C)

    @pl.when(jnp.logical_and(cidx >= n_ctx, is_fwd))
    def _():
        acc_ref[pl.ds(orow, C), :] = y

    @pl.when(jnp.logical_and(cidx >= n_ctx, jnp.logical_not(is_fwd)))
    def _():
        y_sum = acc_ref[pl.ds(orow, C), :] + y

        def head_sum(x):
            outs = []
            for xp in cols(x):
                hi, lo = _split2(xp)
                outs.append(jnp.dot(hi, head_ones, preferred_element_type=F32)
                            + jnp.dot(lo, head_ones, preferred_element_type=F32))
            return jnp.concatenate(outs, axis=1)

        inv = 1.0 / RWKV_HEAD
        dev = y_sum - head_sum(y_sum) * inv
        yn = dev * lax.rsqrt(head_sum(dev * dev) * inv + GN_EPS)
        yn = yn * lnw_ref[...] + lnb_ref[...]
        ar_f = 1.0 / (1.0 + jnp.exp(-(a0_ref[0:1, :] + _dot(ta_ref[0, 0], a2_ref[0]))))
        k_sum = k * (2.0 + (ar_f + a_rate - 2.0) * k_a)
        bonus = head_sum(r * k_sum * rk_ref[...]) * v
        zg = z_ref[0].astype(F32)
        o_ref[0] = ((yn + bonus) * (zg / (1.0 + jnp.exp(-zg)))).astype(o_ref.dtype)


def rwkv_scan(r, k, v, z, tw, ta, w2, a2, w0, a0, k_k, k_a, r_k, ln_w, ln_b, *, n_ctx_tokens):
    b, l, e = r.shape
    rank = tw.shape[-1]
    C = RWKV_CHUNK
    pw = min(e, 2048)
    n_pairs = pw // 128
    n_chunks = l // C
    n_ctx = n_ctx_tokens // C
    l_out = l - n_ctx_tokens
    cix = functools.partial(_rwkv_chunk_index, n_ctx=n_ctx, n_chunks=n_chunks)

    def out_chunk(t):
        c = cix(t)
        return jnp.where(jnp.logical_and(t >= n_chunks, c >= n_ctx), c, n_chunks - 1) - n_ctx

    seq = pl.BlockSpec((1, C, pw), lambda i, j, t: (i, cix(t), j))
    low_d = pl.BlockSpec((1, 1, C, rank), lambda i, j, t: (t // n_chunks, i, cix(t), 0))
    low_2 = pl.BlockSpec((2, 1, C, rank), lambda i, j, t: (0, i, cix(t), 0))
    up = pl.BlockSpec((2, rank, pw), lambda i, j, t: (0, 0, j))
    vec2 = pl.BlockSpec((2, pw), lambda i, j, t: (0, j))
    vec = pl.BlockSpec((1, pw), lambda i, j, t: (0, j))
    row = lambda x: x.astype(F32).reshape(1, e)
    return pl.pallas_call(
        functools.partial(_rwkv_kernel, n_ctx=n_ctx, n_chunks=n_chunks, n_pairs=n_pairs),
        grid=(b, e // pw, 2 * n_chunks),
        in_specs=[seq, seq, seq, seq, low_d, low_2, up, up, vec2, vec2, vec, vec, vec, vec, vec],
        out_specs=pl.BlockSpec((1, C, pw), lambda i, j, t: (i, out_chunk(t), j)),
        out_shape=jax.ShapeDtypeStruct((b, l_out, e), BF16),
        scratch_shapes=[pltpu.VMEM((n_pairs, C, 128), F32),
                        pltpu.VMEM((l_out, pw), F32)],
        name="rwkv_scan",
        compiler_params=pltpu.CompilerParams(
            dimension_semantics=("arbitrary", "arbitrary", "arbitrary"),
            vmem_limit_bytes=VMEM_LIMIT_BYTES),
    )(r, k, v, z, tw, ta, w2.astype(BF16), a2.astype(BF16), w0.astype(F32), a0.astype(F32),
      row(k_k), row(k_a), row(r_k), row(ln_w), row(ln_b))


def _rmsnorm(x, g):
    xf = x.astype(F32)
    y = xf * lax.rsqrt(jnp.mean(xf * xf, axis=-1, keepdims=True) + RMS_EPS)
    return y * g.astype(F32)


def _modulation(c, c_ctx, ada_w, ada_b):
    d = c.shape[-1]
    cond = jnp.concatenate([c, c_ctx[None]], axis=0)
    rows = cond.shape[0]
    pad = (-rows) % 16
    act = jnp.pad(jax.nn.silu(cond), ((0, pad), (0, 0)))
    m = matmul(act, ada_w, ada_b, out_dtype=F32, tm=rows + pad, tn=_pick(3 * d, 1024),
               name="mm_modulation")
    m = m[:rows]
    lat = [t[:, None, :] for t in jnp.split(m[:rows - 1], 3, axis=-1)]
    ctx = [t[None] for t in jnp.split(m[rows - 1:], 3, axis=-1)]
    return lat, ctx


def _q_shift(h):
    b, l, d = h.shape
    rows = l // GRID_W
    g = h.reshape(b, rows, GRID_W, d)
    q = d // 4
    left = jnp.pad(g[:, :, :-1, :q], ((0, 0), (0, 0), (1, 0), (0, 0)))
    right = jnp.pad(g[:, :, 1:, q:2 * q], ((0, 0), (0, 0), (0, 1), (0, 0)))
    up = jnp.pad(g[:, :-1, :, 2 * q:3 * q], ((0, 0), (1, 0), (0, 0), (0, 0)))
    down = jnp.pad(g[:, 1:, :, 3 * q:], ((0, 0), (0, 1), (0, 0), (0, 0)))
    return jnp.concatenate([left, right, up, down], axis=-1).reshape(b, l, d)


def _seq_shift(h):
    half = h.shape[-1] // 2
    prev = jnp.pad(h[:, :-1, :half], ((0, 0), (1, 0), (0, 0)))
    nxt = jnp.pad(h[:, 1:, half:], ((0, 0), (0, 1), (0, 0)))
    return jnp.concatenate([prev, nxt], axis=-1)


def _s5_layer(x_lat, x_ctx, c, c_ctx, norm_g, ada_w, ada_b, in_w, a_re, a_im, log_step,
              b_re, b_im, c_re, c_im, d_skip, glu_w, glu_b, out_w):
    b, l_lat, d = x_lat.shape
    l_ctx = x_ctx.shape[1]
    l = l_ctx + l_lat
    e = in_w.shape[1] // 2
    g = e // S5_GROUP
    (sh_l, sc_l, g_l), (sh_c, sc_c, g_c) = _modulation(c, c_ctx, ada_w, ada_b)
    h_lat = _rmsnorm(x_lat, norm_g) * (1.0 + sc_l) + sh_l
    h_ctx = _rmsnorm(x_ctx, norm_g) * (1.0 + sc_c) + sh_c
    h = jnp.concatenate([h_ctx, h_lat], axis=1).astype(BF16).reshape(b * l, d)
    m = b * l
    tm = _pick(m, 1024)
    te = _pick(e, 1024)

    lane = jnp.arange(e)
    old = (lane % g) * S5_GROUP + lane // g
    perm = (lane[:, None] == old[None, :]).astype(BF16)
    in_w_p = matmul(in_w.reshape(2 * d, e), perm, out_dtype=BF16, tm=_pick(2 * d, 1024), tn=te,
                    name="mm_perm_in").reshape(d, 2 * e)
    glu_w_p = matmul(glu_w.astype(BF16)[old], perm, out_dtype=BF16, tm=te, tn=te,
                     name="mm_perm_glu")
    out_w_p = out_w.astype(BF16)[old]
    d_p = d_skip.astype(F32)[old]
    glu_b_p = glu_b[old]

    uz = matmul(h, in_w_p, out_dtype=BF16, tm=tm, tn=_pick(2 * e, 1024), name="mm_s5_in")
    u, z = uz[:, :e], uz[:, e:]

    n_chunks = l // S5_T
    u_g = to_groups(uz.reshape(b, l, 2 * e), n_chunks=n_chunks)
    w1, ot, ar, ai = s5_operators(a_re, a_im, log_step, b_re, b_im, c_re, c_im)
    y_g = s5_scan(u_g, w1, ot, ar, ai, nb=b, n_ctx=l_ctx // S5_T, n_chunks=n_chunks)
    y_ssm = from_groups(y_g, nb=b, n_chunks=n_chunks).reshape(m, e)

    y1 = jax.nn.gelu(y_ssm.astype(F32) + d_p * u.astype(F32)).astype(BF16)
    y2 = matmul_glu(y1, glu_w_p, glu_b_p, uz, tm=tm, tn=te, name="mm_s5_glu")
    o = matmul(y2, out_w_p, out_dtype=F32, tm=tm, tn=_pick(d, 1024),
               name="mm_s5_out").reshape(b, l, d)
    x_ctx = x_ctx + g_c * o[:, :l_ctx]
    x_lat = x_lat + g_l * o[:, l_ctx:]
    return x_lat, x_ctx


def _rwkv_layer(x_lat, x_ctx, c, c_ctx, norm_g, ada_w, ada_b, mu, in_w, w0, w1, w2, a0, a1, a2,
                k_k, k_a, r_k, ln_w, ln_b, out_w):
    b, l_lat, d = x_lat.shape
    l_ctx = x_ctx.shape[1]
    l = l_ctx + l_lat
    e = in_w.shape[2]
    heads = e // RWKV_HEAD
    m = b * l
    (sh_l, sc_l, g_l), (sh_c, sc_c, _) = _modulation(c, c_ctx, ada_w, ada_b)
    h_lat = _rmsnorm(x_lat, norm_g) * (1.0 + sc_l) + sh_l
    h_ctx = _rmsnorm(x_ctx, norm_g) * (1.0 + sc_c) + sh_c
    h = jnp.concatenate([h_ctx, h_lat], axis=1).astype(BF16).reshape(m, d)
    shifted = (jnp.concatenate([_seq_shift(h_ctx), _q_shift(h_lat)], axis=1)
               .astype(BF16).reshape(m, d))

    tm = _pick(m, 1024)
    tn = _pick(e, 1024)
    rkvz = matmul_lerp(h, shifted, mu[:4], in_w, out_dtype=BF16, tm=tm, tn=tn, name="mm_rwkv_in")
    r, k, v, z = (rkvz[i].reshape(b, l, e) for i in range(4))

    n_dir, _, lora = w1.shape
    lp = (-lora) % 128
    down = jnp.stack([jnp.concatenate([jnp.pad(l1[i], ((0, 0), (0, lp))) for i in range(n_dir)],
                                      axis=1) for l1 in (w1, a1)])
    low = matmul_lerp(h, shifted, mu[4:6], down, out_dtype=F32, tm=tm, tn=down.shape[2],
                      name="mm_lora_down")
    tw, ta = (low[i].reshape(b, l, n_dir, lora + lp).transpose(2, 0, 1, 3) for i in range(2))

    pad_up = lambda l2: jnp.pad(l2, ((0, 0), (0, lp), (0, 0)))
    yg = rwkv_scan(r, k, v, z, tw, ta, pad_up(w2), pad_up(a2), w0, a0, k_k, k_a, r_k.reshape(e),
                   ln_w, ln_b, n_ctx_tokens=l_ctx)
    o = matmul(yg.reshape(b * l_lat, e), out_w, out_dtype=F32,
               tm=_pick(b * l_lat, 1024), tn=_pick(d, 1024), name="mm_rwkv_out").reshape(b, l_lat, d)
    return x_lat + g_l * o


def kernel(x, c, ctx, c_ctx, l0_norm_g, l0_ada_w, l0_ada_b, l0_in_w, l0_a_re, l0_a_im, l0_log_step, l0_b_re, l0_b_im, l0_c_re, l0_c_im, l0_d, l0_glu_w, l0_glu_b, l0_out_w, l1_norm_g, l1_ada_w, l1_ada_b, l1_mu, l1_in_w, l1_w0, l1_w1, l1_w2, l1_a0, l1_a1, l1_a2, l1_k_k, l1_k_a, l1_r_k, l1_ln_w, l1_ln_b, l1_out_w, final_norm_g):
    x_lat, x_ctx = _s5_layer(x, ctx, c, c_ctx, l0_norm_g, l0_ada_w, l0_ada_b, l0_in_w, l0_a_re,
                             l0_a_im, l0_log_step, l0_b_re, l0_b_im, l0_c_re, l0_c_im, l0_d,
                             l0_glu_w, l0_glu_b, l0_out_w)
    x_lat = _rwkv_layer(x_lat, x_ctx, c, c_ctx, l1_norm_g, l1_ada_w, l1_ada_b, l1_mu, l1_in_w,
                        l1_w0, l1_w1, l1_w2, l1_a0, l1_a1, l1_a2, l1_k_k, l1_k_a, l1_r_k,
                        l1_ln_w, l1_ln_b, l1_out_w)
    return _rmsnorm(x_lat, final_norm_g).astype(x.dtype)
```

```python
import functools

import jax
import jax.numpy as jnp
from jax import lax
from jax.experimental import pallas as pl
from jax.experimental.pallas import tpu as pltpu

F32 = jnp.float32
BF16 = jnp.bfloat16

GRID_W = 64
S5_GROUP = 16
S5_STATE = 64
S5_T = 16
RWKV_HEAD = 64
RWKV_CHUNK = 64
RMS_EPS = 1e-6
GN_EPS = 64e-5
NORM_EPS = 1e-12

VMEM_LIMIT_BYTES = 56 * 1024 * 1024


def _dot(a, b):
    return jnp.dot(a.astype(BF16), b.astype(BF16), preferred_element_type=F32)


def _dot_nt(a, b):
    return lax.dot_general(a.astype(BF16), b.astype(BF16), (((1,), (1,)), ((), ())),
                           preferred_element_type=F32)


def _dot_tn(a, b):
    return lax.dot_general(a.astype(BF16), b.astype(BF16), (((0,), (0,)), ((), ())),
                           preferred_element_type=F32)


def _split2(x):
    hi = x.astype(BF16)
    lo = (x - hi.astype(F32)).astype(BF16)
    return hi, lo


def _split3(x):
    p1 = x.astype(BF16)
    r1 = x - p1.astype(F32)
    p2 = r1.astype(BF16)
    p3 = (r1 - p2.astype(F32)).astype(BF16)
    return p1, p2, p3


def _mm_kernel(a_ref, b_ref, o_ref):
    o_ref[...] = jnp.dot(a_ref[...], b_ref[...],
                         preferred_element_type=F32).astype(o_ref.dtype)


def _mm_bias_kernel(a_ref, b_ref, bias_ref, o_ref):
    acc = jnp.dot(a_ref[...], b_ref[...], preferred_element_type=F32)
    o_ref[...] = (acc + bias_ref[...]).astype(o_ref.dtype)


def matmul(a, b, bias=None, *, out_dtype, tm, tn, name):
    m, k = a.shape
    k2, n = b.shape
    assert k == k2 and m % tm == 0 and n % tn == 0, (a.shape, b.shape, tm, tn)
    a = a.astype(BF16)
    b = b.astype(BF16)
    in_specs = [pl.BlockSpec((tm, k), lambda j, i: (i, 0)),
                pl.BlockSpec((k, tn), lambda j, i: (0, j))]
    args = [a, b]
    kern = _mm_kernel
    if bias is not None:
        in_specs.append(pl.BlockSpec((1, tn), lambda j, i: (0, j)))
        args.append(bias.astype(F32).reshape(1, n))
        kern = _mm_bias_kernel
    return pl.pallas_call(
        kern,
        grid=(n // tn, m // tm),
        in_specs=in_specs,
        out_specs=pl.BlockSpec((tm, tn), lambda j, i: (i, j)),
        out_shape=jax.ShapeDtypeStruct((m, n), out_dtype),
        name=name,
        compiler_params=pltpu.CompilerParams(
            dimension_semantics=("arbitrary", "arbitrary"),
            vmem_limit_bytes=VMEM_LIMIT_BYTES),
    )(*args)


def _mm_glu_kernel(a_ref, b_ref, bias_ref, z_ref, o_ref, *, tn):
    j = pl.program_id(0)
    acc = jnp.dot(a_ref[...], b_ref[...], preferred_element_type=F32) + bias_ref[...]
    y = a_ref[:, pl.ds(pl.multiple_of(j * tn, tn), tn)].astype(F32)
    z = z_ref[...].astype(F32)
    o_ref[...] = (y / (1.0 + jnp.exp(-acc)) * (z / (1.0 + jnp.exp(-z)))).astype(o_ref.dtype)


def matmul_glu(y, w, bias, uz, *, tm, tn, name):
    m, e = y.shape
    assert w.shape == (e, e) and uz.shape == (m, 2 * e) and m % tm == 0 and e % tn == 0
    n_tiles = e // tn
    return pl.pallas_call(
        functools.partial(_mm_glu_kernel, tn=tn),
        grid=(n_tiles, m // tm),
        in_specs=[pl.BlockSpec((tm, e), lambda j, i: (i, 0)),
                  pl.BlockSpec((e, tn), lambda j, i: (0, j)),
                  pl.BlockSpec((1, tn), lambda j, i: (0, j)),
                  pl.BlockSpec((tm, tn), lambda j, i: (i, j + n_tiles))],
        out_specs=pl.BlockSpec((tm, tn), lambda j, i: (i, j)),
        out_shape=jax.ShapeDtypeStruct((m, e), BF16),
        name=name,
        compiler_params=pltpu.CompilerParams(
            dimension_semantics=("arbitrary", "arbitrary"),
            vmem_limit_bytes=VMEM_LIMIT_BYTES),
    )(y.astype(BF16), w.astype(BF16), bias.astype(F32).reshape(1, e), uz)


def _mm_lerp_kernel(h_ref, s_ref, mu_ref, w_ref, o_ref, lhs_ref):
    @pl.when(pl.program_id(2) == 0)
    def _():
        h = h_ref[...].astype(F32)
        lhs_ref[...] = (h + (s_ref[...].astype(F32) - h) * mu_ref[0]).astype(lhs_ref.dtype)

    o_ref[0] = jnp.dot(lhs_ref[...], w_ref[0], preferred_element_type=F32).astype(o_ref.dtype)


def matmul_lerp(h, shifted, mu, w, *, out_dtype, tm, tn, name):
    m, d = h.shape
    p, d2, n = w.shape
    assert d == d2 and mu.shape == (p, d) and m % tm == 0 and n % tn == 0
    row = pl.BlockSpec((tm, d), lambda i, q, j: (i, 0))
    return pl.pallas_call(
        _mm_lerp_kernel,
        grid=(m // tm, p, n // tn),
        in_specs=[row, row,
                  pl.BlockSpec((1, 1, d), lambda i, q, j: (q, 0, 0)),
                  pl.BlockSpec((1, d, tn), lambda i, q, j: (q, 0, j))],
        out_specs=pl.BlockSpec((1, tm, tn), lambda i, q, j: (q, i, j)),
        out_shape=jax.ShapeDtypeStruct((p, m, n), out_dtype),
        scratch_shapes=[pltpu.VMEM((tm, d), BF16)],
        name=name,
        compiler_params=pltpu.CompilerParams(
            dimension_semantics=("arbitrary", "arbitrary", "arbitrary"),
            vmem_limit_bytes=VMEM_LIMIT_BYTES),
    )(h, shifted, mu.astype(F32).reshape(p, 1, d), w.astype(BF16))


def _pick(total, pref):
    t = min(pref, total)
    while total % t:
        t //= 2
    return t


def _s5_prep_kernel(are_ref, aim_ref, ls_ref, bre_ref, bim_ref, cre_ref, cim_ref,
                    w1_ref, ot_ref, ar_ref, ai_ref, *, gb):
    T = S5_T
    GC = S5_GROUP
    lane = lax.broadcasted_iota(jnp.int32, (1, 128), 1)
    fwd = lane < S5_STATE
    s_col = lax.broadcasted_iota(jnp.int32, (T, 1), 0).astype(F32)
    ri = lax.broadcasted_iota(jnp.int32, (T * GC, T * GC), 0) % T
    ci = lax.broadcasted_iota(jnp.int32, (T * GC, T * GC), 1) % T
    lane2 = lax.broadcasted_iota(jnp.int32, (1, 256), 1) % 128
    fwd2 = lane2 < S5_STATE
    half = float(T // 2)

    for g in range(gb):
        a_re = are_ref[g]
        a_im = aim_ref[g]
        dt = jnp.exp(ls_ref[g])
        x_re = dt * a_re
        x_im = dt * a_im

        def lam_pow(p):
            mag = jnp.exp(p * x_re)
            ang = p * x_im
            return mag * jnp.cos(ang), mag * jnp.sin(ang)

        l_re, l_im = lam_pow(jnp.ones((1, 1), F32))
        den = a_re * a_re + a_im * a_im
        nr, ni = l_re - 1.0, l_im
        f_re = (nr * a_re + ni * a_im) / den
        f_im = (ni * a_re - nr * a_im) / den
        b_re = bre_ref[g]
        b_im = bim_ref[g]
        bb_re = f_re * b_re - f_im * b_im
        bb_im = f_re * b_im + f_im * b_re
        c_re = cre_ref[g]
        c_im = cim_ref[g]

        def times_pow(v_re, v_im, p):
            p_re, p_im = lam_pow(p)
            o_re = v_re[:, None, :] * p_re[None, :, :] - v_im[:, None, :] * p_im[None, :, :]
            o_im = v_re[:, None, :] * p_im[None, :, :] + v_im[:, None, :] * p_re[None, :, :]
            return o_re.reshape(GC * T, 128), o_im.reshape(GC * T, 128)

        pb = jnp.where(fwd, half - s_col, s_col - half)
        pc = -pb
        bl_re, bl_im = times_pow(bb_re, bb_im, pb)
        cl_re, cl_im = times_pow(c_re, c_im, pc)
        lhs = jnp.concatenate([bl_re, bl_im], axis=1)
        rhs = jnp.concatenate([cl_re, -cl_im], axis=1)
        r_hi, r_lo = _split2(rhs)

        def dot3(l):
            l_hi, l_lo = _split2(l)
            dn = (((1,), (1,)), ((), ()))
            return (lax.dot_general(l_hi, r_hi, dn, preferred_element_type=F32)
                    + lax.dot_general(l_hi, r_lo, dn, preferred_element_type=F32)
                    + lax.dot_general(l_lo, r_hi, dn, preferred_element_type=F32))

        t_f = dot3(jnp.where(fwd2, lhs, 0.0))
        t_b = dot3(jnp.where(fwd2, 0.0, lhs))
        toep = jnp.where(ci >= ri, t_f, 0.0) + jnp.where(ri >= ci, t_b, 0.0)

        ps = jnp.where(fwd, (T - 1.0) - s_col, s_col)
        s_re, s_im = times_pow(bb_re, bb_im, ps)
        w1_ref[g] = jnp.concatenate([toep, s_re, s_im], axis=1).astype(w1_ref.dtype)

        po = jnp.where(fwd, s_col + 1.0, T - s_col)
        o_re, o_im = times_pow(c_re, c_im, po)
        ot_ref[g] = jnp.concatenate([o_re, -o_im], axis=1).astype(ot_ref.dtype)

        pw_re, pw_im = lam_pow(jnp.full((1, 1), float(T), F32))
        ar_ref[g] = pw_re
        ai_ref[g] = pw_im


def _s5_kernel(u_ref, w1_ref, ot_ref, ar_ref, ai_ref, y_ref,
               p_ref, hf_re, hf_im, hb_re, hb_im, *, nb, n_ctx, n_chunks):
    gs = u_ref.shape[0]
    for g in range(gs):
        p_ref[g] = jnp.dot(u_ref[g], w1_ref[g], preferred_element_type=F32)
    ar = jnp.broadcast_to(ar_ref[...], (gs, nb, 128))
    ai = jnp.broadcast_to(ai_ref[...], (gs, nb, 128))
    fwd = lax.broadcasted_iota(jnp.int32, (gs, nb, 128), 2) < S5_STATE

    def step(i, carry):
        h_re, h_im = carry
        cb = jnp.where(i < n_ctx, n_ctx - 1 - i, n_chunks - 1 - (i - n_ctx))
        rf = pl.ds(pl.multiple_of(i * nb, nb), nb)
        rb = pl.ds(pl.multiple_of(cb * nb, nb), nb)
        hf_re[:, rf, :] = h_re
        hf_im[:, rf, :] = h_im
        hb_re[:, rb, :] = h_re
        hb_im[:, rb, :] = h_im
        x_re = jnp.where(fwd, p_ref[:, rf, 256:384], p_ref[:, rb, 256:384])
        x_im = jnp.where(fwd, p_ref[:, rf, 384:512], p_ref[:, rb, 384:512])
        n_re = ar * h_re - ai * h_im + x_re
        n_im = ar * h_im + ai * h_re + x_im
        return n_re, n_im

    zero = jnp.zeros((gs, nb, 128), F32)
    lax.fori_loop(0, n_chunks, step, (zero, zero))

    rows = p_ref.shape[1]
    fwd_all = lax.broadcasted_iota(jnp.int32, (rows, 128), 1) < S5_STATE
    for g in range(gs):
        h_all = jnp.concatenate([jnp.where(fwd_all, hf_re[g], hb_re[g]),
                                 jnp.where(fwd_all, hf_im[g], hb_im[g])], axis=1)
        y = p_ref[g, :, 0:256] + _dot_nt(h_all, ot_ref[g])
        y_ref[g] = y.astype(y_ref.dtype)


def s5_operators(a_re, a_im, log_step, b_re, b_im, c_re, c_im):
    n_dir, g, n = a_re.shape
    gb = 8

    def lanes(x):
        return jnp.concatenate([x[0], x[1]], axis=-1).astype(F32)

    are = lanes(a_re).reshape(g, 1, 2 * n)
    aim = lanes(a_im).reshape(g, 1, 2 * n)
    ls = lanes(jnp.broadcast_to(log_step[:, :, None], (n_dir, g, n))).reshape(g, 1, 2 * n)
    bre = lanes(jnp.swapaxes(b_re, -1, -2))
    bim = lanes(jnp.swapaxes(b_im, -1, -2))
    cre = lanes(c_re)
    cim = lanes(c_im)
    row = pl.BlockSpec((gb, 1, 2 * n), lambda i: (i, 0, 0))
    mat = pl.BlockSpec((gb, S5_GROUP, 2 * n), lambda i: (i, 0, 0))
    tg = S5_T * S5_GROUP
    return pl.pallas_call(
        functools.partial(_s5_prep_kernel, gb=gb),
        grid=(g // gb,),
        in_specs=[row, row, row, mat, mat, mat, mat],
        out_specs=[pl.BlockSpec((gb, tg, tg + 4 * n), lambda i: (i, 0, 0)),
                   pl.BlockSpec((gb, tg, 4 * n), lambda i: (i, 0, 0)),
                   row, row],
        out_shape=[jax.ShapeDtypeStruct((g, tg, tg + 4 * n), BF16),
                   jax.ShapeDtypeStruct((g, tg, 4 * n), BF16),
                   jax.ShapeDtypeStruct((g, 1, 2 * n), F32),
                   jax.ShapeDtypeStruct((g, 1, 2 * n), F32)],
        name="s5_operators",
        compiler_params=pltpu.CompilerParams(dimension_semantics=("arbitrary",),
                                             vmem_limit_bytes=VMEM_LIMIT_BYTES),
    )(are, aim, ls, bre, bim, cre, cim)


def s5_scan(u_g, w1, ot, ar, ai, *, nb, n_ctx, n_chunks):
    g, rows, tg = u_g.shape
    assert rows == n_chunks * nb and tg == S5_T * S5_GROUP
    gs = _pick(g, 4)
    blk = lambda w: pl.BlockSpec((gs,) + w, lambda i: (i, 0, 0))
    return pl.pallas_call(
        functools.partial(_s5_kernel, nb=nb, n_ctx=n_ctx, n_chunks=n_chunks),
        grid=(g // gs,),
        in_specs=[blk((rows, tg)), blk(w1.shape[1:]), blk(ot.shape[1:]),
                  blk((1, 128)), blk((1, 128))],
        out_specs=blk((rows, tg)),
        out_shape=jax.ShapeDtypeStruct((g, rows, tg), BF16),
        scratch_shapes=[pltpu.VMEM((gs, rows, w1.shape[2]), F32)]
                       + [pltpu.VMEM((gs, rows, 128), F32)] * 4,
        name="s5_scan",
        compiler_params=pltpu.CompilerParams(dimension_semantics=("arbitrary",),
                                             vmem_limit_bytes=VMEM_LIMIT_BYTES),
    )(u_g, w1, ot, ar, ai)


def _eye(n):
    r = lax.broadcasted_iota(jnp.int32, (n, n), 0)
    c = lax.broadcasted_iota(jnp.int32, (n, n), 1)
    return jnp.where(r == c, 1.0, 0.0).astype(BF16)


def _to_groups_kernel(u_ref, o_ref, *, nb, n_j):
    g = o_ref.shape[0]
    eye = _eye(g)
    for b in range(nb):
        x = u_ref[b]
        rows = jnp.concatenate([x[:, j * g:(j + 1) * g] for j in range(n_j)], axis=0)
        o_ref[:, b, :] = _dot_nt(eye, rows).astype(o_ref.dtype)


def _from_groups_kernel(y_ref, o_ref, *, nb, n_j):
    g = y_ref.shape[0]
    t = o_ref.shape[1]
    eye = _eye(y_ref.shape[2])
    for b in range(nb):
        cols = _dot_nt(eye, y_ref[:, b, :]).astype(o_ref.dtype)
        for i in range(n_j):
            o_ref[b, :, i * g:(i + 1) * g] = cols[i * t:(i + 1) * t, :]


def to_groups(uz, *, n_chunks):
    b, l, e2 = uz.shape
    e = e2 // 2
    g = e // S5_GROUP
    return pl.pallas_call(
        functools.partial(_to_groups_kernel, nb=b, n_j=S5_GROUP),
        grid=(n_chunks,),
        in_specs=[pl.BlockSpec((b, S5_T, e), lambda c: (0, c, 0))],
        out_specs=pl.BlockSpec((g, b, S5_T * S5_GROUP), lambda c: (0, c, 0)),
        out_shape=jax.ShapeDtypeStruct((g, n_chunks * b, S5_T * S5_GROUP), BF16),
        name="s5_to_groups",
        compiler_params=pltpu.CompilerParams(dimension_semantics=("arbitrary",),
                                             vmem_limit_bytes=VMEM_LIMIT_BYTES),
    )(uz)


def from_groups(y_g, *, nb, n_chunks):
    g, rows, tg = y_g.shape
    e = g * S5_GROUP
    return pl.pallas_call(
        functools.partial(_from_groups_kernel, nb=nb, n_j=S5_GROUP),
        grid=(n_chunks,),
        in_specs=[pl.BlockSpec((g, nb, tg), lambda c: (0, c, 0))],
        out_specs=pl.BlockSpec((nb, S5_T, e), lambda c: (0, c, 0)),
        out_shape=jax.ShapeDtypeStruct((nb, n_chunks * S5_T, e), BF16),
        name="s5_from_groups",
        compiler_params=pltpu.CompilerParams(dimension_semantics=("arbitrary",),
                                             vmem_limit_bytes=VMEM_LIMIT_BYTES),
    )(y_g)


def _rwkv_chunk_index(step, n_ctx, n_chunks):
    i = step % n_chunks
    back = jnp.where(i < n_ctx, n_ctx - 1 - i, n_chunks - 1 - (i - n_ctx))
    return jnp.where(step < n_chunks, i, back)


def _rwkv_kernel(r_ref, k_ref, v_ref, z_ref, tw_ref, ta_ref, w2_ref, a2_ref, w0_ref, a0_ref,
                 kk_ref, ka_ref, rk_ref, lnw_ref, lnb_ref, o_ref, s_ref, acc_ref,
                 *, n_ctx, n_chunks, n_pairs):
    C = RWKV_CHUNK
    step = pl.program_id(2)
    i = step % n_chunks
    is_fwd = step < n_chunks
    cidx = _rwkv_chunk_index(step, n_ctx, n_chunks)
    sgn = jnp.where(is_fwd, 1, -1)

    t_i = lax.broadcasted_iota(jnp.int32, (C, 128), 0)
    lane = lax.broadcasted_iota(jnp.int32, (C, 128), 1)
    s_i = lane % C
    lane_head = lane // RWKV_HEAD
    before = (s_i - t_i) * sgn
    strict = before < 0
    incl = before <= 0
    same_blk = (t_i // 16) == (s_i // 16)
    diag_m = strict & same_blk
    off_m = strict & jnp.logical_not(same_blk)
    tr = lax.broadcasted_iota(jnp.int32, (C, C), 0)
    tc = lax.broadcasted_iota(jnp.int32, (C, C), 1)
    cum_m = jnp.where((tc - tr) * sgn <= 0, 1.0, 0.0).astype(BF16)
    ri = lax.broadcasted_iota(jnp.int32, (128, 128), 0)
    ci = lax.broadcasted_iota(jnp.int32, (128, 128), 1)
    same_head = (ri // RWKV_HEAD) == (ci // RWKV_HEAD)
    head_ones = jnp.where(same_head, 1.0, 0.0).astype(BF16)

    @pl.when(i == 0)
    def _():
        s_ref[...] = jnp.zeros(s_ref.shape, F32)

    def stack(x):
        x = x.astype(BF16)
        zero = jnp.zeros_like(x)
        return jnp.concatenate([jnp.where(lane_head == 0, x, zero),
                                jnp.where(lane_head == 1, x, zero)], axis=0)

    def pack(x):
        return jnp.where(lane_head == 0, x[:C], x[C:])

    r = r_ref[0].astype(F32)
    k = k_ref[0].astype(F32)
    v = v_ref[0].astype(F32)
    d = jnp.where(is_fwd, 0, 1)
    wp = w0_ref[pl.ds(d, 1), :] + _dot(jnp.tanh(tw_ref[0, 0]), w2_ref[d])
    ap = a0_ref[pl.ds(d, 1), :] + _dot(ta_ref[d, 0], a2_ref[d])
    sp = jnp.maximum(-wp, 0.0) + jnp.log(1.0 + jnp.exp(-jnp.abs(wp)))
    lw = -jnp.exp(-sp - 0.5)
    a_rate = 1.0 / (1.0 + jnp.exp(-ap))
    k_a = ka_ref[...]
    kkr = k * kk_ref[...]
    kd = k * (1.0 + (a_rate - 1.0) * k_a)
    lc = sum(jnp.dot(cum_m, p, preferred_element_type=F32) for p in _split3(lw))
    l_mid = lc[C // 2:C // 2 + 1, :]
    l_end = jnp.where(is_fwd, lc[C - 1:C, :], lc[0:1, :])
    dl = lc - l_mid
    e_in = jnp.exp(dl)
    e_ex = jnp.exp(dl - lw)
    e_iv = jnp.exp(-dl)
    c_row = jnp.exp(l_mid)
    e2 = jnp.exp(l_end - l_mid)
    wc = jnp.exp(l_end)
    rt = r * e_in
    ar_iv = a_rate * e_iv
    kt = kd * e_iv

    pairs = range(n_pairs)

    def each(f, *lists):
        return [f(*args) for args in zip(*lists)]

    def cols(x):
        return [x[:, p * 128:(p + 1) * 128] for p in pairs]

    def head_sum(x):
        hi, lo = _split2(jnp.concatenate(cols(x), axis=0))
        tot = (jnp.dot(hi, head_ones, preferred_element_type=F32)
               + jnp.dot(lo, head_ones, preferred_element_type=F32))
        return jnp.concatenate([tot[p * C:(p + 1) * C] for p in pairs], axis=1)

    kk_all = kkr / jnp.maximum(jnp.sqrt(head_sum(kkr * kkr)), NORM_EPS)
    kk = cols(kk_all)
    mm = lambda l, r_: jnp.dot(l.astype(BF16), r_, preferred_element_type=F32)
    wide = lambda l, *rs: mm(l, jnp.concatenate(rs, axis=1))
    stack2 = lambda x: jnp.concatenate([stack(x[:, :128]), stack(x[:, 128:])], axis=1)

    at = each(lambda x, e: -x * e, kk, cols(e_ex))
    bt = each(lambda x, e: x * e, kk, cols(ar_iv))
    rp, kp, vp_ = cols(rt), cols(kt), cols(v)
    vs = each(stack, vp_)
    a4 = each(lambda a, r_, b, k_: _dot_nt(jnp.concatenate([a, r_], axis=0),
                                           jnp.concatenate([stack(b), stack(k_)], axis=0)),
              at, rp, bt, kp)
    ak = each(lambda m: jnp.where(strict, m[:C, 128:], 0.0), a4)
    rb = each(lambda m: jnp.where(incl, m[C:, :128], 0.0), a4)
    rk = each(lambda m: jnp.where(incl, m[C:, 128:], 0.0), a4)
    akv = each(mm, ak, vs)
    ad = each(lambda m: jnp.where(diag_m, m[:C, :128], 0.0), a4)
    ao = each(lambda m: jnp.where(off_m, m[:C, :128], 0.0), a4)
    ad_s = each(stack, ad)
    p2 = each(mm, ad, ad_s)
    p43 = each(lambda p, a_s: wide(p, stack(p), a_s), p2, ad_s)
    p4 = [m[:, :128] for m in p43]
    s3 = each(lambda a, p, m: a + p + m[:, 128:], ad, p2, p43)
    p8s = each(lambda p, s: wide(p, stack(p), stack(s)), p4, s3)
    p8 = [m[:, :128] for m in p8s]
    s7 = each(lambda s, p, m: s + p + m[:, 128:], s3, p4, p8s)
    dm = each(lambda s, p: s + p + mm(p, stack(s)), s7, p8)
    s_old = [s_ref[p] for p in pairs]
    asr = each(lambda a, r_, c_, s: _dot_nt(jnp.concatenate([a * c_, r_ * c_], axis=0), stack(s)),
               at, rp, cols(c_row), s_old)
    x0 = each(lambda m, q: m[:C] + q, asr, akv)
    dx = each(lambda d_, o, x: wide(d_, stack(o), stack(x)), dm, ao, x0)
    n1 = each(lambda o, m: o + m[:, :128], ao, dx)
    xa = each(lambda x, m: x + m[:, 128:], x0, dx)
    xc = each(lambda n_, x: x + mm(n_, stack(x)), n1, xa)
    w1_ = each(lambda n_, x: mm(n_, stack(x)), n1, xc)
    u = each(lambda n_, x, w_: x + mm(n_, stack(w_)), n1, xc, w1_)
    ys = each(lambda m, rb_, rk_, u_, v_s: m[C:] + mm(
        jnp.concatenate([rb_, rk_], axis=1), jnp.concatenate([stack(u_), v_s], axis=0)),
        asr, rb, rk, u, vs)
    upd = each(lambda u_, v_, b, k_, e: pack(_dot_tn(
        jnp.concatenate([u_, v_], axis=0), jnp.concatenate([b * e, k_ * e], axis=0))),
        u, vp_, bt, kp, cols(e2))
    for p in pairs:
        s_ref[p] = s_old[p] * wc[:, p * 128:(p + 1) * 128] + upd[p]

    y = jnp.concatenate(ys, axis=1)
    orow = pl.multiple_of(jnp.maximum(cidx - n_ctx, 0) * C, C)

    @pl.when(jnp.logical_and(cidx >= n_ctx, is_fwd))
    def _():
        acc_ref[pl.ds(orow, C), :] = y

    @pl.when(jnp.logical_and(cidx >= n_ctx, jnp.logical_not(is_fwd)))
    def _():
        y_sum = acc_ref[pl.ds(orow, C), :] + y

        inv = 1.0 / RWKV_HEAD
        dev = y_sum - head_sum(y_sum) * inv
        yn = dev * lax.rsqrt(head_sum(dev * dev) * inv + GN_EPS)
        yn = yn * lnw_ref[...] + lnb_ref[...]
        ar_f = 1.0 / (1.0 + jnp.exp(-(a0_ref[0:1, :] + _dot(ta_ref[0, 0], a2_ref[0]))))
        k_sum = k * (2.0 + (ar_f + a_rate - 2.0) * k_a)
        bonus = head_sum(r * k_sum * rk_ref[...]) * v
        zg = z_ref[0].astype(F32)
        o_ref[0] = ((yn + bonus) * (zg / (1.0 + jnp.exp(-zg)))).astype(o_ref.dtype)


def rwkv_scan(r, k, v, z, tw, ta, w2, a2, w0, a0, k_k, k_a, r_k, ln_w, ln_b, *, n_ctx_tokens):
    b, l, e = r.shape
    rank = tw.shape[-1]
    C = RWKV_CHUNK
    pw = min(e, 2048)
    n_pairs = pw // 128
    n_chunks = l // C
    n_ctx = n_ctx_tokens // C
    l_out = l - n_ctx_tokens
    cix = functools.partial(_rwkv_chunk_index, n_ctx=n_ctx, n_chunks=n_chunks)

    def out_chunk(t):
        c = cix(t)
        return jnp.where(jnp.logical_and(t >= n_chunks, c >= n_ctx), c, n_chunks - 1) - n_ctx

    seq = pl.BlockSpec((1, C, pw), lambda i, j, t: (i, cix(t), j))
    low_d = pl.BlockSpec((1, 1, C, rank), lambda i, j, t: (t // n_chunks, i, cix(t), 0))
    low_2 = pl.BlockSpec((2, 1, C, rank), lambda i, j, t: (0, i, cix(t), 0))
    up = pl.BlockSpec((2, rank, pw), lambda i, j, t: (0, 0, j))
    vec2 = pl.BlockSpec((2, pw), lambda i, j, t: (0, j))
    vec = pl.BlockSpec((1, pw), lambda i, j, t: (0, j))
    row = lambda x: x.astype(F32).reshape(1, e)
    return pl.pallas_call(
        functools.partial(_rwkv_kernel, n_ctx=n_ctx, n_chunks=n_chunks, n_pairs=n_pairs),
        grid=(b, e // pw, 2 * n_chunks),
        in_specs=[seq, seq, seq, seq, low_d, low_2, up, up, vec2, vec2, vec, vec, vec, vec, vec],
        out_specs=pl.BlockSpec((1, C, pw), lambda i, j, t: (i, out_chunk(t), j)),
        out_shape=jax.ShapeDtypeStruct((b, l_out, e), BF16),
        scratch_shapes=[pltpu.VMEM((n_pairs, C, 128), F32),
                        pltpu.VMEM((l_out, pw), F32)],
        name="rwkv_scan",
        compiler_params=pltpu.CompilerParams(
            dimension_semantics=("arbitrary", "arbitrary", "arbitrary"),
            vmem_limit_bytes=VMEM_LIMIT_BYTES),
    )(r, k, v, z, tw, ta, w2.astype(BF16), a2.astype(BF16), w0.astype(F32), a0.astype(F32),
      row(k_k), row(k_a), row(r_k), row(ln_w), row(ln_b))


def _rmsnorm(x, g):
    xf = x.astype(F32)
    y = xf * lax.rsqrt(jnp.mean(xf * xf, axis=-1, keepdims=True) + RMS_EPS)
    return y * g.astype(F32)


def _modulation(c, c_ctx, ada_w, ada_b):
    d = c.shape[-1]
    cond = jnp.concatenate([c, c_ctx[None]], axis=0)
    rows = cond.shape[0]
    pad = (-rows) % 16
    act = jnp.pad(jax.nn.silu(cond), ((0, pad), (0, 0)))
    m = matmul(act, ada_w, ada_b, out_dtype=F32, tm=rows + pad, tn=_pick(3 * d, 1024),
               name="mm_modulation")
    m = m[:rows]
    lat = [t[:, None, :] for t in jnp.split(m[:rows - 1], 3, axis=-1)]
    ctx = [t[None] for t in jnp.split(m[rows - 1:], 3, axis=-1)]
    return lat, ctx


def _q_shift(h):
    b, l, d = h.shape
    rows = l // GRID_W
    g = h.reshape(b, rows, GRID_W, d)
    q = d // 4
    left = jnp.pad(g[:, :, :-1, :q], ((0, 0), (0, 0), (1, 0), (0, 0)))
    right = jnp.pad(g[:, :, 1:, q:2 * q], ((0, 0), (0, 0), (0, 1), (0, 0)))
    up = jnp.pad(g[:, :-1, :, 2 * q:3 * q], ((0, 0), (1, 0), (0, 0), (0, 0)))
    down = jnp.pad(g[:, 1:, :, 3 * q:], ((0, 0), (0, 1), (0, 0), (0, 0)))
    return jnp.concatenate([left, right, up, down], axis=-1).reshape(b, l, d)


def _seq_shift(h):
    half = h.shape[-1] // 2
    prev = jnp.pad(h[:, :-1, :half], ((0, 0), (1, 0), (0, 0)))
    nxt = jnp.pad(h[:, 1:, half:], ((0, 0), (0, 1), (0, 0)))
    return jnp.concatenate([prev, nxt], axis=-1)


def _s5_layer(x_lat, x_ctx, c, c_ctx, norm_g, ada_w, ada_b, in_w, a_re, a_im, log_step,
              b_re, b_im, c_re, c_im, d_skip, glu_w, glu_b, out_w):
    b, l_lat, d = x_lat.shape
    l_ctx = x_ctx.shape[1]
    l = l_ctx + l_lat
    e = in_w.shape[1] // 2
    g = e // S5_GROUP
    (sh_l, sc_l, g_l), (sh_c, sc_c, g_c) = _modulation(c, c_ctx, ada_w, ada_b)
    h_lat = _rmsnorm(x_lat, norm_g) * (1.0 + sc_l) + sh_l
    h_ctx = _rmsnorm(x_ctx, norm_g) * (1.0 + sc_c) + sh_c
    h = jnp.concatenate([h_ctx, h_lat], axis=1).astype(BF16).reshape(b * l, d)
    m = b * l
    tm = _pick(m, 1024)
    te = _pick(e, 1024)

    lane = jnp.arange(e)
    old = (lane % g) * S5_GROUP + lane // g
    perm = (lane[:, None] == old[None, :]).astype(BF16)
    in_w_p = matmul(in_w.reshape(2 * d, e), perm, out_dtype=BF16, tm=_pick(2 * d, 1024), tn=te,
                    name="mm_perm_in").reshape(d, 2 * e)
    glu_w_p = matmul(glu_w.astype(BF16)[old], perm, out_dtype=BF16, tm=te, tn=te,
                     name="mm_perm_glu")
    out_w_p = out_w.astype(BF16)[old]
    d_p = d_skip.astype(F32)[old]
    glu_b_p = glu_b[old]

    uz = matmul(h, in_w_p, out_dtype=BF16, tm=tm, tn=_pick(2 * e, 1024), name="mm_s5_in")
    u, z = uz[:, :e], uz[:, e:]

    n_chunks = l // S5_T
    u_g = to_groups(uz.reshape(b, l, 2 * e), n_chunks=n_chunks)
    w1, ot, ar, ai = s5_operators(a_re, a_im, log_step, b_re, b_im, c_re, c_im)
    y_g = s5_scan(u_g, w1, ot, ar, ai, nb=b, n_ctx=l_ctx // S5_T, n_chunks=n_chunks)
    y_ssm = from_groups(y_g, nb=b, n_chunks=n_chunks).reshape(m, e)

    y1 = jax.nn.gelu(y_ssm.astype(F32) + d_p * u.astype(F32)).astype(BF16)
    y2 = matmul_glu(y1, glu_w_p, glu_b_p, uz, tm=tm, tn=te, name="mm_s5_glu")
    o = matmul(y2, out_w_p, out_dtype=F32, tm=tm, tn=_pick(d, 1024),
               name="mm_s5_out").reshape(b, l, d)
    x_ctx = x_ctx + g_c * o[:, :l_ctx]
    x_lat = x_lat + g_l * o[:, l_ctx:]
    return x_lat, x_ctx


def _rwkv_layer(x_lat, x_ctx, c, c_ctx, norm_g, ada_w, ada_b, mu, in_w, w0, w1, w2, a0, a1, a2,
                k_k, k_a, r_k, ln_w, ln_b, out_w):
    b, l_lat, d = x_lat.shape
    l_ctx = x_ctx.shape[1]
    l = l_ctx + l_lat
    e = in_w.shape[2]
    heads = e // RWKV_HEAD
    m = b * l
    (sh_l, sc_l, g_l), (sh_c, sc_c, _) = _modulation(c, c_ctx, ada_w, ada_b)
    h_lat = _rmsnorm(x_lat, norm_g) * (1.0 + sc_l) + sh_l
    h_ctx = _rmsnorm(x_ctx, norm_g) * (1.0 + sc_c) + sh_c
    h = jnp.concatenate([h_ctx, h_lat], axis=1).astype(BF16).reshape(m, d)
    shifted = (jnp.concatenate([_seq_shift(h_ctx), _q_shift(h_lat)], axis=1)
               .astype(BF16).reshape(m, d))

    tm = _pick(m, 1024)
    tn = _pick(e, 1024)
    rkvz = matmul_lerp(h, shifted, mu[:4], in_w, out_dtype=BF16, tm=tm, tn=tn, name="mm_rwkv_in")
    r, k, v, z = (rkvz[i].reshape(b, l, e) for i in range(4))

    n_dir, _, lora = w1.shape
    lp = (-lora) % 128
    down = jnp.stack([jnp.concatenate([jnp.pad(l1[i], ((0, 0), (0, lp))) for i in range(n_dir)],
                                      axis=1) for l1 in (w1, a1)])
    low = matmul_lerp(h, shifted, mu[4:6], down, out_dtype=F32, tm=tm, tn=down.shape[2],
                      name="mm_lora_down")
    tw, ta = (low[i].reshape(b, l, n_dir, lora + lp).transpose(2, 0, 1, 3) for i in range(2))

    pad_up = lambda l2: jnp.pad(l2, ((0, 0), (0, lp), (0, 0)))
    yg = rwkv_scan(r, k, v, z, tw, ta, pad_up(w2), pad_up(a2), w0, a0, k_k, k_a, r_k.reshape(e),
                   ln_w, ln_b, n_ctx_tokens=l_ctx)
    o = matmul(yg.reshape(b * l_lat, e), out_w, out_dtype=F32,
               tm=_pick(b * l_lat, 1024), tn=_pick(d, 1024), name="mm_rwkv_out").reshape(b, l_lat, d)
    return x_lat + g_l * o


def kernel(x, c, ctx, c_ctx, l0_norm_g, l0_ada_w, l0_ada_b, l0_in_w, l0_a_re, l0_a_im, l0_log_step, l0_b_re, l0_b_im, l0_c_re, l0_c_im, l0_d, l0_glu_w, l0_glu_b, l0_out_w, l1_norm_g, l1_ada_w, l1_ada_b, l1_mu, l1_in_w, l1_w0, l1_w1, l1_w2, l1_a0, l1_a1, l1_a2, l1_k_k, l1_k_a, l1_r_k, l1_ln_w, l1_ln_b, l1_out_w, final_norm_g):
    x_lat, x_ctx = _s5_layer(x, ctx, c, c_ctx, l0_norm_g, l0_ada_w, l0_ada_b, l0_in_w, l0_a_re,
                             l0_a_im, l0_log_step, l0_b_re, l0_b_im, l0_c_re, l0_c_im, l0_d,
                             l0_glu_w, l0_glu_b, l0_out_w)
    x_lat = _rwkv_layer(x_lat, x_ctx, c, c_ctx, l1_norm_g, l1_ada_w, l1_ada_b, l1_mu, l1_in_w,
                        l1_w0, l1_w1, l1_w2, l1_a0, l1_a1, l1_a2, l1_k_k, l1_k_a, l1_r_k,
                        l1_ln_w, l1_ln_b, l1_out_w)
    return _rmsnorm(x_lat, final_norm_g).astype(x.dtype)
```

```python
import functools

import jax
import jax.numpy as jnp
from jax import lax
from jax.experimental import pallas as pl
from jax.experimental.pallas import tpu as pltpu

F32 = jnp.float32
BF16 = jnp.bfloat16

GRID_W = 64
S5_GROUP = 16
S5_STATE = 64
S5_T = 16
RWKV_HEAD = 64
RWKV_CHUNK = 64
RMS_EPS = 1e-6
GN_EPS = 64e-5
NORM_EPS = 1e-12

VMEM_LIMIT_BYTES = 56 * 1024 * 1024


def _dot(a, b):
    return jnp.dot(a.astype(BF16), b.astype(BF16), preferred_element_type=F32)


def _dot_nt(a, b):
    return lax.dot_general(a.astype(BF16), b.astype(BF16), (((1,), (1,)), ((), ())),
                           preferred_element_type=F32)


def _dot_tn(a, b):
    return lax.dot_general(a.astype(BF16), b.astype(BF16), (((0,), (0,)), ((), ())),
                           preferred_element_type=F32)


def _split2(x):
    hi = x.astype(BF16)
    lo = (x - hi.astype(F32)).astype(BF16)
    return hi, lo


def _split3(x):
    p1 = x.astype(BF16)
    r1 = x - p1.astype(F32)
    p2 = r1.astype(BF16)
    p3 = (r1 - p2.astype(F32)).astype(BF16)
    return p1, p2, p3


def _mm_kernel(a_ref, b_ref, o_ref):
    o_ref[...] = jnp.dot(a_ref[...], b_ref[...],
                         preferred_element_type=F32).astype(o_ref.dtype)


def _mm_bias_kernel(a_ref, b_ref, bias_ref, o_ref):
    acc = jnp.dot(a_ref[...], b_ref[...], preferred_element_type=F32)
    o_ref[...] = (acc + bias_ref[...]).astype(o_ref.dtype)


def matmul(a, b, bias=None, *, out_dtype, tm, tn, name):
    m, k = a.shape
    k2, n = b.shape
    assert k == k2 and m % tm == 0 and n % tn == 0, (a.shape, b.shape, tm, tn)
    a = a.astype(BF16)
    b = b.astype(BF16)
    in_specs = [pl.BlockSpec((tm, k), lambda j, i: (i, 0)),
                pl.BlockSpec((k, tn), lambda j, i: (0, j))]
    args = [a, b]
    kern = _mm_kernel
    if bias is not None:
        in_specs.append(pl.BlockSpec((1, tn), lambda j, i: (0, j)))
        args.append(bias.astype(F32).reshape(1, n))
        kern = _mm_bias_kernel
    return pl.pallas_call(
        kern,
        grid=(n // tn, m // tm),
        in_specs=in_specs,
        out_specs=pl.BlockSpec((tm, tn), lambda j, i: (i, j)),
        out_shape=jax.ShapeDtypeStruct((m, n), out_dtype),
        name=name,
        compiler_params=pltpu.CompilerParams(
            dimension_semantics=("arbitrary", "arbitrary"),
            vmem_limit_bytes=VMEM_LIMIT_BYTES),
    )(*args)


def _mm_glu_kernel(a_ref, b_ref, bias_ref, z_ref, o_ref, *, tn):
    j = pl.program_id(0)
    acc = jnp.dot(a_ref[...], b_ref[...], preferred_element_type=F32) + bias_ref[...]
    y = a_ref[:, pl.ds(pl.multiple_of(j * tn, tn), tn)].astype(F32)
    z = z_ref[...].astype(F32)
    o_ref[...] = (y / (1.0 + jnp.exp(-acc)) * (z / (1.0 + jnp.exp(-z)))).astype(o_ref.dtype)


def matmul_glu(y, w, bias, uz, *, tm, tn, name):
    m, e = y.shape
    assert w.shape == (e, e) and uz.shape == (m, 2 * e) and m % tm == 0 and e % tn == 0
    n_tiles = e // tn
    return pl.pallas_call(
        functools.partial(_mm_glu_kernel, tn=tn),
        grid=(n_tiles, m // tm),
        in_specs=[pl.BlockSpec((tm, e), lambda j, i: (i, 0)),
                  pl.BlockSpec((e, tn), lambda j, i: (0, j)),
                  pl.BlockSpec((1, tn), lambda j, i: (0, j)),
                  pl.BlockSpec((tm, tn), lambda j, i: (i, j + n_tiles))],
        out_specs=pl.BlockSpec((tm, tn), lambda j, i: (i, j)),
        out_shape=jax.ShapeDtypeStruct((m, e), BF16),
        name=name,
        compiler_params=pltpu.CompilerParams(
            dimension_semantics=("arbitrary", "arbitrary"),
            vmem_limit_bytes=VMEM_LIMIT_BYTES),
    )(y.astype(BF16), w.astype(BF16), bias.astype(F32).reshape(1, e), uz)


def _mm_lerp_kernel(h_ref, s_ref, mu_ref, w_ref, o_ref, lhs_ref):
    @pl.when(pl.program_id(2) == 0)
    def _():
        h = h_ref[...].astype(F32)
        lhs_ref[...] = (h + (s_ref[...].astype(F32) - h) * mu_ref[0]).astype(lhs_ref.dtype)

    o_ref[0] = jnp.dot(lhs_ref[...], w_ref[0], preferred_element_type=F32).astype(o_ref.dtype)


def matmul_lerp(h, shifted, mu, w, *, out_dtype, tm, tn, name):
    m, d = h.shape
    p, d2, n = w.shape
    assert d == d2 and mu.shape == (p, d) and m % tm == 0 and n % tn == 0
    row = pl.BlockSpec((tm, d), lambda i, q, j: (i, 0))
    return pl.pallas_call(
        _mm_lerp_kernel,
        grid=(m // tm, p, n // tn),
        in_specs=[row, row,
                  pl.BlockSpec((1, 1, d), lambda i, q, j: (q, 0, 0)),
                  pl.BlockSpec((1, d, tn), lambda i, q, j: (q, 0, j))],
        out_specs=pl.BlockSpec((1, tm, tn), lambda i, q, j: (q, i, j)),
        out_shape=jax.ShapeDtypeStruct((p, m, n), out_dtype),
        scratch_shapes=[pltpu.VMEM((tm, d), BF16)],
        name=name,
        compiler_params=pltpu.CompilerParams(
            dimension_semantics=("arbitrary", "arbitrary", "arbitrary"),
            vmem_limit_bytes=VMEM_LIMIT_BYTES),
    )(h, shifted, mu.astype(F32).reshape(p, 1, d), w.astype(BF16))


def _pick(total, pref):
    t = min(pref, total)
    while total % t:
        t //= 2
    return t


def _s5_prep_kernel(are_ref, aim_ref, ls_ref, bre_ref, bim_ref, cre_ref, cim_ref,
                    w1_ref, ot_ref, ar_ref, ai_ref, *, gb):
    T = S5_T
    GC = S5_GROUP
    lane = lax.broadcasted_iota(jnp.int32, (1, 128), 1)
    fwd = lane < S5_STATE
    s_col = lax.broadcasted_iota(jnp.int32, (T, 1), 0).astype(F32)
    ri = lax.broadcasted_iota(jnp.int32, (T * GC, T * GC), 0) % T
    ci = lax.broadcasted_iota(jnp.int32, (T * GC, T * GC), 1) % T
    lane2 = lax.broadcasted_iota(jnp.int32, (1, 256), 1) % 128
    fwd2 = lane2 < S5_STATE
    half = float(T // 2)

    for g in range(gb):
        a_re = are_ref[g]
        a_im = aim_ref[g]
        dt = jnp.exp(ls_ref[g])
        x_re = dt * a_re
        x_im = dt * a_im

        def lam_pow(p):
            mag = jnp.exp(p * x_re)
            ang = p * x_im
            return mag * jnp.cos(ang), mag * jnp.sin(ang)

        l_re, l_im = lam_pow(jnp.ones((1, 1), F32))
        den = a_re * a_re + a_im * a_im
        nr, ni = l_re - 1.0, l_im
        f_re = (nr * a_re + ni * a_im) / den
        f_im = (ni * a_re - nr * a_im) / den
        b_re = bre_ref[g]
        b_im = bim_ref[g]
        bb_re = f_re * b_re - f_im * b_im
        bb_im = f_re * b_im + f_im * b_re
        c_re = cre_ref[g]
        c_im = cim_ref[g]

        def times_pow(v_re, v_im, p):
            p_re, p_im = lam_pow(p)
            o_re = v_re[:, None, :] * p_re[None, :, :] - v_im[:, None, :] * p_im[None, :, :]
            o_im = v_re[:, None, :] * p_im[None, :, :] + v_im[:, None, :] * p_re[None, :, :]
            return o_re.reshape(GC * T, 128), o_im.reshape(GC * T, 128)

        pb = jnp.where(fwd, half - s_col, s_col - half)
        pc = -pb
        bl_re, bl_im = times_pow(bb_re, bb_im, pb)
        cl_re, cl_im = times_pow(c_re, c_im, pc)
        lhs = jnp.concatenate([bl_re, bl_im], axis=1)
        rhs = jnp.concatenate([cl_re, -cl_im], axis=1)
        r_hi, r_lo = _split2(rhs)

        def dot3(l):
            l_hi, l_lo = _split2(l)
            dn = (((1,), (1,)), ((), ()))
            return (lax.dot_general(l_hi, r_hi, dn, preferred_element_type=F32)
                    + lax.dot_general(l_hi, r_lo, dn, preferred_element_type=F32)
                    + lax.dot_general(l_lo, r_hi, dn, preferred_element_type=F32))

        t_f = dot3(jnp.where(fwd2, lhs, 0.0))
        t_b = dot3(jnp.where(fwd2, 0.0, lhs))
        toep = jnp.where(ci >= ri, t_f, 0.0) + jnp.where(ri >= ci, t_b, 0.0)

        ps = jnp.where(fwd, (T - 1.0) - s_col, s_col)
        s_re, s_im = times_pow(bb_re, bb_im, ps)
        w1_ref[g] = jnp.concatenate([toep, s_re, s_im], axis=1).astype(w1_ref.dtype)

        po = jnp.where(fwd, s_col + 1.0, T - s_col)
        o_re, o_im = times_pow(c_re, c_im, po)
        ot_ref[g] = jnp.concatenate([o_re, -o_im], axis=1).astype(ot_ref.dtype)

        pw_re, pw_im = lam_pow(jnp.full((1, 1), float(T), F32))
        ar_ref[g] = pw_re
        ai_ref[g] = pw_im


def _s5_kernel(u_ref, w1_ref, ot_ref, ar_ref, ai_ref, y_ref,
               p_ref, hf_re, hf_im, hb_re, hb_im, *, nb, n_ctx, n_chunks):
    gs = u_ref.shape[0]
    for g in range(gs):
        p_ref[g] = jnp.dot(u_ref[g], w1_ref[g], preferred_element_type=F32)
    ar = jnp.broadcast_to(ar_ref[...], (gs, nb, 128))
    ai = jnp.broadcast_to(ai_ref[...], (gs, nb, 128))
    fwd = lax.broadcasted_iota(jnp.int32, (gs, nb, 128), 2) < S5_STATE

    def step(i, carry):
        h_re, h_im = carry
        cb = jnp.where(i < n_ctx, n_ctx - 1 - i, n_chunks - 1 - (i - n_ctx))
        rf = pl.ds(pl.multiple_of(i * nb, nb), nb)
        rb = pl.ds(pl.multiple_of(cb * nb, nb), nb)
        hf_re[:, rf, :] = h_re
        hf_im[:, rf, :] = h_im
        hb_re[:, rb, :] = h_re
        hb_im[:, rb, :] = h_im
        x_re = jnp.where(fwd, p_ref[:, rf, 256:384], p_ref[:, rb, 256:384])
        x_im = jnp.where(fwd, p_ref[:, rf, 384:512], p_ref[:, rb, 384:512])
        n_re = ar * h_re - ai * h_im + x_re
        n_im = ar * h_im + ai * h_re + x_im
        return n_re, n_im

    zero = jnp.zeros((gs, nb, 128), F32)
    lax.fori_loop(0, n_chunks, step, (zero, zero))

    rows = p_ref.shape[1]
    fwd_all = lax.broadcasted_iota(jnp.int32, (rows, 128), 1) < S5_STATE
    for g in range(gs):
        h_all = jnp.concatenate([jnp.where(fwd_all, hf_re[g], hb_re[g]),
                                 jnp.where(fwd_all, hf_im[g], hb_im[g])], axis=1)
        y = p_ref[g, :, 0:256] + _dot_nt(h_all, ot_ref[g])
        y_ref[g] = y.astype(y_ref.dtype)


def s5_operators(a_re, a_im, log_step, b_re, b_im, c_re, c_im):
    n_dir, g, n = a_re.shape
    gb = 8

    def lanes(x):
        return jnp.concatenate([x[0], x[1]], axis=-1).astype(F32)

    are = lanes(a_re).reshape(g, 1, 2 * n)
    aim = lanes(a_im).reshape(g, 1, 2 * n)
    ls = lanes(jnp.broadcast_to(log_step[:, :, None], (n_dir, g, n))).reshape(g, 1, 2 * n)
    bre = lanes(jnp.swapaxes(b_re, -1, -2))
    bim = lanes(jnp.swapaxes(b_im, -1, -2))
    cre = lanes(c_re)
    cim = lanes(c_im)
    row = pl.BlockSpec((gb, 1, 2 * n), lambda i: (i, 0, 0))
    mat = pl.BlockSpec((gb, S5_GROUP, 2 * n), lambda i: (i, 0, 0))
    tg = S5_T * S5_GROUP
    return pl.pallas_call(
        functools.partial(_s5_prep_kernel, gb=gb),
        grid=(g // gb,),
        in_specs=[row, row, row, mat, mat, mat, mat],
        out_specs=[pl.BlockSpec((gb, tg, tg + 4 * n), lambda i: (i, 0, 0)),
                   pl.BlockSpec((gb, tg, 4 * n), lambda i: (i, 0, 0)),
                   row, row],
        out_shape=[jax.ShapeDtypeStruct((g, tg, tg + 4 * n), BF16),
                   jax.ShapeDtypeStruct((g, tg, 4 * n), BF16),
                   jax.ShapeDtypeStruct((g, 1, 2 * n), F32),
                   jax.ShapeDtypeStruct((g, 1, 2 * n), F32)],
        name="s5_operators",
        compiler_params=pltpu.CompilerParams(dimension_semantics=("arbitrary",),
                                             vmem_limit_bytes=VMEM_LIMIT_BYTES),
    )(are, aim, ls, bre, bim, cre, cim)


def s5_scan(u_g, w1, ot, ar, ai, *, nb, n_ctx, n_chunks):
    g, rows, tg = u_g.shape
    assert rows == n_chunks * nb and tg == S5_T * S5_GROUP
    gs = _pick(g, 4)
    blk = lambda w: pl.BlockSpec((gs,) + w, lambda i: (i, 0, 0))
    return pl.pallas_call(
        functools.partial(_s5_kernel, nb=nb, n_ctx=n_ctx, n_chunks=n_chunks),
        grid=(g // gs,),
        in_specs=[blk((rows, tg)), blk(w1.shape[1:]), blk(ot.shape[1:]),
                  blk((1, 128)), blk((1, 128))],
        out_specs=blk((rows, tg)),
        out_shape=jax.ShapeDtypeStruct((g, rows, tg), BF16),
        scratch_shapes=[pltpu.VMEM((gs, rows, w1.shape[2]), F32)]
                       + [pltpu.VMEM((gs, rows, 128), F32)] * 4,
        name="s5_scan",
        compiler_params=pltpu.CompilerParams(dimension_semantics=("arbitrary",),
                                             vmem_limit_bytes=VMEM_LIMIT_BYTES),
    )(u_g, w1, ot, ar, ai)


def _eye(n):
    r = lax.broadcasted_iota(jnp.int32, (n, n), 0)
    c = lax.broadcasted_iota(jnp.int32, (n, n), 1)
    return jnp.where(r == c, 1.0, 0.0).astype(BF16)


def _to_groups_kernel(u_ref, o_ref, *, nb, n_j):
    g = o_ref.shape[0]
    eye = _eye(g)
    for b in range(nb):
        x = u_ref[b]
        rows = jnp.concatenate([x[:, j * g:(j + 1) * g] for j in range(n_j)], axis=0)
        o_ref[:, b, :] = _dot_nt(eye, rows).astype(o_ref.dtype)


def _from_groups_kernel(y_ref, o_ref, *, nb, n_j):
    g = y_ref.shape[0]
    t = o_ref.shape[1]
    eye = _eye(y_ref.shape[2])
    for b in range(nb):
        cols = _dot_nt(eye, y_ref[:, b, :]).astype(o_ref.dtype)
        for i in range(n_j):
            o_ref[b, :, i * g:(i + 1) * g] = cols[i * t:(i + 1) * t, :]


def to_groups(uz, *, n_chunks):
    b, l, e2 = uz.shape
    e = e2 // 2
    g = e // S5_GROUP
    return pl.pallas_call(
        functools.partial(_to_groups_kernel, nb=b, n_j=S5_GROUP),
        grid=(n_chunks,),
        in_specs=[pl.BlockSpec((b, S5_T, e), lambda c: (0, c, 0))],
        out_specs=pl.BlockSpec((g, b, S5_T * S5_GROUP), lambda c: (0, c, 0)),
        out_shape=jax.ShapeDtypeStruct((g, n_chunks * b, S5_T * S5_GROUP), BF16),
        name="s5_to_groups",
        compiler_params=pltpu.CompilerParams(dimension_semantics=("arbitrary",),
                                             vmem_limit_bytes=VMEM_LIMIT_BYTES),
    )(uz)


def from_groups(y_g, *, nb, n_chunks):
    g, rows, tg = y_g.shape
    e = g * S5_GROUP
    return pl.pallas_call(
        functools.partial(_from_groups_kernel, nb=nb, n_j=S5_GROUP),
        grid=(n_chunks,),
        in_specs=[pl.BlockSpec((g, nb, tg), lambda c: (0, c, 0))],
        out_specs=pl.BlockSpec((nb, S5_T, e), lambda c: (0, c, 0)),
        out_shape=jax.ShapeDtypeStruct((nb, n_chunks * S5_T, e), BF16),
        name="s5_from_groups",
        compiler_params=pltpu.CompilerParams(dimension_semantics=("arbitrary",),
                                             vmem_limit_bytes=VMEM_LIMIT_BYTES),
    )(y_g)


def _rwkv_chunk_index(item, n_ctx, n_chunks):
    i = item % n_chunks
    back = jnp.where(i < n_ctx, n_ctx - 1 - i, n_chunks - 1 - (i - n_ctx))
    return jnp.where(item < n_chunks, i, back)


_HB_AC, _HB_RC, _HB_BH, _HB_KH, _HB_V, _HB_RB, _HB_RK, _HB_DM = range(8)
_HF_AKV, _HF_AO, _HF_Q, _HF_ZS = range(4)


def _rwkv_step(r_ref, k_ref, v_ref, z_ref, tw_ref, ta_ref, w2_ref, a2_ref, w0_ref, a0_ref,
               kk_ref, ka_ref, rk_ref, lnw_ref, lnb_ref, o_ref, s_ref, acc_ref,
               hbw, hfw, hww, hbr, hfr, hwr, *, n_ctx, n_chunks, n_pairs):
    C = RWKV_CHUNK
    n_items = 2 * n_chunks
    step = pl.program_id(2)
    pairs = range(n_pairs)

    def each(f, *lists):
        return [f(*args) for args in zip(*lists)]

    def cols(x):
        return [x[:, p * 128:(p + 1) * 128] for p in pairs]

    def join(xs):
        return jnp.concatenate(xs, axis=1)

    qa = jnp.minimum(step, n_items - 1)
    a_fwd = qa < n_chunks
    a_cidx = _rwkv_chunk_index(qa, n_ctx, n_chunks)
    d = jnp.where(a_fwd, 0, 1)
    sgn = jnp.where(a_fwd, 1, -1)
    qb = jnp.maximum(step - 1, 0)
    b_fwd = qb < n_chunks
    b_cidx = _rwkv_chunk_index(qb, n_ctx, n_chunks)
    b_live = step >= 1

    t_i = lax.broadcasted_iota(jnp.int32, (C, 128), 0)
    lane = lax.broadcasted_iota(jnp.int32, (C, 128), 1)
    s_i = lane % C
    lane_head = lane // RWKV_HEAD
    before = (s_i - t_i) * sgn
    strict = before < 0
    incl = before <= 0
    same_blk = (t_i // 16) == (s_i // 16)
    diag_m = strict & same_blk
    off_m = strict & jnp.logical_not(same_blk)
    tr = lax.broadcasted_iota(jnp.int32, (C, C), 0)
    tc = lax.broadcasted_iota(jnp.int32, (C, C), 1)
    cum_m = jnp.where((tc - tr) * sgn <= 0, 1.0, 0.0).astype(BF16)
    ri = lax.broadcasted_iota(jnp.int32, (128, 128), 0)
    ci = lax.broadcasted_iota(jnp.int32, (128, 128), 1)
    head_ones = jnp.where((ri // RWKV_HEAD) == (ci // RWKV_HEAD), 1.0, 0.0).astype(BF16)

    @pl.when(qb % n_chunks == 0)
    def _():
        s_ref[...] = jnp.zeros(s_ref.shape, F32)

    def stack(x):
        x = x.astype(BF16)
        zero = jnp.zeros_like(x)
        return jnp.concatenate([jnp.where(lane_head == 0, x, zero),
                                jnp.where(lane_head == 1, x, zero)], axis=0)

    def pack(x):
        return jnp.where(lane_head == 0, x[:C], x[C:])

    def head_sum(x):
        tot = _dot(jnp.concatenate(cols(x), axis=0), head_ones)
        return join([tot[p * C:(p + 1) * C] for p in pairs])

    mm = lambda l, r_: jnp.dot(l.astype(BF16), r_, preferred_element_type=F32)
    wide = lambda l, *rs: mm(l, jnp.concatenate(rs, axis=1))

    r = r_ref[0].astype(F32)
    k = k_ref[0].astype(F32)
    v = v_ref[0].astype(F32)
    wp = w0_ref[pl.ds(d, 1), :] + _dot(jnp.tanh(tw_ref[0, 0]), w2_ref[d])
    ap = a0_ref[pl.ds(d, 1), :] + _dot(ta_ref[d, 0], a2_ref[d])
    sp = jnp.maximum(-wp, 0.0) + jnp.log(1.0 + jnp.exp(-jnp.abs(wp)))
    lw = -jnp.exp(-sp - 0.5)
    a_rate = 1.0 / (1.0 + jnp.exp(-ap))
    k_a = ka_ref[...]
    kkr = k * kk_ref[...]
    kd = k * (1.0 + (a_rate - 1.0) * k_a)
    lc = sum(jnp.dot(cum_m, p, preferred_element_type=F32) for p in _split3(lw))
    l_mid = lc[C // 2:C // 2 + 1, :]
    l_end = jnp.where(a_fwd, lc[C - 1:C, :], lc[0:1, :])
    dl = lc - l_mid
    e_in = jnp.exp(dl)
    e_ex = jnp.exp(dl - lw)
    e_iv = jnp.exp(-dl)
    c_row = jnp.exp(l_mid)
    e2 = jnp.exp(l_end - l_mid)
    kk_all = kkr / jnp.maximum(jnp.sqrt(head_sum(kkr * kkr)), NORM_EPS)
    at_all = -kk_all * e_ex
    bt_all = kk_all * (a_rate * e_iv)
    rt_all = r * e_in
    kt_all = kd * e_iv
    hww[...] = jnp.exp(l_end)
    hbw[_HB_AC] = (at_all * c_row).astype(BF16)
    hbw[_HB_RC] = (rt_all * c_row).astype(BF16)
    hbw[_HB_BH] = (bt_all * e2).astype(BF16)
    hbw[_HB_KH] = (kt_all * e2).astype(BF16)
    hbw[_HB_V] = v_ref[0]
    at, bt, rp, kp = cols(at_all), cols(bt_all), cols(rt_all), cols(kt_all)
    vs = each(stack, cols(v))

    s_old = [s_ref[p] for p in pairs]
    asr = each(lambda a, r_, s: _dot_nt(jnp.concatenate([a, r_], axis=0), stack(s)),
               cols(hbr[_HB_AC]), cols(hbr[_HB_RC]), s_old)
    a4 = each(lambda a, r_, b, k_: _dot_nt(jnp.concatenate([a, r_], axis=0),
                                           jnp.concatenate([stack(b), stack(k_)], axis=0)),
              at, rp, bt, kp)
    ak = each(lambda m: jnp.where(strict, m[:C, 128:], 0.0), a4)
    hbw[_HB_RB] = join(each(lambda m: jnp.where(incl, m[C:, :128], 0.0), a4)).astype(BF16)
    hbw[_HB_RK] = join(each(lambda m: jnp.where(incl, m[C:, 128:], 0.0), a4)).astype(BF16)
    ad = each(lambda m: jnp.where(diag_m, m[:C, :128], 0.0), a4)
    ao = each(lambda m: jnp.where(off_m, m[:C, :128], 0.0), a4)
    hfw[_HF_AO] = join(ao)

    x0 = each(lambda m, q: m[:C] + q, asr, cols(hfr[_HF_AKV]))
    ao_b = cols(hfr[_HF_AO])
    dx = each(lambda d_, o, x: wide(d_, stack(o), stack(x)),
              cols(hbr[_HB_DM]), ao_b, x0)
    ad_s = each(stack, ad)
    hfw[_HF_AKV] = join(each(mm, ak, vs))
    p2 = each(mm, ad, ad_s)

    n1 = each(lambda o, m: o + m[:, :128], ao_b, dx)
    xa = each(lambda x, m: x + m[:, 128:], x0, dx)
    xc = each(lambda n_, x: x + mm(n_, stack(x)), n1, xa)
    p43 = each(lambda p, a_s: wide(p, stack(p), a_s), p2, ad_s)
    p4 = [m[:, :128] for m in p43]
    s3 = each(lambda a, p, m: a + p + m[:, 128:], ad, p2, p43)

    w1_ = each(lambda n_, x: mm(n_, stack(x)), n1, xc)
    p8s = each(lambda p, s: wide(p, stack(p), stack(s)), p4, s3)
    p8 = [m[:, :128] for m in p8s]
    s7 = each(lambda s, p, m: s + p + m[:, 128:], s3, p4, p8s)

    u = each(lambda n_, x, w_: x + mm(n_, stack(w_)), n1, xc, w1_)
    hbw[_HB_DM] = join(each(lambda s, p: s + p + mm(p, stack(s)), s7, p8)).astype(BF16)

    v_b = cols(hbr[_HB_V])
    ys = each(lambda m, rb_, rk_, u_, v_: m[C:] + mm(
        jnp.concatenate([rb_, rk_], axis=1), jnp.concatenate([stack(u_), stack(v_)], axis=0)),
        asr, cols(hbr[_HB_RB]), cols(hbr[_HB_RK]), u, v_b)
    upd = each(lambda u_, v_, b, k_: pack(_dot_tn(
        jnp.concatenate([u_, v_], axis=0), jnp.concatenate([b, k_], axis=0))),
        u, v_b, cols(hbr[_HB_BH]), cols(hbr[_HB_KH]))
    wc_b = hwr[...]
    for p in pairs:
        s_ref[p] = s_old[p] * wc_b[:, p * 128:(p + 1) * 128] + upd[p]

    y = join(ys)
    a_lat = a_cidx >= n_ctx
    b_lat = jnp.logical_and(b_cidx >= n_ctx, b_live)
    orow = pl.multiple_of(jnp.maximum(b_cidx - n_ctx, 0) * C, C)

    @pl.when(jnp.logical_and(a_lat, jnp.logical_not(a_fwd)))
    def _():
        ar_f = 1.0 / (1.0 + jnp.exp(-(a0_ref[0:1, :] + _dot(ta_ref[0, 0], a2_ref[0]))))
        k_sum = k * (2.0 + (ar_f + a_rate - 2.0) * k_a)
        hfw[_HF_Q] = r * k_sum * rk_ref[...]
        zg = z_ref[0].astype(F32)
        hfw[_HF_ZS] = zg / (1.0 + jnp.exp(-zg))

    @pl.when(jnp.logical_and(b_lat, b_fwd))
    def _():
        acc_ref[pl.ds(orow, C), :] = y

    @pl.when(jnp.logical_and(b_lat, jnp.logical_not(b_fwd)))
    def _():
        y_sum = acc_ref[pl.ds(orow, C), :] + y
        inv = 1.0 / RWKV_HEAD
        dev = y_sum - head_sum(y_sum) * inv
        yn = dev * lax.rsqrt(head_sum(dev * dev) * inv + GN_EPS)
        yn = yn * lnw_ref[...] + lnb_ref[...]
        bonus = head_sum(hfr[_HF_Q]) * hbr[_HB_V].astype(F32)
        o_ref[0] = ((yn + bonus) * hfr[_HF_ZS]).astype(o_ref.dtype)


def _rwkv_kernel(*refs, n_ctx, n_chunks, n_pairs):
    *io, s_ref, acc_ref, hb0, hb1, hf0, hf1, hw0, hw1 = refs
    step = pl.program_id(2)
    run = functools.partial(_rwkv_step, *io, s_ref, acc_ref,
                            n_ctx=n_ctx, n_chunks=n_chunks, n_pairs=n_pairs)

    @pl.when(step == 0)
    def _():
        hb1[...] = jnp.zeros(hb1.shape, hb1.dtype)
        hf1[...] = jnp.zeros(hf1.shape, hf1.dtype)
        hw1[...] = jnp.zeros(hw1.shape, hw1.dtype)

    @pl.when(step % 2 == 0)
    def _():
        run(hb0, hf0, hw0, hb1, hf1, hw1)

    @pl.when(step % 2 == 1)
    def _():
        run(hb1, hf1, hw1, hb0, hf0, hw0)


def rwkv_scan(r, k, v, z, tw, ta, w2, a2, w0, a0, k_k, k_a, r_k, ln_w, ln_b, *, n_ctx_tokens):
    b, l, e = r.shape
    rank = tw.shape[-1]
    C = RWKV_CHUNK
    pw = min(e, 2048)
    n_pairs = pw // 128
    n_chunks = l // C
    n_ctx = n_ctx_tokens // C
    n_items = 2 * n_chunks
    l_out = l - n_ctx_tokens
    cix = functools.partial(_rwkv_chunk_index, n_ctx=n_ctx, n_chunks=n_chunks)
    item_a = lambda t: jnp.minimum(t, n_items - 1)
    item_b = lambda t: jnp.maximum(t - 1, 0)

    def out_chunk(t):
        q = item_b(t)
        c = cix(q)
        return jnp.where(jnp.logical_and(q >= n_chunks, c >= n_ctx), c, n_chunks - 1) - n_ctx

    seq = pl.BlockSpec((1, C, pw), lambda i, j, t: (i, cix(item_a(t)), j))
    low_d = pl.BlockSpec((1, 1, C, rank),
                         lambda i, j, t: (item_a(t) // n_chunks, i, cix(item_a(t)), 0))
    low_2 = pl.BlockSpec((2, 1, C, rank), lambda i, j, t: (0, i, cix(item_a(t)), 0))
    up = pl.BlockSpec((2, rank, pw), lambda i, j, t: (0, 0, j))
    vec2 = pl.BlockSpec((2, pw), lambda i, j, t: (0, j))
    vec = pl.BlockSpec((1, pw), lambda i, j, t: (0, j))
    row = lambda x: x.astype(F32).reshape(1, e)
    return pl.pallas_call(
        functools.partial(_rwkv_kernel, n_ctx=n_ctx, n_chunks=n_chunks, n_pairs=n_pairs),
        grid=(b, e // pw, n_items + 1),
        in_specs=[seq, seq, seq, seq, low_d, low_2, up, up, vec2, vec2, vec, vec, vec, vec, vec],
        out_specs=pl.BlockSpec((1, C, pw), lambda i, j, t: (i, out_chunk(t), j)),
        out_shape=jax.ShapeDtypeStruct((b, l_out, e), BF16),
        scratch_shapes=[pltpu.VMEM((n_pairs, C, 128), F32),
                        pltpu.VMEM((l_out, pw), F32),
                        pltpu.VMEM((8, C, pw), BF16), pltpu.VMEM((8, C, pw), BF16),
                        pltpu.VMEM((4, C, pw), F32), pltpu.VMEM((4, C, pw), F32),
                        pltpu.VMEM((1, pw), F32), pltpu.VMEM((1, pw), F32)],
        name="rwkv_scan",
        compiler_params=pltpu.CompilerParams(
            dimension_semantics=("arbitrary", "arbitrary", "arbitrary"),
            vmem_limit_bytes=VMEM_LIMIT_BYTES),
    )(r, k, v, z, tw, ta, w2.astype(BF16), a2.astype(BF16), w0.astype(F32), a0.astype(F32),
      row(k_k), row(k_a), row(r_k), row(ln_w), row(ln_b))


def _rmsnorm(x, g):
    xf = x.astype(F32)
    y = xf * lax.rsqrt(jnp.mean(xf * xf, axis=-1, keepdims=True) + RMS_EPS)
    return y * g.astype(F32)


def _modulation(c, c_ctx, ada_w, ada_b):
    d = c.shape[-1]
    cond = jnp.concatenate([c, c_ctx[None]], axis=0)
    rows = cond.shape[0]
    pad = (-rows) % 16
    act = jnp.pad(jax.nn.silu(cond), ((0, pad), (0, 0)))
    m = matmul(act, ada_w, ada_b, out_dtype=F32, tm=rows + pad, tn=_pick(3 * d, 1024),
               name="mm_modulation")
    m = m[:rows]
    lat = [t[:, None, :] for t in jnp.split(m[:rows - 1], 3, axis=-1)]
    ctx = [t[None] for t in jnp.split(m[rows - 1:], 3, axis=-1)]
    return lat, ctx


def _q_shift(h):
    b, l, d = h.shape
    rows = l // GRID_W
    g = h.reshape(b, rows, GRID_W, d)
    q = d // 4
    left = jnp.pad(g[:, :, :-1, :q], ((0, 0), (0, 0), (1, 0), (0, 0)))
    right = jnp.pad(g[:, :, 1:, q:2 * q], ((0, 0), (0, 0), (0, 1), (0, 0)))
    up = jnp.pad(g[:, :-1, :, 2 * q:3 * q], ((0, 0), (1, 0), (0, 0), (0, 0)))
    down = jnp.pad(g[:, 1:, :, 3 * q:], ((0, 0), (0, 1), (0, 0), (0, 0)))
    return jnp.concatenate([left, right, up, down], axis=-1).reshape(b, l, d)


def _seq_shift(h):
    half = h.shape[-1] // 2
    prev = jnp.pad(h[:, :-1, :half], ((0, 0), (1, 0), (0, 0)))
    nxt = jnp.pad(h[:, 1:, half:], ((0, 0), (0, 1), (0, 0)))
    return jnp.concatenate([prev, nxt], axis=-1)


def _s5_layer(x_lat, x_ctx, c, c_ctx, norm_g, ada_w, ada_b, in_w, a_re, a_im, log_step,
              b_re, b_im, c_re, c_im, d_skip, glu_w, glu_b, out_w):
    b, l_lat, d = x_lat.shape
    l_ctx = x_ctx.shape[1]
    l = l_ctx + l_lat
    e = in_w.shape[1] // 2
    g = e // S5_GROUP
    (sh_l, sc_l, g_l), (sh_c, sc_c, g_c) = _modulation(c, c_ctx, ada_w, ada_b)
    h_lat = _rmsnorm(x_lat, norm_g) * (1.0 + sc_l) + sh_l
    h_ctx = _rmsnorm(x_ctx, norm_g) * (1.0 + sc_c) + sh_c
    h = jnp.concatenate([h_ctx, h_lat], axis=1).astype(BF16).reshape(b * l, d)
    m = b * l
    tm = _pick(m, 1024)
    te = _pick(e, 1024)

    lane = jnp.arange(e)
    old = (lane % g) * S5_GROUP + lane // g
    perm = (lane[:, None] == old[None, :]).astype(BF16)
    in_w_p = matmul(in_w.reshape(2 * d, e), perm, out_dtype=BF16, tm=_pick(2 * d, 1024), tn=te,
                    name="mm_perm_in").reshape(d, 2 * e)
    glu_w_p = matmul(glu_w.astype(BF16)[old], perm, out_dtype=BF16, tm=te, tn=te,
                     name="mm_perm_glu")
    out_w_p = out_w.astype(BF16)[old]
    d_p = d_skip.astype(F32)[old]
    glu_b_p = glu_b[old]

    uz = matmul(h, in_w_p, out_dtype=BF16, tm=tm, tn=_pick(2 * e, 1024), name="mm_s5_in")
    u = uz[:, :e]

    n_chunks = l // S5_T
    u_g = to_groups(uz.reshape(b, l, 2 * e), n_chunks=n_chunks)
    w1, ot, ar, ai = s5_operators(a_re, a_im, log_step, b_re, b_im, c_re, c_im)
    y_g = s5_scan(u_g, w1, ot, ar, ai, nb=b, n_ctx=l_ctx // S5_T, n_chunks=n_chunks)
    y_ssm = from_groups(y_g, nb=b, n_chunks=n_chunks).reshape(m, e)

    y1 = jax.nn.gelu(y_ssm.astype(F32) + d_p * u.astype(F32)).astype(BF16)
    y2 = matmul_glu(y1, glu_w_p, glu_b_p, uz, tm=tm, tn=te, name="mm_s5_glu")
    o = matmul(y2, out_w_p, out_dtype=F32, tm=tm, tn=_pick(d, 1024),
               name="mm_s5_out").reshape(b, l, d)
    x_ctx = x_ctx + g_c * o[:, :l_ctx]
    x_lat = x_lat + g_l * o[:, l_ctx:]
    return x_lat, x_ctx


def _rwkv_layer(x_lat, x_ctx, c, c_ctx, norm_g, ada_w, ada_b, mu, in_w, w0, w1, w2, a0, a1, a2,
                k_k, k_a, r_k, ln_w, ln_b, out_w):
    b, l_lat, d = x_lat.shape
    l_ctx = x_ctx.shape[1]
    l = l_ctx + l_lat
    e = in_w.shape[2]
    m = b * l
    (sh_l, sc_l, g_l), (sh_c, sc_c, _) = _modulation(c, c_ctx, ada_w, ada_b)
    h_lat = _rmsnorm(x_lat, norm_g) * (1.0 + sc_l) + sh_l
    h_ctx = _rmsnorm(x_ctx, norm_g) * (1.0 + sc_c) + sh_c
    h = jnp.concatenate([h_ctx, h_lat], axis=1).astype(BF16).reshape(m, d)
    shifted = (jnp.concatenate([_seq_shift(h_ctx), _q_shift(h_lat)], axis=1)
               .astype(BF16).reshape(m, d))

    tm = _pick(m, 1024)
    tn = _pick(e, 1024)
    rkvz = matmul_lerp(h, shifted, mu[:4], in_w, out_dtype=BF16, tm=tm, tn=tn, name="mm_rwkv_in")
    r, k, v, z = (rkvz[i].reshape(b, l, e) for i in range(4))

    n_dir, _, lora = w1.shape
    lp = (-lora) % 128
    down = jnp.stack([jnp.concatenate([jnp.pad(l1[i], ((0, 0), (0, lp))) for i in range(n_dir)],
                                      axis=1) for l1 in (w1, a1)])
    low = matmul_lerp(h, shifted, mu[4:6], down, out_dtype=F32, tm=tm, tn=down.shape[2],
                      name="mm_lora_down")
    tw, ta = (low[i].reshape(b, l, n_dir, lora + lp).transpose(2, 0, 1, 3) for i in range(2))

    pad_up = lambda l2: jnp.pad(l2, ((0, 0), (0, lp), (0, 0)))
    yg = rwkv_scan(r, k, v, z, tw, ta, pad_up(w2), pad_up(a2), w0, a0, k_k, k_a, r_k.reshape(e),
                   ln_w, ln_b, n_ctx_tokens=l_ctx)
    o = matmul(yg.reshape(b * l_lat, e), out_w, out_dtype=F32,
               tm=_pick(b * l_lat, 1024), tn=_pick(d, 1024), name="mm_rwkv_out").reshape(b, l_lat, d)
    return x_lat + g_l * o


def kernel(x, c, ctx, c_ctx, l0_norm_g, l0_ada_w, l0_ada_b, l0_in_w, l0_a_re, l0_a_im, l0_log_step, l0_b_re, l0_b_im, l0_c_re, l0_c_im, l0_d, l0_glu_w, l0_glu_b, l0_out_w, l1_norm_g, l1_ada_w, l1_ada_b, l1_mu, l1_in_w, l1_w0, l1_w1, l1_w2, l1_a0, l1_a1, l1_a2, l1_k_k, l1_k_a, l1_r_k, l1_ln_w, l1_ln_b, l1_out_w, final_norm_g):
    x_lat, x_ctx = _s5_layer(x, ctx, c, c_ctx, l0_norm_g, l0_ada_w, l0_ada_b, l0_in_w, l0_a_re,
                             l0_a_im, l0_log_step, l0_b_re, l0_b_im, l0_c_re, l0_c_im, l0_d,
                             l0_glu_w, l0_glu_b, l0_out_w)
    x_lat = _rwkv_layer(x_lat, x_ctx, c, c_ctx, l1_norm_g, l1_ada_w, l1_ada_b, l1_mu, l1_in_w,
                        l1_w0, l1_w1, l1_w2, l1_a0, l1_a1, l1_a2, l1_k_k, l1_k_a, l1_r_k,
                        l1_ln_w, l1_ln_b, l1_out_w)
    return _rmsnorm(x_lat, final_norm_g).astype(x.dtype)
```

```python
import functools

import jax
import jax.numpy as jnp
from jax import lax
from jax.experimental import pallas as pl
from jax.experimental.pallas import tpu as pltpu

F32 = jnp.float32
BF16 = jnp.bfloat16

GRID_W = 64
S5_GROUP = 16
S5_STATE = 64
S5_T = 16
RWKV_HEAD = 64
RWKV_CHUNK = 64
RMS_EPS = 1e-6
GN_EPS = 64e-5
NORM_EPS = 1e-12

VMEM_LIMIT_BYTES = 56 * 1024 * 1024


def _dot(a, b):
    return jnp.dot(a.astype(BF16), b.astype(BF16), preferred_element_type=F32)


def _dot_nt(a, b):
    return lax.dot_general(a.astype(BF16), b.astype(BF16), (((1,), (1,)), ((), ())),
                           preferred_element_type=F32)


def _dot_tn(a, b):
    return lax.dot_general(a.astype(BF16), b.astype(BF16), (((0,), (0,)), ((), ())),
                           preferred_element_type=F32)


def _split2(x):
    hi = x.astype(BF16)
    lo = (x - hi.astype(F32)).astype(BF16)
    return hi, lo


def _split3(x):
    p1 = x.astype(BF16)
    r1 = x - p1.astype(F32)
    p2 = r1.astype(BF16)
    p3 = (r1 - p2.astype(F32)).astype(BF16)
    return p1, p2, p3


def _mm_kernel(a_ref, b_ref, o_ref):
    o_ref[...] = jnp.dot(a_ref[...], b_ref[...],
                         preferred_element_type=F32).astype(o_ref.dtype)


def _mm_bias_kernel(a_ref, b_ref, bias_ref, o_ref):
    acc = jnp.dot(a_ref[...], b_ref[...], preferred_element_type=F32)
    o_ref[...] = (acc + bias_ref[...]).astype(o_ref.dtype)


def matmul(a, b, bias=None, *, out_dtype, tm, tn, name):
    m, k = a.shape
    k2, n = b.shape
    assert k == k2 and m % tm == 0 and n % tn == 0, (a.shape, b.shape, tm, tn)
    a = a.astype(BF16)
    b = b.astype(BF16)
    in_specs = [pl.BlockSpec((tm, k), lambda j, i: (i, 0)),
                pl.BlockSpec((k, tn), lambda j, i: (0, j))]
    args = [a, b]
    kern = _mm_kernel
    if bias is not None:
        in_specs.append(pl.BlockSpec((1, tn), lambda j, i: (0, j)))
        args.append(bias.astype(F32).reshape(1, n))
        kern = _mm_bias_kernel
    return pl.pallas_call(
        kern,
        grid=(n // tn, m // tm),
        in_specs=in_specs,
        out_specs=pl.BlockSpec((tm, tn), lambda j, i: (i, j)),
        out_shape=jax.ShapeDtypeStruct((m, n), out_dtype),
        name=name,
        compiler_params=pltpu.CompilerParams(
            dimension_semantics=("arbitrary", "arbitrary"),
            vmem_limit_bytes=VMEM_LIMIT_BYTES),
    )(*args)


def _mm_glu_kernel(a_ref, b_ref, bias_ref, z_ref, o_ref, *, tn):
    j = pl.program_id(0)
    acc = jnp.dot(a_ref[...], b_ref[...], preferred_element_type=F32) + bias_ref[...]
    y = a_ref[:, pl.ds(pl.multiple_of(j * tn, tn), tn)].astype(F32)
    z = z_ref[...].astype(F32)
    o_ref[...] = (y / (1.0 + jnp.exp(-acc)) * (z / (1.0 + jnp.exp(-z)))).astype(o_ref.dtype)


def matmul_glu(y, w, bias, uz, *, tm, tn, name):
    m, e = y.shape
    assert w.shape == (e, e) and uz.shape == (m, 2 * e) and m % tm == 0 and e % tn == 0
    n_tiles = e // tn
    return pl.pallas_call(
        functools.partial(_mm_glu_kernel, tn=tn),
        grid=(n_tiles, m // tm),
        in_specs=[pl.BlockSpec((tm, e), lambda j, i: (i, 0)),
                  pl.BlockSpec((e, tn), lambda j, i: (0, j)),
                  pl.BlockSpec((1, tn), lambda j, i: (0, j)),
                  pl.BlockSpec((tm, tn), lambda j, i: (i, j + n_tiles))],
        out_specs=pl.BlockSpec((tm, tn), lambda j, i: (i, j)),
        out_shape=jax.ShapeDtypeStruct((m, e), BF16),
        name=name,
        compiler_params=pltpu.CompilerParams(
            dimension_semantics=("arbitrary", "arbitrary"),
            vmem_limit_bytes=VMEM_LIMIT_BYTES),
    )(y.astype(BF16), w.astype(BF16), bias.astype(F32).reshape(1, e), uz)


def _mm_lerp_kernel(h_ref, s_ref, mu_ref, w_ref, o_ref, lhs_ref):
    @pl.when(pl.program_id(2) == 0)
    def _():
        h = h_ref[...].astype(F32)
        lhs_ref[...] = (h + (s_ref[...].astype(F32) - h) * mu_ref[0]).astype(lhs_ref.dtype)

    o_ref[0] = jnp.dot(lhs_ref[...], w_ref[0], preferred_element_type=F32).astype(o_ref.dtype)


def matmul_lerp(h, shifted, mu, w, *, out_dtype, tm, tn, name):
    m, d = h.shape
    p, d2, n = w.shape
    assert d == d2 and mu.shape == (p, d) and m % tm == 0 and n % tn == 0
    row = pl.BlockSpec((tm, d), lambda i, q, j: (i, 0))
    return pl.pallas_call(
        _mm_lerp_kernel,
        grid=(m // tm, p, n // tn),
        in_specs=[row, row,
                  pl.BlockSpec((1, 1, d), lambda i, q, j: (q, 0, 0)),
                  pl.BlockSpec((1, d, tn), lambda i, q, j: (q, 0, j))],
        out_specs=pl.BlockSpec((1, tm, tn), lambda i, q, j: (q, i, j)),
        out_shape=jax.ShapeDtypeStruct((p, m, n), out_dtype),
        scratch_shapes=[pltpu.VMEM((tm, d), BF16)],
        name=name,
        compiler_params=pltpu.CompilerParams(
            dimension_semantics=("arbitrary", "arbitrary", "arbitrary"),
            vmem_limit_bytes=VMEM_LIMIT_BYTES),
    )(h, shifted, mu.astype(F32).reshape(p, 1, d), w.astype(BF16))


def _pick(total, pref):
    t = min(pref, total)
    while total % t:
        t //= 2
    return t


def _s5_prep_kernel(are_ref, aim_ref, ls_ref, bre_ref, bim_ref, cre_ref, cim_ref,
                    w1_ref, ot_ref, ar_ref, ai_ref, *, gb):
    T = S5_T
    GC = S5_GROUP
    lane = lax.broadcasted_iota(jnp.int32, (1, 128), 1)
    fwd = lane < S5_STATE
    s_col = lax.broadcasted_iota(jnp.int32, (T, 1), 0).astype(F32)
    ri = lax.broadcasted_iota(jnp.int32, (T * GC, T * GC), 0) % T
    ci = lax.broadcasted_iota(jnp.int32, (T * GC, T * GC), 1) % T
    lane2 = lax.broadcasted_iota(jnp.int32, (1, 256), 1) % 128
    fwd2 = lane2 < S5_STATE
    half = float(T // 2)

    for g in range(gb):
        a_re = are_ref[g]
        a_im = aim_ref[g]
        dt = jnp.exp(ls_ref[g])
        x_re = dt * a_re
        x_im = dt * a_im

        def lam_pow(p):
            mag = jnp.exp(p * x_re)
            ang = p * x_im
            return mag * jnp.cos(ang), mag * jnp.sin(ang)

        l_re, l_im = lam_pow(jnp.ones((1, 1), F32))
        den = a_re * a_re + a_im * a_im
        nr, ni = l_re - 1.0, l_im
        f_re = (nr * a_re + ni * a_im) / den
        f_im = (ni * a_re - nr * a_im) / den
        b_re = bre_ref[g]
        b_im = bim_ref[g]
        bb_re = f_re * b_re - f_im * b_im
        bb_im = f_re * b_im + f_im * b_re
        c_re = cre_ref[g]
        c_im = cim_ref[g]

        def times_pow(v_re, v_im, p):
            p_re, p_im = lam_pow(p)
            o_re = v_re[:, None, :] * p_re[None, :, :] - v_im[:, None, :] * p_im[None, :, :]
            o_im = v_re[:, None, :] * p_im[None, :, :] + v_im[:, None, :] * p_re[None, :, :]
            return o_re.reshape(GC * T, 128), o_im.reshape(GC * T, 128)

        pb = jnp.where(fwd, half - s_col, s_col - half)
        pc = -pb
        bl_re, bl_im = times_pow(bb_re, bb_im, pb)
        cl_re, cl_im = times_pow(c_re, c_im, pc)
        lhs = jnp.concatenate([bl_re, bl_im], axis=1)
        rhs = jnp.concatenate([cl_re, -cl_im], axis=1)
        r_hi, r_lo = _split2(rhs)

        def dot3(l):
            l_hi, l_lo = _split2(l)
            dn = (((1,), (1,)), ((), ()))
            return (lax.dot_general(l_hi, r_hi, dn, preferred_element_type=F32)
                    + lax.dot_general(l_hi, r_lo, dn, preferred_element_type=F32)
                    + lax.dot_general(l_lo, r_hi, dn, preferred_element_type=F32))

        t_f = dot3(jnp.where(fwd2, lhs, 0.0))
        t_b = dot3(jnp.where(fwd2, 0.0, lhs))
        toep = jnp.where(ci >= ri, t_f, 0.0) + jnp.where(ri >= ci, t_b, 0.0)

        ps = jnp.where(fwd, (T - 1.0) - s_col, s_col)
        s_re, s_im = times_pow(bb_re, bb_im, ps)
        w1_ref[g] = jnp.concatenate([toep, s_re, s_im], axis=1).astype(w1_ref.dtype)

        po = jnp.where(fwd, s_col + 1.0, T - s_col)
        o_re, o_im = times_pow(c_re, c_im, po)
        ot_ref[g] = jnp.concatenate([o_re, -o_im], axis=1).astype(ot_ref.dtype)

        pw_re, pw_im = lam_pow(jnp.full((1, 1), float(T), F32))
        ar_ref[g] = pw_re
        ai_ref[g] = pw_im


def _s5_kernel(u_ref, w1_ref, ot_ref, ar_ref, ai_ref, y_ref,
               p_ref, hf_re, hf_im, hb_re, hb_im, *, nb, n_ctx, n_chunks):
    gs = u_ref.shape[0]
    for g in range(gs):
        p_ref[g] = _dot(u_ref[g], w1_ref[g])
    ar = jnp.broadcast_to(ar_ref[...], (gs, nb, 128))
    ai = jnp.broadcast_to(ai_ref[...], (gs, nb, 128))
    fwd = lax.broadcasted_iota(jnp.int32, (gs, nb, 128), 2) < S5_STATE

    def step(i, carry):
        h_re, h_im = carry
        cb = jnp.where(i < n_ctx, n_ctx - 1 - i, n_chunks - 1 - (i - n_ctx))
        rf = pl.ds(pl.multiple_of(i * nb, nb), nb)
        rb = pl.ds(pl.multiple_of(cb * nb, nb), nb)
        hf_re[:, rf, :] = h_re
        hf_im[:, rf, :] = h_im
        hb_re[:, rb, :] = h_re
        hb_im[:, rb, :] = h_im
        x_re = jnp.where(fwd, p_ref[:, rf, 256:384], p_ref[:, rb, 256:384])
        x_im = jnp.where(fwd, p_ref[:, rf, 384:512], p_ref[:, rb, 384:512])
        n_re = ar * h_re - ai * h_im + x_re
        n_im = ar * h_im + ai * h_re + x_im
        return n_re, n_im

    zero = jnp.zeros((gs, nb, 128), F32)
    lax.fori_loop(0, n_chunks, step, (zero, zero))

    rows = p_ref.shape[1]
    fwd_all = lax.broadcasted_iota(jnp.int32, (rows, 128), 1) < S5_STATE
    for g in range(gs):
        h_all = jnp.concatenate([jnp.where(fwd_all, hf_re[g], hb_re[g]),
                                 jnp.where(fwd_all, hf_im[g], hb_im[g])], axis=1)
        y = p_ref[g, :, 0:256] + _dot_nt(h_all, ot_ref[g])
        y_ref[g] = y.astype(y_ref.dtype)


def s5_operators(a_re, a_im, log_step, b_re, b_im, c_re, c_im):
    n_dir, g, n = a_re.shape
    gb = 8

    def lanes(x):
        return jnp.concatenate([x[0], x[1]], axis=-1).astype(F32)

    are = lanes(a_re).reshape(g, 1, 2 * n)
    aim = lanes(a_im).reshape(g, 1, 2 * n)
    ls = lanes(jnp.broadcast_to(log_step[:, :, None], (n_dir, g, n))).reshape(g, 1, 2 * n)
    bre = lanes(jnp.swapaxes(b_re, -1, -2))
    bim = lanes(jnp.swapaxes(b_im, -1, -2))
    cre = lanes(c_re)
    cim = lanes(c_im)
    row = pl.BlockSpec((gb, 1, 2 * n), lambda i: (i, 0, 0))
    mat = pl.BlockSpec((gb, S5_GROUP, 2 * n), lambda i: (i, 0, 0))
    tg = S5_T * S5_GROUP
    return pl.pallas_call(
        functools.partial(_s5_prep_kernel, gb=gb),
        grid=(g // gb,),
        in_specs=[row, row, row, mat, mat, mat, mat],
        out_specs=[pl.BlockSpec((gb, tg, tg + 4 * n), lambda i: (i, 0, 0)),
                   pl.BlockSpec((gb, tg, 4 * n), lambda i: (i, 0, 0)),
                   row, row],
        out_shape=[jax.ShapeDtypeStruct((g, tg, tg + 4 * n), BF16),
                   jax.ShapeDtypeStruct((g, tg, 4 * n), BF16),
                   jax.ShapeDtypeStruct((g, 1, 2 * n), F32),
                   jax.ShapeDtypeStruct((g, 1, 2 * n), F32)],
        name="s5_operators",
        compiler_params=pltpu.CompilerParams(dimension_semantics=("arbitrary",),
                                             vmem_limit_bytes=VMEM_LIMIT_BYTES),
    )(are, aim, ls, bre, bim, cre, cim)


def s5_scan(u_g, w1, ot, ar, ai, *, nb, n_ctx, n_chunks):
    g, rows, tg = u_g.shape
    assert rows == n_chunks * nb and tg == S5_T * S5_GROUP
    gs = _pick(g, 4)
    blk = lambda w: pl.BlockSpec((gs,) + w, lambda i: (i, 0, 0))
    return pl.pallas_call(
        functools.partial(_s5_kernel, nb=nb, n_ctx=n_ctx, n_chunks=n_chunks),
        grid=(g // gs,),
        in_specs=[blk((rows, tg)), blk(w1.shape[1:]), blk(ot.shape[1:]),
                  blk((1, 128)), blk((1, 128))],
        out_specs=blk((rows, tg)),
        out_shape=jax.ShapeDtypeStruct((g, rows, tg), F32),
        scratch_shapes=[pltpu.VMEM((gs, rows, w1.shape[2]), F32)]
                       + [pltpu.VMEM((gs, rows, 128), F32)] * 4,
        name="s5_scan",
        compiler_params=pltpu.CompilerParams(dimension_semantics=("arbitrary",),
                                             vmem_limit_bytes=VMEM_LIMIT_BYTES),
    )(u_g, w1, ot, ar, ai)


def _eye(n):
    r = lax.broadcasted_iota(jnp.int32, (n, n), 0)
    c = lax.broadcasted_iota(jnp.int32, (n, n), 1)
    return jnp.where(r == c, 1.0, 0.0).astype(BF16)


def _to_groups_kernel(u_ref, o_ref, *, nb, n_j):
    g = o_ref.shape[0]
    eye = _eye(g)
    for b in range(nb):
        x = u_ref[b]
        rows = jnp.concatenate([x[:, j * g:(j + 1) * g] for j in range(n_j)], axis=0)
        o_ref[:, b, :] = _dot_nt(eye, rows).astype(o_ref.dtype)


def _from_groups_kernel(y_ref, o_ref, *, nb, n_j):
    g = y_ref.shape[0]
    t = o_ref.shape[1]
    eye = _eye(y_ref.shape[2]).astype(F32)
    for b in range(nb):
        cols = lax.dot_general(eye, y_ref[:, b, :], (((1,), (1,)), ((), ())),
                               preferred_element_type=F32)
        for i in range(n_j):
            o_ref[b, :, i * g:(i + 1) * g] = cols[i * t:(i + 1) * t, :]


def to_groups(uz, *, n_chunks):
    b, l, e2 = uz.shape
    e = e2 // 2
    g = e // S5_GROUP
    return pl.pallas_call(
        functools.partial(_to_groups_kernel, nb=b, n_j=S5_GROUP),
        grid=(n_chunks,),
        in_specs=[pl.BlockSpec((b, S5_T, e), lambda c: (0, c, 0))],
        out_specs=pl.BlockSpec((g, b, S5_T * S5_GROUP), lambda c: (0, c, 0)),
        out_shape=jax.ShapeDtypeStruct((g, n_chunks * b, S5_T * S5_GROUP), F32),
        name="s5_to_groups",
        compiler_params=pltpu.CompilerParams(dimension_semantics=("arbitrary",),
                                             vmem_limit_bytes=VMEM_LIMIT_BYTES),
    )(uz)


def from_groups(y_g, *, nb, n_chunks):
    g, rows, tg = y_g.shape
    e = g * S5_GROUP
    return pl.pallas_call(
        functools.partial(_from_groups_kernel, nb=nb, n_j=S5_GROUP),
        grid=(n_chunks,),
        in_specs=[pl.BlockSpec((g, nb, tg), lambda c: (0, c, 0))],
        out_specs=pl.BlockSpec((nb, S5_T, e), lambda c: (0, c, 0)),
        out_shape=jax.ShapeDtypeStruct((nb, n_chunks * S5_T, e), F32),
        name="s5_from_groups",
        compiler_params=pltpu.CompilerParams(dimension_semantics=("arbitrary",),
                                             vmem_limit_bytes=VMEM_LIMIT_BYTES),
    )(y_g)


def _rwkv_chunk_index(item, n_ctx, n_chunks):
    i = item % n_chunks
    back = jnp.where(i < n_ctx, n_ctx - 1 - i, n_chunks - 1 - (i - n_ctx))
    return jnp.where(item < n_chunks, i, back)


_HB_AC, _HB_RC, _HB_BH, _HB_KH, _HB_V, _HB_RB, _HB_RK, _HB_DM = range(8)
_HF_AKV, _HF_AO, _HF_Q, _HF_ZS = range(4)


def _rwkv_step(r_ref, k_ref, v_ref, z_ref, tw_ref, ta_ref, w2_ref, a2_ref, w0_ref, a0_ref,
               kk_ref, ka_ref, rk_ref, lnw_ref, lnb_ref, o_ref, s_ref, acc_ref,
               hbw, hfw, hww, hbr, hfr, hwr, *, n_ctx, n_chunks, n_pairs):
    C = RWKV_CHUNK
    n_items = 2 * n_chunks
    step = pl.program_id(2)
    pairs = range(n_pairs)

    def each(f, *lists):
        return [f(*args) for args in zip(*lists)]

    def cols(x):
        return [x[:, p * 128:(p + 1) * 128] for p in pairs]

    def join(xs):
        return jnp.concatenate(xs, axis=1)

    qa = jnp.minimum(step, n_items - 1)
    a_fwd = qa < n_chunks
    a_cidx = _rwkv_chunk_index(qa, n_ctx, n_chunks)
    d = jnp.where(a_fwd, 0, 1)
    sgn = jnp.where(a_fwd, 1, -1)
    qb = jnp.maximum(step - 1, 0)
    b_fwd = qb < n_chunks
    b_cidx = _rwkv_chunk_index(qb, n_ctx, n_chunks)
    b_live = step >= 1

    t_i = lax.broadcasted_iota(jnp.int32, (C, 128), 0)
    lane = lax.broadcasted_iota(jnp.int32, (C, 128), 1)
    s_i = lane % C
    lane_head = lane // RWKV_HEAD
    before = (s_i - t_i) * sgn
    strict = before < 0
    incl = before <= 0
    same_blk = (t_i // 16) == (s_i // 16)
    diag_m = strict & same_blk
    off_m = strict & jnp.logical_not(same_blk)
    tr = lax.broadcasted_iota(jnp.int32, (C, C), 0)
    tc = lax.broadcasted_iota(jnp.int32, (C, C), 1)
    cum_m = jnp.where((tc - tr) * sgn <= 0, 1.0, 0.0).astype(BF16)
    ri = lax.broadcasted_iota(jnp.int32, (128, 128), 0)
    ci = lax.broadcasted_iota(jnp.int32, (128, 128), 1)
    head_ones = jnp.where((ri // RWKV_HEAD) == (ci // RWKV_HEAD), 1.0, 0.0).astype(BF16)

    @pl.when(qb % n_chunks == 0)
    def _():
        s_ref[...] = jnp.zeros(s_ref.shape, F32)

    def stack(x):
        x = x.astype(BF16)
        zero = jnp.zeros_like(x)
        return jnp.concatenate([jnp.where(lane_head == 0, x, zero),
                                jnp.where(lane_head == 1, x, zero)], axis=0)

    def pack(x):
        return jnp.where(lane_head == 0, x[:C], x[C:])

    def head_sum(x):
        tot = _dot(jnp.concatenate(cols(x), axis=0), head_ones)
        return join([tot[p * C:(p + 1) * C] for p in pairs])

    mm = lambda l, r_: jnp.dot(l.astype(BF16), r_, preferred_element_type=F32)
    wide = lambda l, *rs: mm(l, jnp.concatenate(rs, axis=1))

    r = r_ref[0, 0].astype(F32)
    k = k_ref[0, 0].astype(F32)
    v = v_ref[0, 0].astype(F32)
    wp = w0_ref[pl.ds(d, 1), :] + _dot(jnp.tanh(tw_ref[0, 0]), w2_ref[d])
    ap = a0_ref[pl.ds(d, 1), :] + _dot(ta_ref[d, 0], a2_ref[d])
    sp = jnp.maximum(-wp, 0.0) + jnp.log(1.0 + jnp.exp(-jnp.abs(wp)))
    lw = -jnp.exp(-sp - 0.5)
    a_rate = 1.0 / (1.0 + jnp.exp(-ap))
    k_a = ka_ref[...]
    kkr = k * kk_ref[...]
    kd = k * (1.0 + (a_rate - 1.0) * k_a)
    lc = sum(jnp.dot(cum_m, p, preferred_element_type=F32) for p in _split3(lw))
    l_mid = lc[C // 2:C // 2 + 1, :]
    l_end = jnp.where(a_fwd, lc[C - 1:C, :], lc[0:1, :])
    dl = lc - l_mid
    e_in = jnp.exp(dl)
    e_ex = jnp.exp(dl - lw)
    e_iv = jnp.exp(-dl)
    c_row = jnp.exp(l_mid)
    e2 = jnp.exp(l_end - l_mid)
    kk_all = kkr / jnp.maximum(jnp.sqrt(head_sum(kkr * kkr)), NORM_EPS)
    at_all = -kk_all * e_ex
    bt_all = kk_all * (a_rate * e_iv)
    rt_all = r * e_in
    kt_all = kd * e_iv
    hww[...] = jnp.exp(l_end)
    hbw[_HB_AC] = (at_all * c_row).astype(BF16)
    hbw[_HB_RC] = (rt_all * c_row).astype(BF16)
    hbw[_HB_BH] = (bt_all * e2).astype(BF16)
    hbw[_HB_KH] = (kt_all * e2).astype(BF16)
    hbw[_HB_V] = v_ref[0, 0]
    at, bt, rp, kp = cols(at_all), cols(bt_all), cols(rt_all), cols(kt_all)
    vs = each(stack, cols(v))

    s_old = [s_ref[p] for p in pairs]
    asr = each(lambda a, r_, s: _dot_nt(jnp.concatenate([a, r_], axis=0), stack(s)),
               cols(hbr[_HB_AC]), cols(hbr[_HB_RC]), s_old)
    a4 = each(lambda a, r_, b, k_: _dot_nt(jnp.concatenate([a, r_], axis=0),
                                           jnp.concatenate([stack(b), stack(k_)], axis=0)),
              at, rp, bt, kp)
    ak = each(lambda m: jnp.where(strict, m[:C, 128:], 0.0), a4)
    hbw[_HB_RB] = join(each(lambda m: jnp.where(incl, m[C:, :128], 0.0), a4)).astype(BF16)
    hbw[_HB_RK] = join(each(lambda m: jnp.where(incl, m[C:, 128:], 0.0), a4)).astype(BF16)
    ad = each(lambda m: jnp.where(diag_m, m[:C, :128], 0.0), a4)
    ao = each(lambda m: jnp.where(off_m, m[:C, :128], 0.0), a4)
    hfw[_HF_AO] = join(ao)

    x0 = each(lambda m, q: m[:C] + q, asr, cols(hfr[_HF_AKV]))
    ao_b = cols(hfr[_HF_AO])
    dx = each(lambda d_, o, x: wide(d_, stack(o), stack(x)),
              cols(hbr[_HB_DM]), ao_b, x0)
    ad_s = each(stack, ad)
    hfw[_HF_AKV] = join(each(mm, ak, vs))
    p2 = each(mm, ad, ad_s)

    n1 = each(lambda o, m: o + m[:, :128], ao_b, dx)
    xa = each(lambda x, m: x + m[:, 128:], x0, dx)
    xc = each(lambda n_, x: x + mm(n_, stack(x)), n1, xa)
    p43 = each(lambda p, a_s: wide(p, stack(p), a_s), p2, ad_s)
    p4 = [m[:, :128] for m in p43]
    s3 = each(lambda a, p, m: a + p + m[:, 128:], ad, p2, p43)

    w1_ = each(lambda n_, x: mm(n_, stack(x)), n1, xc)
    p8s = each(lambda p, s: wide(p, stack(p), stack(s)), p4, s3)
    p8 = [m[:, :128] for m in p8s]
    s7 = each(lambda s, p, m: s + p + m[:, 128:], s3, p4, p8s)

    u = each(lambda n_, x, w_: x + mm(n_, stack(w_)), n1, xc, w1_)
    hbw[_HB_DM] = join(each(lambda s, p: s + p + mm(p, stack(s)), s7, p8)).astype(BF16)

    v_b = cols(hbr[_HB_V])
    ys = each(lambda m, rb_, rk_, u_, v_: m[C:] + mm(
        jnp.concatenate([rb_, rk_], axis=1), jnp.concatenate([stack(u_), stack(v_)], axis=0)),
        asr, cols(hbr[_HB_RB]), cols(hbr[_HB_RK]), u, v_b)
    upd = each(lambda u_, v_, b, k_: pack(_dot_tn(
        jnp.concatenate([u_, v_], axis=0), jnp.concatenate([b, k_], axis=0))),
        u, v_b, cols(hbr[_HB_BH]), cols(hbr[_HB_KH]))
    wc_b = hwr[...]
    for p in pairs:
        s_ref[p] = s_old[p] * wc_b[:, p * 128:(p + 1) * 128] + upd[p]

    y = join(ys)
    a_lat = a_cidx >= n_ctx
    b_lat = jnp.logical_and(b_cidx >= n_ctx, b_live)
    orow = pl.multiple_of(jnp.maximum(b_cidx - n_ctx, 0) * C, C)

    @pl.when(jnp.logical_and(a_lat, jnp.logical_not(a_fwd)))
    def _():
        ar_f = 1.0 / (1.0 + jnp.exp(-(a0_ref[0:1, :] + _dot(ta_ref[0, 0], a2_ref[0]))))
        k_sum = k * (2.0 + (ar_f + a_rate - 2.0) * k_a)
        hfw[_HF_Q] = r * k_sum * rk_ref[...]
        zg = z_ref[0, 0].astype(F32)
        hfw[_HF_ZS] = zg / (1.0 + jnp.exp(-zg))

    @pl.when(jnp.logical_and(b_lat, b_fwd))
    def _():
        acc_ref[pl.ds(orow, C), :] = y

    @pl.when(jnp.logical_and(b_lat, jnp.logical_not(b_fwd)))
    def _():
        y_sum = acc_ref[pl.ds(orow, C), :] + y
        inv = 1.0 / RWKV_HEAD
        dev = y_sum - head_sum(y_sum) * inv
        yn = dev * lax.rsqrt(head_sum(dev * dev) * inv + GN_EPS)
        yn = yn * lnw_ref[...] + lnb_ref[...]
        bonus = head_sum(hfr[_HF_Q]) * hbr[_HB_V].astype(F32)
        o_ref[0] = ((yn + bonus) * hfr[_HF_ZS]).astype(o_ref.dtype)


def _rwkv_kernel(*refs, n_ctx, n_chunks, n_pairs):
    *io, s_ref, acc_ref, hb0, hb1, hf0, hf1, hw0, hw1 = refs
    step = pl.program_id(2)
    run = functools.partial(_rwkv_step, *io, s_ref, acc_ref,
                            n_ctx=n_ctx, n_chunks=n_chunks, n_pairs=n_pairs)

    @pl.when(step == 0)
    def _():
        hb1[...] = jnp.zeros(hb1.shape, hb1.dtype)
        hf1[...] = jnp.zeros(hf1.shape, hf1.dtype)
        hw1[...] = jnp.zeros(hw1.shape, hw1.dtype)

    @pl.when(step % 2 == 0)
    def _():
        run(hb0, hf0, hw0, hb1, hf1, hw1)

    @pl.when(step % 2 == 1)
    def _():
        run(hb1, hf1, hw1, hb0, hf0, hw0)


def rwkv_scan(rkvz, tw, ta, w2, a2, w0, a0, k_k, k_a, r_k, ln_w, ln_b, *, n_ctx_tokens):
    _, b, l, e = rkvz.shape
    rank = tw.shape[-1]
    C = RWKV_CHUNK
    pw = min(e, 2048)
    n_pairs = pw // 128
    n_chunks = l // C
    n_ctx = n_ctx_tokens // C
    n_items = 2 * n_chunks
    l_out = l - n_ctx_tokens
    cix = functools.partial(_rwkv_chunk_index, n_ctx=n_ctx, n_chunks=n_chunks)
    item_a = lambda t: jnp.minimum(t, n_items - 1)
    item_b = lambda t: jnp.maximum(t - 1, 0)

    def out_chunk(t):
        q = item_b(t)
        c = cix(q)
        return jnp.where(jnp.logical_and(q >= n_chunks, c >= n_ctx), c, n_chunks - 1) - n_ctx

    seq = [pl.BlockSpec((1, 1, C, pw), functools.partial(
        lambda i, j, t, p: (p, i, cix(item_a(t)), j), p=p)) for p in range(4)]
    low_d = pl.BlockSpec((1, 1, C, rank),
                         lambda i, j, t: (item_a(t) // n_chunks, i, cix(item_a(t)), 0))
    low_2 = pl.BlockSpec((2, 1, C, rank), lambda i, j, t: (0, i, cix(item_a(t)), 0))
    up = pl.BlockSpec((2, rank, pw), lambda i, j, t: (0, 0, j))
    vec2 = pl.BlockSpec((2, pw), lambda i, j, t: (0, j))
    vec = pl.BlockSpec((1, pw), lambda i, j, t: (0, j))
    row = lambda x: x.astype(F32).reshape(1, e)
    return pl.pallas_call(
        functools.partial(_rwkv_kernel, n_ctx=n_ctx, n_chunks=n_chunks, n_pairs=n_pairs),
        grid=(b, e // pw, n_items + 1),
        in_specs=seq + [low_d, low_2, up, up, vec2, vec2, vec, vec, vec, vec, vec],
        out_specs=pl.BlockSpec((1, C, pw), lambda i, j, t: (i, out_chunk(t), j)),
        out_shape=jax.ShapeDtypeStruct((b, l_out, e), BF16),
        scratch_shapes=[pltpu.VMEM((n_pairs, C, 128), F32),
                        pltpu.VMEM((l_out, pw), F32),
                        pltpu.VMEM((8, C, pw), BF16), pltpu.VMEM((8, C, pw), BF16),
                        pltpu.VMEM((4, C, pw), F32), pltpu.VMEM((4, C, pw), F32),
                        pltpu.VMEM((1, pw), F32), pltpu.VMEM((1, pw), F32)],
        name="rwkv_scan",
        compiler_params=pltpu.CompilerParams(
            dimension_semantics=("arbitrary", "arbitrary", "arbitrary"),
            vmem_limit_bytes=VMEM_LIMIT_BYTES),
    )(rkvz, rkvz, rkvz, rkvz, tw, ta, w2.astype(BF16), a2.astype(BF16), w0.astype(F32),
      a0.astype(F32),
      row(k_k), row(k_a), row(r_k), row(ln_w), row(ln_b))


def _rmsnorm(x, g):
    xf = x.astype(F32)
    y = xf * lax.rsqrt(jnp.mean(xf * xf, axis=-1, keepdims=True) + RMS_EPS)
    return y * g.astype(F32)


def _modulation(c, c_ctx, ada_w, ada_b):
    d = c.shape[-1]
    cond = jnp.concatenate([c, c_ctx[None]], axis=0)
    rows = cond.shape[0]
    pad = (-rows) % 16
    act = jnp.pad(jax.nn.silu(cond), ((0, pad), (0, 0)))
    m = matmul(act, ada_w, ada_b, out_dtype=F32, tm=rows + pad, tn=_pick(3 * d, 1024),
               name="mm_modulation")
    m = m[:rows]
    lat = [t[:, None, :] for t in jnp.split(m[:rows - 1], 3, axis=-1)]
    ctx = [t[None] for t in jnp.split(m[rows - 1:], 3, axis=-1)]
    return lat, ctx


def _q_shift(h):
    b, l, d = h.shape
    rows = l // GRID_W
    g = h.reshape(b, rows, GRID_W, d)
    q = d // 4
    left = jnp.pad(g[:, :, :-1, :q], ((0, 0), (0, 0), (1, 0), (0, 0)))
    right = jnp.pad(g[:, :, 1:, q:2 * q], ((0, 0), (0, 0), (0, 1), (0, 0)))
    up = jnp.pad(g[:, :-1, :, 2 * q:3 * q], ((0, 0), (1, 0), (0, 0), (0, 0)))
    down = jnp.pad(g[:, 1:, :, 3 * q:], ((0, 0), (0, 1), (0, 0), (0, 0)))
    return jnp.concatenate([left, right, up, down], axis=-1).reshape(b, l, d)


def _seq_shift(h):
    half = h.shape[-1] // 2
    prev = jnp.pad(h[:, :-1, :half], ((0, 0), (1, 0), (0, 0)))
    nxt = jnp.pad(h[:, 1:, half:], ((0, 0), (0, 1), (0, 0)))
    return jnp.concatenate([prev, nxt], axis=-1)


def _s5_layer(x_lat, x_ctx, c, c_ctx, norm_g, ada_w, ada_b, in_w, a_re, a_im, log_step,
              b_re, b_im, c_re, c_im, d_skip, glu_w, glu_b, out_w):
    b, l_lat, d = x_lat.shape
    l_ctx = x_ctx.shape[1]
    l = l_ctx + l_lat
    e = in_w.shape[1] // 2
    g = e // S5_GROUP
    (sh_l, sc_l, g_l), (sh_c, sc_c, g_c) = _modulation(c, c_ctx, ada_w, ada_b)
    h_lat = _rmsnorm(x_lat, norm_g) * (1.0 + sc_l) + sh_l
    h_ctx = _rmsnorm(x_ctx, norm_g) * (1.0 + sc_c) + sh_c
    h = jnp.concatenate([h_ctx, h_lat], axis=1).astype(BF16).reshape(b * l, d)
    m = b * l
    tm = _pick(m, 1024)
    te = _pick(e, 1024)

    lane = jnp.arange(e)
    old = (lane % g) * S5_GROUP + lane // g
    perm = (lane[:, None] == old[None, :]).astype(BF16)
    in_w_p = matmul(in_w.reshape(2 * d, e), perm, out_dtype=BF16, tm=_pick(2 * d, 1024), tn=te,
                    name="mm_perm_in").reshape(d, 2 * e)
    rows_p = lambda w: (w.astype(BF16).reshape(g, S5_GROUP, w.shape[1]).transpose(1, 0, 2)
                        .reshape(w.shape))
    glu_w_p = matmul(rows_p(glu_w), perm, out_dtype=BF16, tm=te, tn=te, name="mm_perm_glu")
    out_w_p = rows_p(out_w)
    d_p = d_skip.astype(F32)[old]
    glu_b_p = glu_b[old]

    uz = matmul(h, in_w_p, out_dtype=BF16, tm=tm, tn=_pick(2 * e, 1024), name="mm_s5_in")
    u = uz[:, :e]

    n_chunks = l // S5_T
    u_g = to_groups(uz.reshape(b, l, 2 * e), n_chunks=n_chunks)
    w1, ot, ar, ai = s5_operators(a_re, a_im, log_step, b_re, b_im, c_re, c_im)
    y_g = s5_scan(u_g, w1, ot, ar, ai, nb=b, n_ctx=l_ctx // S5_T, n_chunks=n_chunks)
    y_ssm = from_groups(y_g, nb=b, n_chunks=n_chunks).reshape(m, e)

    y1 = jax.nn.gelu(y_ssm.astype(F32) + d_p * u.astype(F32)).astype(BF16)
    y2 = matmul_glu(y1, glu_w_p, glu_b_p, uz, tm=tm, tn=te, name="mm_s5_glu")
    o = matmul(y2, out_w_p, out_dtype=F32, tm=tm, tn=_pick(d, 1024),
               name="mm_s5_out").reshape(b, l, d)
    x_ctx = x_ctx + g_c * o[:, :l_ctx]
    x_lat = x_lat + g_l * o[:, l_ctx:]
    return x_lat, x_ctx


def _rwkv_layer(x_lat, x_ctx, c, c_ctx, norm_g, ada_w, ada_b, mu, in_w, w0, w1, w2, a0, a1, a2,
                k_k, k_a, r_k, ln_w, ln_b, out_w):
    b, l_lat, d = x_lat.shape
    l_ctx = x_ctx.shape[1]
    l = l_ctx + l_lat
    e = in_w.shape[2]
    m = b * l
    (sh_l, sc_l, g_l), (sh_c, sc_c, _) = _modulation(c, c_ctx, ada_w, ada_b)
    h_lat = _rmsnorm(x_lat, norm_g) * (1.0 + sc_l) + sh_l
    h_ctx = _rmsnorm(x_ctx, norm_g) * (1.0 + sc_c) + sh_c
    h = jnp.concatenate([h_ctx, h_lat], axis=1).astype(BF16).reshape(m, d)
    shifted = (jnp.concatenate([_seq_shift(h_ctx), _q_shift(h_lat)], axis=1)
               .astype(BF16).reshape(m, d))

    tm = _pick(m, 1024)
    tn = _pick(e, 1024)
    rkvz = matmul_lerp(h, shifted, mu[:4], in_w, out_dtype=BF16, tm=tm, tn=tn,
                       name="mm_rwkv_in").reshape(4, b, l, e)

    n_dir, _, lora = w1.shape
    lp = (-lora) % 128
    down = jnp.stack([jnp.concatenate([jnp.pad(l1[i], ((0, 0), (0, lp))) for i in range(n_dir)],
                                      axis=1) for l1 in (w1, a1)])
    low = matmul_lerp(h, shifted, mu[4:6], down, out_dtype=F32, tm=tm, tn=down.shape[2],
                      name="mm_lora_down")
    tw, ta = (low[i].reshape(b, l, n_dir, lora + lp).transpose(2, 0, 1, 3) for i in range(2))

    pad_up = lambda l2: jnp.pad(l2, ((0, 0), (0, lp), (0, 0)))
    yg = rwkv_scan(rkvz, tw, ta, pad_up(w2), pad_up(a2), w0, a0, k_k, k_a, r_k.reshape(e),
                   ln_w, ln_b, n_ctx_tokens=l_ctx)
    o = matmul(yg.reshape(b * l_lat, e), out_w, out_dtype=F32,
               tm=_pick(b * l_lat, 1024), tn=_pick(d, 1024), name="mm_rwkv_out").reshape(b, l_lat, d)
    return x_lat + g_l * o


def kernel(x, c, ctx, c_ctx, l0_norm_g, l0_ada_w, l0_ada_b, l0_in_w, l0_a_re, l0_a_im, l0_log_step, l0_b_re, l0_b_im, l0_c_re, l0_c_im, l0_d, l0_glu_w, l0_glu_b, l0_out_w, l1_norm_g, l1_ada_w, l1_ada_b, l1_mu, l1_in_w, l1_w0, l1_w1, l1_w2, l1_a0, l1_a1, l1_a2, l1_k_k, l1_k_a, l1_r_k, l1_ln_w, l1_ln_b, l1_out_w, final_norm_g):
    x_lat, x_ctx = _s5_layer(x, ctx, c, c_ctx, l0_norm_g, l0_ada_w, l0_ada_b, l0_in_w, l0_a_re,
                             l0_a_im, l0_log_step, l0_b_re, l0_b_im, l0_c_re, l0_c_im, l0_d,
                             l0_glu_w, l0_glu_b, l0_out_w)
    x_lat = _rwkv_layer(x_lat, x_ctx, c, c_ctx, l1_norm_g, l1_ada_w, l1_ada_b, l1_mu, l1_in_w,
                        l1_w0, l1_w1, l1_w2, l1_a0, l1_a1, l1_a2, l1_k_k, l1_k_a, l1_r_k,
                        l1_ln_w, l1_ln_b, l1_out_w)
    return _rmsnorm(x_lat, final_norm_g).astype(x.dtype)
```

```python
import functools
import math

import jax
import jax.numpy as jnp
from jax import lax
from jax.experimental import pallas as pl
from jax.experimental.pallas import tpu as pltpu

F32 = jnp.float32
BF16 = jnp.bfloat16

GRID_W = 64
S5_GROUP = 16
S5_STATE = 64
S5_T = 16
RWKV_HEAD = 64
RWKV_CHUNK = 64
RMS_EPS = 1e-6
GN_EPS = 64e-5
NORM_EPS = 1e-12

VMEM_LIMIT_BYTES = 56 * 1024 * 1024


def _dot(a, b):
    return jnp.dot(a.astype(BF16), b.astype(BF16), preferred_element_type=F32)


def _dot_nt(a, b):
    return lax.dot_general(a.astype(BF16), b.astype(BF16), (((1,), (1,)), ((), ())),
                           preferred_element_type=F32)


def _dot_tn(a, b):
    return lax.dot_general(a.astype(BF16), b.astype(BF16), (((0,), (0,)), ((), ())),
                           preferred_element_type=F32)


def _split2(x):
    hi = x.astype(BF16)
    lo = (x - hi.astype(F32)).astype(BF16)
    return hi, lo


def _split3(x):
    p1 = x.astype(BF16)
    r1 = x - p1.astype(F32)
    p2 = r1.astype(BF16)
    p3 = (r1 - p2.astype(F32)).astype(BF16)
    return p1, p2, p3


def _mm_kernel(a_ref, b_ref, o_ref):
    o_ref[...] = jnp.dot(a_ref[...], b_ref[...],
                         preferred_element_type=F32).astype(o_ref.dtype)


def _mm_bias_kernel(a_ref, b_ref, bias_ref, o_ref):
    acc = jnp.dot(a_ref[...], b_ref[...], preferred_element_type=F32)
    o_ref[...] = (acc + bias_ref[...]).astype(o_ref.dtype)


def matmul(a, b, bias=None, *, out_dtype, tm, tn, name):
    m, k = a.shape
    k2, n = b.shape
    assert k == k2 and m % tm == 0 and n % tn == 0, (a.shape, b.shape, tm, tn)
    a = a.astype(BF16)
    b = b.astype(BF16)
    in_specs = [pl.BlockSpec((tm, k), lambda j, i: (i, 0)),
                pl.BlockSpec((k, tn), lambda j, i: (0, j))]
    args = [a, b]
    kern = _mm_kernel
    if bias is not None:
        in_specs.append(pl.BlockSpec((1, tn), lambda j, i: (0, j)))
        args.append(bias.astype(F32).reshape(1, n))
        kern = _mm_bias_kernel
    return pl.pallas_call(
        kern,
        grid=(n // tn, m // tm),
        in_specs=in_specs,
        out_specs=pl.BlockSpec((tm, tn), lambda j, i: (i, j)),
        out_shape=jax.ShapeDtypeStruct((m, n), out_dtype),
        name=name,
        compiler_params=pltpu.CompilerParams(
            dimension_semantics=("arbitrary", "arbitrary"),
            vmem_limit_bytes=VMEM_LIMIT_BYTES),
    )(*args)


def _mm_glu_kernel(a_ref, b_ref, bias_ref, z_ref, o_ref, *, tn):
    j = pl.program_id(0)
    acc = jnp.dot(a_ref[...], b_ref[...], preferred_element_type=F32) + bias_ref[...]
    y = a_ref[:, pl.ds(pl.multiple_of(j * tn, tn), tn)].astype(F32)
    z = z_ref[...].astype(F32)
    o_ref[...] = (y / (1.0 + jnp.exp(-acc)) * (z / (1.0 + jnp.exp(-z)))).astype(o_ref.dtype)


def matmul_glu(y, w, bias, uz, *, tm, tn, name):
    m, e = y.shape
    assert w.shape == (e, e) and uz.shape == (m, 2 * e) and m % tm == 0 and e % tn == 0
    n_tiles = e // tn
    return pl.pallas_call(
        functools.partial(_mm_glu_kernel, tn=tn),
        grid=(n_tiles, m // tm),
        in_specs=[pl.BlockSpec((tm, e), lambda j, i: (i, 0)),
                  pl.BlockSpec((e, tn), lambda j, i: (0, j)),
                  pl.BlockSpec((1, tn), lambda j, i: (0, j)),
                  pl.BlockSpec((tm, tn), lambda j, i: (i, j + n_tiles))],
        out_specs=pl.BlockSpec((tm, tn), lambda j, i: (i, j)),
        out_shape=jax.ShapeDtypeStruct((m, e), BF16),
        name=name,
        compiler_params=pltpu.CompilerParams(
            dimension_semantics=("arbitrary", "arbitrary"),
            vmem_limit_bytes=VMEM_LIMIT_BYTES),
    )(y.astype(BF16), w.astype(BF16), bias.astype(F32).reshape(1, e), uz)


def _mm_lerp_kernel(h_ref, s_ref, mu_ref, w_ref, o_ref, lhs_ref):
    @pl.when(pl.program_id(2) == 0)
    def _():
        h = h_ref[...].astype(F32)
        lhs_ref[...] = (h + (s_ref[...].astype(F32) - h) * mu_ref[0]).astype(lhs_ref.dtype)

    o_ref[0] = jnp.dot(lhs_ref[...], w_ref[0], preferred_element_type=F32).astype(o_ref.dtype)


def matmul_lerp(h, shifted, mu, w, *, out_dtype, tm, tn, name):
    m, d = h.shape
    p, d2, n = w.shape
    assert d == d2 and mu.shape == (p, d) and m % tm == 0 and n % tn == 0
    row = pl.BlockSpec((tm, d), lambda i, q, j: (i, 0))
    return pl.pallas_call(
        _mm_lerp_kernel,
        grid=(m // tm, p, n // tn),
        in_specs=[row, row,
                  pl.BlockSpec((1, 1, d), lambda i, q, j: (q, 0, 0)),
                  pl.BlockSpec((1, d, tn), lambda i, q, j: (q, 0, j))],
        out_specs=pl.BlockSpec((1, tm, tn), lambda i, q, j: (q, i, j)),
        out_shape=jax.ShapeDtypeStruct((p, m, n), out_dtype),
        scratch_shapes=[pltpu.VMEM((tm, d), BF16)],
        name=name,
        compiler_params=pltpu.CompilerParams(
            dimension_semantics=("arbitrary", "arbitrary", "arbitrary"),
            vmem_limit_bytes=VMEM_LIMIT_BYTES),
    )(h, shifted, mu.astype(F32).reshape(p, 1, d), w.astype(BF16))


def _norm_kernel(*refs, residual, modulate, emit_x, in_place):
    refs = list(refs)
    x_ref = refs.pop(0)
    o_ref = refs.pop(0) if residual else None
    pm_ref = refs.pop(0) if residual else None
    g_ref = refs.pop(0)
    m_ref = refs.pop(0) if modulate else None
    if in_place:
        refs.pop(0)
    xo_ref = refs.pop(0) if emit_x else None
    h_ref = refs.pop(0)
    x = x_ref[0].astype(F32)
    if residual:
        x = x + pm_ref[0, 2:3, :] * o_ref[0]
    if emit_x:
        xo_ref[0] = x
    y = x * lax.rsqrt(jnp.mean(x * x, axis=-1, keepdims=True) + RMS_EPS) * g_ref[...]
    if modulate:
        y = y * (1.0 + m_ref[0, 1:2, :]) + m_ref[0, 0:1, :]
    h_ref[0] = y.astype(h_ref.dtype)


def norm_rows(x, gain, *, out, out_row0, out_rows, mod=None, res=None, emit_x=False, tr=256,
              name):
    b, l, d = x.shape
    assert l % tr == 0 and out_row0 % tr == 0
    blk = lambda row0, per_batch=True: pl.BlockSpec(
        (1, tr, d), lambda i, j: (i if per_batch else 0, row0 // tr + j, 0))
    modspec = lambda m: pl.BlockSpec((1, 3, d), (lambda i, j: (i, 0, 0)) if m.shape[0] == b
                                     else (lambda i, j: (0, 0, 0)))
    args, specs = [x], [blk(0)]
    if res is not None:
        o, o_row0, prev_mod = res
        args += [o, prev_mod]
        specs += [blk(o_row0), modspec(prev_mod)]
    args.append(gain.astype(F32).reshape(1, d))
    specs.append(pl.BlockSpec((1, d), lambda i, j: (0, 0)))
    if mod is not None:
        args.append(mod)
        specs.append(modspec(mod))
    out_shapes, out_specs, aliases = [], [], {}
    if emit_x:
        out_shapes.append(jax.ShapeDtypeStruct((b, l, d), F32))
        out_specs.append(blk(0))
    if isinstance(out, jax.Array):
        aliases = {len(args): len(out_shapes)}
        args.append(out)
        specs.append(pl.BlockSpec(memory_space=pl.ANY))
        out_shapes.append(jax.ShapeDtypeStruct(out.shape, out.dtype))
    else:
        out_shapes.append(jax.ShapeDtypeStruct((b, out_rows, d), out))
    out_specs.append(blk(out_row0))
    kern = functools.partial(_norm_kernel, residual=res is not None, modulate=mod is not None,
                             emit_x=emit_x, in_place=bool(aliases))
    outs = pl.pallas_call(
        kern, grid=(b, l // tr), in_specs=specs, out_specs=out_specs, out_shape=out_shapes,
        input_output_aliases=aliases, name=name,
        compiler_params=pltpu.CompilerParams(dimension_semantics=("arbitrary", "arbitrary"),
                                             vmem_limit_bytes=VMEM_LIMIT_BYTES),
    )(*args)
    return (outs[1], outs[0]) if emit_x else (outs[0], None)


def _pick(total, pref):
    t = min(pref, total)
    while total % t:
        t //= 2
    return t


def _s5_prep_kernel(are_ref, aim_ref, ls_ref, bre_ref, bim_ref, cre_ref, cim_ref,
                    w1_ref, ot_ref, ar_ref, ai_ref, *, gb):
    T = S5_T
    GC = S5_GROUP
    lane = lax.broadcasted_iota(jnp.int32, (1, 128), 1)
    fwd = lane < S5_STATE
    s_col = lax.broadcasted_iota(jnp.int32, (T, 1), 0).astype(F32)
    ri = lax.broadcasted_iota(jnp.int32, (T * GC, T * GC), 0) % T
    ci = lax.broadcasted_iota(jnp.int32, (T * GC, T * GC), 1) % T
    lane2 = lax.broadcasted_iota(jnp.int32, (1, 256), 1) % 128
    fwd2 = lane2 < S5_STATE
    half = float(T // 2)

    for g in range(gb):
        a_re = are_ref[g]
        a_im = aim_ref[g]
        dt = jnp.exp(ls_ref[g])
        x_re = dt * a_re
        x_im = dt * a_im

        def lam_pow(p):
            mag = jnp.exp(p * x_re)
            ang = p * x_im
            return mag * jnp.cos(ang), mag * jnp.sin(ang)

        l_re, l_im = lam_pow(jnp.ones((1, 1), F32))
        den = a_re * a_re + a_im * a_im
        nr, ni = l_re - 1.0, l_im
        f_re = (nr * a_re + ni * a_im) / den
        f_im = (ni * a_re - nr * a_im) / den
        b_re = bre_ref[g]
        b_im = bim_ref[g]
        bb_re = f_re * b_re - f_im * b_im
        bb_im = f_re * b_im + f_im * b_re
        c_re = cre_ref[g]
        c_im = cim_ref[g]

        def times_pow(v_re, v_im, p):
            p_re, p_im = lam_pow(p)
            o_re = v_re[:, None, :] * p_re[None, :, :] - v_im[:, None, :] * p_im[None, :, :]
            o_im = v_re[:, None, :] * p_im[None, :, :] + v_im[:, None, :] * p_re[None, :, :]
            return o_re.reshape(GC * T, 128), o_im.reshape(GC * T, 128)

        pb = jnp.where(fwd, half - s_col, s_col - half)
        pc = -pb
        bl_re, bl_im = times_pow(bb_re, bb_im, pb)
        cl_re, cl_im = times_pow(c_re, c_im, pc)
        lhs = jnp.concatenate([bl_re, bl_im], axis=1)
        rhs = jnp.concatenate([cl_re, -cl_im], axis=1)
        r_hi, r_lo = _split2(rhs)

        def dot3(l):
            l_hi, l_lo = _split2(l)
            dn = (((1,), (1,)), ((), ()))
            return (lax.dot_general(l_hi, r_hi, dn, preferred_element_type=F32)
                    + lax.dot_general(l_hi, r_lo, dn, preferred_element_type=F32)
                    + lax.dot_general(l_lo, r_hi, dn, preferred_element_type=F32))

        t_f = dot3(jnp.where(fwd2, lhs, 0.0))
        t_b = dot3(jnp.where(fwd2, 0.0, lhs))
        toep = jnp.where(ci >= ri, t_f, 0.0) + jnp.where(ri >= ci, t_b, 0.0)

        ps = jnp.where(fwd, (T - 1.0) - s_col, s_col)
        s_re, s_im = times_pow(bb_re, bb_im, ps)
        w1_ref[g] = jnp.concatenate([toep, s_re, s_im], axis=1).astype(w1_ref.dtype)

        po = jnp.where(fwd, s_col + 1.0, T - s_col)
        o_re, o_im = times_pow(c_re, c_im, po)
        ot_ref[g] = jnp.concatenate([o_re, -o_im], axis=1).astype(ot_ref.dtype)

        pw_re, pw_im = lam_pow(jnp.full((1, 1), float(T), F32))
        ar_ref[g] = pw_re
        ai_ref[g] = pw_im


def _s5_kernel(u_ref, w1_ref, ot_ref, ar_ref, ai_ref, y_ref,
               p_ref, hf_re, hf_im, hb_re, hb_im, *, nb, n_ctx, n_chunks):
    gs = u_ref.shape[0]
    for g in range(gs):
        p_ref[g] = _dot(u_ref[g], w1_ref[g])
    ar = jnp.broadcast_to(ar_ref[...], (gs, nb, 128))
    ai = jnp.broadcast_to(ai_ref[...], (gs, nb, 128))
    fwd = lax.broadcasted_iota(jnp.int32, (gs, nb, 128), 2) < S5_STATE

    def step(i, carry):
        h_re, h_im = carry
        cb = jnp.where(i < n_ctx, n_ctx - 1 - i, n_chunks - 1 - (i - n_ctx))
        rf = pl.ds(pl.multiple_of(i * nb, nb), nb)
        rb = pl.ds(pl.multiple_of(cb * nb, nb), nb)
        hf_re[:, rf, :] = h_re
        hf_im[:, rf, :] = h_im
        hb_re[:, rb, :] = h_re
        hb_im[:, rb, :] = h_im
        x_re = jnp.where(fwd, p_ref[:, rf, 256:384], p_ref[:, rb, 256:384])
        x_im = jnp.where(fwd, p_ref[:, rf, 384:512], p_ref[:, rb, 384:512])
        n_re = ar * h_re - ai * h_im + x_re
        n_im = ar * h_im + ai * h_re + x_im
        return n_re, n_im

    zero = jnp.zeros((gs, nb, 128), F32)
    lax.fori_loop(0, n_chunks, step, (zero, zero))

    rows = p_ref.shape[1]
    fwd_all = lax.broadcasted_iota(jnp.int32, (rows, 128), 1) < S5_STATE
    for g in range(gs):
        h_all = jnp.concatenate([jnp.where(fwd_all, hf_re[g], hb_re[g]),
                                 jnp.where(fwd_all, hf_im[g], hb_im[g])], axis=1)
        y = p_ref[g, :, 0:256] + _dot_nt(h_all, ot_ref[g])
        y_ref[g] = y.astype(y_ref.dtype)


def s5_operators(a_re, a_im, log_step, b_re, b_im, c_re, c_im):
    n_dir, g, n = a_re.shape
    gb = 8

    def lanes(x):
        return jnp.concatenate([x[0], x[1]], axis=-1).astype(F32)

    are = lanes(a_re).reshape(g, 1, 2 * n)
    aim = lanes(a_im).reshape(g, 1, 2 * n)
    ls = lanes(jnp.broadcast_to(log_step[:, :, None], (n_dir, g, n))).reshape(g, 1, 2 * n)
    bre = lanes(jnp.swapaxes(b_re, -1, -2))
    bim = lanes(jnp.swapaxes(b_im, -1, -2))
    cre = lanes(c_re)
    cim = lanes(c_im)
    row = pl.BlockSpec((gb, 1, 2 * n), lambda i: (i, 0, 0))
    mat = pl.BlockSpec((gb, S5_GROUP, 2 * n), lambda i: (i, 0, 0))
    tg = S5_T * S5_GROUP
    return pl.pallas_call(
        functools.partial(_s5_prep_kernel, gb=gb),
        grid=(g // gb,),
        in_specs=[row, row, row, mat, mat, mat, mat],
        out_specs=[pl.BlockSpec((gb, tg, tg + 4 * n), lambda i: (i, 0, 0)),
                   pl.BlockSpec((gb, tg, 4 * n), lambda i: (i, 0, 0)),
                   row, row],
        out_shape=[jax.ShapeDtypeStruct((g, tg, tg + 4 * n), BF16),
                   jax.ShapeDtypeStruct((g, tg, 4 * n), BF16),
                   jax.ShapeDtypeStruct((g, 1, 2 * n), F32),
                   jax.ShapeDtypeStruct((g, 1, 2 * n), F32)],
        name="s5_operators",
        compiler_params=pltpu.CompilerParams(dimension_semantics=("arbitrary",),
                                             vmem_limit_bytes=VMEM_LIMIT_BYTES),
    )(are, aim, ls, bre, bim, cre, cim)


def s5_scan(u_g, w1, ot, ar, ai, *, nb, n_ctx, n_chunks):
    g, rows, tg = u_g.shape
    assert rows == n_chunks * nb and tg == S5_T * S5_GROUP
    gs = _pick(g, 4)
    blk = lambda w: pl.BlockSpec((gs,) + w, lambda i: (i, 0, 0))
    return pl.pallas_call(
        functools.partial(_s5_kernel, nb=nb, n_ctx=n_ctx, n_chunks=n_chunks),
        grid=(g // gs,),
        in_specs=[blk((rows, tg)), blk(w1.shape[1:]), blk(ot.shape[1:]),
                  blk((1, 128)), blk((1, 128))],
        out_specs=blk((rows, tg)),
        out_shape=jax.ShapeDtypeStruct((g, rows, tg), F32),
        scratch_shapes=[pltpu.VMEM((gs, rows, w1.shape[2]), F32)]
                       + [pltpu.VMEM((gs, rows, 128), F32)] * 4,
        name="s5_scan",
        compiler_params=pltpu.CompilerParams(dimension_semantics=("arbitrary",),
                                             vmem_limit_bytes=VMEM_LIMIT_BYTES),
    )(u_g, w1, ot, ar, ai)


def _eye(n):
    r = lax.broadcasted_iota(jnp.int32, (n, n), 0)
    c = lax.broadcasted_iota(jnp.int32, (n, n), 1)
    return jnp.where(r == c, 1.0, 0.0).astype(BF16)


def _to_groups_kernel(u_ref, o_ref, *, nb, n_j):
    g = o_ref.shape[0]
    eye = _eye(g)
    for b in range(nb):
        x = u_ref[b]
        rows = jnp.concatenate([x[:, j * g:(j + 1) * g] for j in range(n_j)], axis=0)
        o_ref[:, b, :] = _dot_nt(eye, rows).astype(o_ref.dtype)


def _from_groups_kernel(y_ref, o_ref, *, nb, n_j):
    g = y_ref.shape[0]
    t = o_ref.shape[1]
    eye = _eye(y_ref.shape[2]).astype(F32)
    for b in range(nb):
        cols = lax.dot_general(eye, y_ref[:, b, :], (((1,), (1,)), ((), ())),
                               preferred_element_type=F32)
        for i in range(n_j):
            o_ref[b, :, i * g:(i + 1) * g] = cols[i * t:(i + 1) * t, :]


def to_groups(uz, *, n_chunks):
    b, l, e2 = uz.shape
    e = e2 // 2
    g = e // S5_GROUP
    return pl.pallas_call(
        functools.partial(_to_groups_kernel, nb=b, n_j=S5_GROUP),
        grid=(n_chunks,),
        in_specs=[pl.BlockSpec((b, S5_T, e), lambda c: (0, c, 0))],
        out_specs=pl.BlockSpec((g, b, S5_T * S5_GROUP), lambda c: (0, c, 0)),
        out_shape=jax.ShapeDtypeStruct((g, n_chunks * b, S5_T * S5_GROUP), F32),
        name="s5_to_groups",
        compiler_params=pltpu.CompilerParams(dimension_semantics=("arbitrary",),
                                             vmem_limit_bytes=VMEM_LIMIT_BYTES),
    )(uz)


def from_groups(y_g, *, nb, n_chunks):
    g, rows, tg = y_g.shape
    e = g * S5_GROUP
    return pl.pallas_call(
        functools.partial(_from_groups_kernel, nb=nb, n_j=S5_GROUP),
        grid=(n_chunks,),
        in_specs=[pl.BlockSpec((g, nb, tg), lambda c: (0, c, 0))],
        out_specs=pl.BlockSpec((nb, S5_T, e), lambda c: (0, c, 0)),
        out_shape=jax.ShapeDtypeStruct((nb, n_chunks * S5_T, e), F32),
        name="s5_from_groups",
        compiler_params=pltpu.CompilerParams(dimension_semantics=("arbitrary",),
                                             vmem_limit_bytes=VMEM_LIMIT_BYTES),
    )(y_g)


def _rwkv_chunk_index(item, n_ctx, n_chunks):
    i = item % n_chunks
    back = jnp.where(i < n_ctx, n_ctx - 1 - i, n_chunks - 1 - (i - n_ctx))
    return jnp.where(item < n_chunks, i, back)


_HB_AC, _HB_RC, _HB_BH, _HB_KH, _HB_V, _HB_RB, _HB_RK, _HB_DM = range(8)
_HF_AKV, _HF_AO, _HF_Q, _HF_ZS = range(4)


def _rwkv_step(r_ref, k_ref, v_ref, z_ref, tw_ref, ta_ref, w2_ref, a2_ref, w0_ref, a0_ref,
               kk_ref, ka_ref, rk_ref, lnw_ref, lnb_ref, o_ref, s_ref, acc_ref,
               hbw, hfw, hww, hbr, hfr, hwr, *, n_ctx, n_chunks, n_pairs):
    C = RWKV_CHUNK
    n_items = 2 * n_chunks
    step = pl.program_id(2)
    pairs = range(n_pairs)

    def each(f, *lists):
        return [f(*args) for args in zip(*lists)]

    def cols(x):
        return [x[:, p * 128:(p + 1) * 128] for p in pairs]

    def join(xs):
        return jnp.concatenate(xs, axis=1)

    qa = jnp.minimum(step, n_items - 1)
    a_fwd = qa < n_chunks
    a_cidx = _rwkv_chunk_index(qa, n_ctx, n_chunks)
    d = jnp.where(a_fwd, 0, 1)
    sgn = jnp.where(a_fwd, 1, -1)
    qb = jnp.maximum(step - 1, 0)
    b_fwd = qb < n_chunks
    b_cidx = _rwkv_chunk_index(qb, n_ctx, n_chunks)
    b_live = step >= 1

    t_i = lax.broadcasted_iota(jnp.int32, (C, 128), 0)
    lane = lax.broadcasted_iota(jnp.int32, (C, 128), 1)
    s_i = lane % C
    lane_head = lane // RWKV_HEAD
    before = (s_i - t_i) * sgn
    strict = before < 0
    incl = before <= 0
    same_blk = (t_i // 16) == (s_i // 16)
    diag_m = strict & same_blk
    off_m = strict & jnp.logical_not(same_blk)
    tr = lax.broadcasted_iota(jnp.int32, (C, C), 0)
    tc = lax.broadcasted_iota(jnp.int32, (C, C), 1)
    cum_m = jnp.where((tc - tr) * sgn <= 0, 1.0, 0.0).astype(BF16)
    ri = lax.broadcasted_iota(jnp.int32, (128, 128), 0)
    ci = lax.broadcasted_iota(jnp.int32, (128, 128), 1)
    head_ones = jnp.where((ri // RWKV_HEAD) == (ci // RWKV_HEAD), 1.0, 0.0).astype(BF16)

    @pl.when(qb % n_chunks == 0)
    def _():
        s_ref[...] = jnp.zeros(s_ref.shape, F32)

    def stack(x):
        x = x.astype(BF16)
        zero = jnp.zeros_like(x)
        return jnp.concatenate([jnp.where(lane_head == 0, x, zero),
                                jnp.where(lane_head == 1, x, zero)], axis=0)

    def pack(x):
        return jnp.where(lane_head == 0, x[:C], x[C:])

    def head_sum(x):
        tot = _dot(jnp.concatenate(cols(x), axis=0), head_ones)
        return join([tot[p * C:(p + 1) * C] for p in pairs])

    mm = lambda l, r_: jnp.dot(l.astype(BF16), r_, preferred_element_type=F32)
    wide = lambda l, *rs: mm(l, jnp.concatenate(rs, axis=1))

    r = r_ref[0, 0].astype(F32)
    k = k_ref[0, 0].astype(F32)
    v = v_ref[0, 0].astype(F32)
    wp = w0_ref[pl.ds(d, 1), :] + _dot(jnp.tanh(tw_ref[0, 0]), w2_ref[d])
    ap = a0_ref[pl.ds(d, 1), :] + _dot(ta_ref[d, 0], a2_ref[d])
    sp = jnp.maximum(-wp, 0.0) + jnp.log(1.0 + jnp.exp(-jnp.abs(wp)))
    lw = -jnp.exp(-sp - 0.5)
    a_rate = 1.0 / (1.0 + jnp.exp(-ap))
    k_a = ka_ref[...]
    kkr = k * kk_ref[...]
    kd = k * (1.0 + (a_rate - 1.0) * k_a)
    lc = sum(jnp.dot(cum_m, p, preferred_element_type=F32) for p in _split3(lw))
    l_mid = lc[C // 2:C // 2 + 1, :]
    l_end = jnp.where(a_fwd, lc[C - 1:C, :], lc[0:1, :])
    dl = lc - l_mid
    e_in = jnp.exp(dl)
    e_ex = jnp.exp(dl - lw)
    e_iv = jnp.exp(-dl)
    c_row = jnp.exp(l_mid)
    e2 = jnp.exp(l_end - l_mid)
    kk_all = kkr / jnp.maximum(jnp.sqrt(head_sum(kkr * kkr)), NORM_EPS)
    at_all = -kk_all * e_ex
    bt_all = kk_all * (a_rate * e_iv)
    rt_all = r * e_in
    kt_all = kd * e_iv
    hww[...] = jnp.exp(l_end)
    hbw[_HB_AC] = (at_all * c_row).astype(BF16)
    hbw[_HB_RC] = (rt_all * c_row).astype(BF16)
    hbw[_HB_BH] = (bt_all * e2).astype(BF16)
    hbw[_HB_KH] = (kt_all * e2).astype(BF16)
    hbw[_HB_V] = v_ref[0, 0]
    at, bt, rp, kp = cols(at_all), cols(bt_all), cols(rt_all), cols(kt_all)
    vs = each(stack, cols(v))

    s_old = [s_ref[p] for p in pairs]
    asr = each(lambda a, r_, s: _dot_nt(jnp.concatenate([a, r_], axis=0), stack(s)),
               cols(hbr[_HB_AC]), cols(hbr[_HB_RC]), s_old)
    a4 = each(lambda a, r_, b, k_: _dot_nt(jnp.concatenate([a, r_], axis=0),
                                           jnp.concatenate([stack(b), stack(k_)], axis=0)),
              at, rp, bt, kp)
    ak = each(lambda m: jnp.where(strict, m[:C, 128:], 0.0), a4)
    hbw[_HB_RB] = join(each(lambda m: jnp.where(incl, m[C:, :128], 0.0), a4)).astype(BF16)
    hbw[_HB_RK] = join(each(lambda m: jnp.where(incl, m[C:, 128:], 0.0), a4)).astype(BF16)
    ad = each(lambda m: jnp.where(diag_m, m[:C, :128], 0.0), a4)
    ao = each(lambda m: jnp.where(off_m, m[:C, :128], 0.0), a4)
    hfw[_HF_AO] = join(ao)

    x0 = each(lambda m, q: m[:C] + q, asr, cols(hfr[_HF_AKV]))
    ao_b = cols(hfr[_HF_AO])
    dx = each(lambda d_, o, x: wide(d_, stack(o), stack(x)),
              cols(hbr[_HB_DM]), ao_b, x0)
    ad_s = each(stack, ad)
    hfw[_HF_AKV] = join(each(mm, ak, vs))
    p2 = each(mm, ad, ad_s)

    n1 = each(lambda o, m: o + m[:, :128], ao_b, dx)
    xa = each(lambda x, m: x + m[:, 128:], x0, dx)
    xc = each(lambda n_, x: x + mm(n_, stack(x)), n1, xa)
    p43 = each(lambda p, a_s: wide(p, stack(p), a_s), p2, ad_s)
    p4 = [m[:, :128] for m in p43]
    s3 = each(lambda a, p, m: a + p + m[:, 128:], ad, p2, p43)

    w1_ = each(lambda n_, x: mm(n_, stack(x)), n1, xc)
    p8s = each(lambda p, s: wide(p, stack(p), stack(s)), p4, s3)
    p8 = [m[:, :128] for m in p8s]
    s7 = each(lambda s, p, m: s + p + m[:, 128:], s3, p4, p8s)

    u = each(lambda n_, x, w_: x + mm(n_, stack(w_)), n1, xc, w1_)
    hbw[_HB_DM] = join(each(lambda s, p: s + p + mm(p, stack(s)), s7, p8)).astype(BF16)

    v_b = cols(hbr[_HB_V])
    ys = each(lambda m, rb_, rk_, u_, v_: m[C:] + mm(
        jnp.concatenate([rb_, rk_], axis=1), jnp.concatenate([stack(u_), stack(v_)], axis=0)),
        asr, cols(hbr[_HB_RB]), cols(hbr[_HB_RK]), u, v_b)
    upd = each(lambda u_, v_, b, k_: pack(_dot_tn(
        jnp.concatenate([u_, v_], axis=0), jnp.concatenate([b, k_], axis=0))),
        u, v_b, cols(hbr[_HB_BH]), cols(hbr[_HB_KH]))
    wc_b = hwr[...]
    for p in pairs:
        s_ref[p] = s_old[p] * wc_b[:, p * 128:(p + 1) * 128] + upd[p]

    y = join(ys)
    a_lat = a_cidx >= n_ctx
    b_lat = jnp.logical_and(b_cidx >= n_ctx, b_live)
    orow = pl.multiple_of(jnp.maximum(b_cidx - n_ctx, 0) * C, C)

    @pl.when(jnp.logical_and(a_lat, jnp.logical_not(a_fwd)))
    def _():
        ar_f = 1.0 / (1.0 + jnp.exp(-(a0_ref[0:1, :] + _dot(ta_ref[0, 0], a2_ref[0]))))
        k_sum = k * (2.0 + (ar_f + a_rate - 2.0) * k_a)
        hfw[_HF_Q] = r * k_sum * rk_ref[...]
        zg = z_ref[0, 0].astype(F32)
        hfw[_HF_ZS] = zg / (1.0 + jnp.exp(-zg))

    @pl.when(jnp.logical_and(b_lat, b_fwd))
    def _():
        acc_ref[pl.ds(orow, C), :] = y

    @pl.when(jnp.logical_and(b_lat, jnp.logical_not(b_fwd)))
    def _():
        y_sum = acc_ref[pl.ds(orow, C), :] + y
        inv = 1.0 / RWKV_HEAD
        dev = y_sum - head_sum(y_sum) * inv
        yn = dev * lax.rsqrt(head_sum(dev * dev) * inv + GN_EPS)
        yn = yn * lnw_ref[...] + lnb_ref[...]
        bonus = head_sum(hfr[_HF_Q]) * hbr[_HB_V].astype(F32)
        o_ref[0] = ((yn + bonus) * hfr[_HF_ZS]).astype(o_ref.dtype)


def _rwkv_kernel(*refs, n_ctx, n_chunks, n_pairs):
    *io, s_ref, acc_ref, hb0, hb1, hf0, hf1, hw0, hw1 = refs
    step = pl.program_id(2)
    run = functools.partial(_rwkv_step, *io, s_ref, acc_ref,
                            n_ctx=n_ctx, n_chunks=n_chunks, n_pairs=n_pairs)

    @pl.when(step == 0)
    def _():
        hb1[...] = jnp.zeros(hb1.shape, hb1.dtype)
        hf1[...] = jnp.zeros(hf1.shape, hf1.dtype)
        hw1[...] = jnp.zeros(hw1.shape, hw1.dtype)

    @pl.when(step % 2 == 0)
    def _():
        run(hb0, hf0, hw0, hb1, hf1, hw1)

    @pl.when(step % 2 == 1)
    def _():
        run(hb1, hf1, hw1, hb0, hf0, hw0)


def rwkv_scan(rkvz, tw, ta, w2, a2, w0, a0, k_k, k_a, r_k, ln_w, ln_b, *, n_ctx_tokens):
    _, b, l, e = rkvz.shape
    rank = tw.shape[-1]
    C = RWKV_CHUNK
    pw = min(e, 2048)
    n_pairs = pw // 128
    n_chunks = l // C
    n_ctx = n_ctx_tokens // C
    n_items = 2 * n_chunks
    l_out = l - n_ctx_tokens
    cix = functools.partial(_rwkv_chunk_index, n_ctx=n_ctx, n_chunks=n_chunks)
    item_a = lambda t: jnp.minimum(t, n_items - 1)
    item_b = lambda t: jnp.maximum(t - 1, 0)

    def out_chunk(t):
        q = item_b(t)
        c = cix(q)
        return jnp.where(jnp.logical_and(q >= n_chunks, c >= n_ctx), c, n_chunks - 1) - n_ctx

    seq = [pl.BlockSpec((1, 1, C, pw), functools.partial(
        lambda i, j, t, p: (p, i, cix(item_a(t)), j), p=p)) for p in range(4)]
    low_d = pl.BlockSpec((1, 1, C, rank),
                         lambda i, j, t: (item_a(t) // n_chunks, i, cix(item_a(t)), 0))
    low_2 = pl.BlockSpec((2, 1, C, rank), lambda i, j, t: (0, i, cix(item_a(t)), 0))
    up = pl.BlockSpec((2, rank, pw), lambda i, j, t: (0, 0, j))
    vec2 = pl.BlockSpec((2, pw), lambda i, j, t: (0, j))
    vec = pl.BlockSpec((1, pw), lambda i, j, t: (0, j))
    row = lambda x: x.astype(F32).reshape(1, e)
    return pl.pallas_call(
        functools.partial(_rwkv_kernel, n_ctx=n_ctx, n_chunks=n_chunks, n_pairs=n_pairs),
        grid=(b, e // pw, n_items + 1),
        in_specs=seq + [low_d, low_2, up, up, vec2, vec2, vec, vec, vec, vec, vec],
        out_specs=pl.BlockSpec((1, C, pw), lambda i, j, t: (i, out_chunk(t), j)),
        out_shape=jax.ShapeDtypeStruct((b, l_out, e), BF16),
        scratch_shapes=[pltpu.VMEM((n_pairs, C, 128), F32),
                        pltpu.VMEM((l_out, pw), F32),
                        pltpu.VMEM((8, C, pw), BF16), pltpu.VMEM((8, C, pw), BF16),
                        pltpu.VMEM((4, C, pw), F32), pltpu.VMEM((4, C, pw), F32),
                        pltpu.VMEM((1, pw), F32), pltpu.VMEM((1, pw), F32)],
        name="rwkv_scan",
        compiler_params=pltpu.CompilerParams(
            dimension_semantics=("arbitrary", "arbitrary", "arbitrary"),
            vmem_limit_bytes=VMEM_LIMIT_BYTES),
    )(rkvz, rkvz, rkvz, rkvz, tw, ta, w2.astype(BF16), a2.astype(BF16), w0.astype(F32),
      a0.astype(F32),
      row(k_k), row(k_a), row(r_k), row(ln_w), row(ln_b))


def _modulation(c, c_ctx, ada_w, ada_b):
    d = c.shape[-1]
    cond = jnp.concatenate([c, c_ctx[None]], axis=0)
    rows = cond.shape[0]
    pad = (-rows) % 16
    act = jnp.pad(jax.nn.silu(cond), ((0, pad), (0, 0)))
    m = matmul(act, ada_w, ada_b, out_dtype=F32, tm=rows + pad, tn=_pick(3 * d, 1024),
               name="mm_modulation")
    m = m[:rows].reshape(rows, 3, d)
    return m[:rows - 1], m[rows - 1:]


def _q_shift(h):
    b, l, d = h.shape
    rows = l // GRID_W
    g = h.reshape(b, rows, GRID_W, d)
    q = d // 4
    left = jnp.pad(g[:, :, :-1, :q], ((0, 0), (0, 0), (1, 0), (0, 0)))
    right = jnp.pad(g[:, :, 1:, q:2 * q], ((0, 0), (0, 0), (0, 1), (0, 0)))
    up = jnp.pad(g[:, :-1, :, 2 * q:3 * q], ((0, 0), (1, 0), (0, 0), (0, 0)))
    down = jnp.pad(g[:, 1:, :, 3 * q:], ((0, 0), (0, 1), (0, 0), (0, 0)))
    return jnp.concatenate([left, right, up, down], axis=-1).reshape(b, l, d)


def _seq_shift(h):
    half = h.shape[-1] // 2
    prev = jnp.pad(h[:, :-1, :half], ((0, 0), (1, 0), (0, 0)))
    nxt = jnp.pad(h[:, 1:, half:], ((0, 0), (0, 1), (0, 0)))
    return jnp.concatenate([prev, nxt], axis=-1)


def _s5_layer(x_lat, x_ctx, c, c_ctx, norm_g, ada_w, ada_b, in_w, a_re, a_im, log_step,
              b_re, b_im, c_re, c_im, d_skip, glu_w, glu_b, out_w):
    b, l_lat, d = x_lat.shape
    l_ctx = x_ctx.shape[1]
    l = l_ctx + l_lat
    e = in_w.shape[1] // 2
    g = e // S5_GROUP
    mod_l, mod_c = _modulation(c, c_ctx, ada_w, ada_b)
    tr = math.gcd(l_ctx, l_lat, 256)
    h, _ = norm_rows(x_lat, norm_g, out=BF16, out_row0=l_ctx, out_rows=l, mod=mod_l, tr=tr,
                     name="norm0_lat")
    h, _ = norm_rows(x_ctx, norm_g, out=h, out_row0=0, out_rows=l, mod=mod_c, tr=tr,
                     name="norm0_ctx")
    h = h.reshape(b * l, d)
    m = b * l
    tm = _pick(m, 1024)
    te = _pick(e, 1024)

    lane = jnp.arange(e)
    old = (lane % g) * S5_GROUP + lane // g
    perm = (lane[:, None] == old[None, :]).astype(BF16)
    in_w_p = matmul(in_w.reshape(2 * d, e), perm, out_dtype=BF16, tm=_pick(2 * d, 1024), tn=te,
                    name="mm_perm_in").reshape(d, 2 * e)
    rows_p = lambda w: (w.astype(BF16).reshape(g, S5_GROUP, w.shape[1]).transpose(1, 0, 2)
                        .reshape(w.shape))
    glu_w_p = matmul(rows_p(glu_w), perm, out_dtype=BF16, tm=te, tn=te, name="mm_perm_glu")
    out_w_p = rows_p(out_w)
    d_p = d_skip.astype(F32)[old]
    glu_b_p = glu_b[old]

    uz = matmul(h, in_w_p, out_dtype=BF16, tm=tm, tn=_pick(2 * e, 1024), name="mm_s5_in")
    u = uz[:, :e]

    n_chunks = l // S5_T
    u_g = to_groups(uz.reshape(b, l, 2 * e), n_chunks=n_chunks)
    w1, ot, ar, ai = s5_operators(a_re, a_im, log_step, b_re, b_im, c_re, c_im)
    y_g = s5_scan(u_g, w1, ot, ar, ai, nb=b, n_ctx=l_ctx // S5_T, n_chunks=n_chunks)
    y_ssm = from_groups(y_g, nb=b, n_chunks=n_chunks).reshape(m, e)

    y1 = jax.nn.gelu(y_ssm.astype(F32) + d_p * u.astype(F32)).astype(BF16)
    y2 = matmul_glu(y1, glu_w_p, glu_b_p, uz, tm=tm, tn=te, name="mm_s5_glu")
    o = matmul(y2, out_w_p, out_dtype=F32, tm=tm, tn=_pick(d, 1024),
               name="mm_s5_out").reshape(b, l, d)
    return o, mod_l, mod_c


def _rwkv_layer(x_lat, x_ctx, prev, c, c_ctx, norm_g, ada_w, ada_b, mu, in_w, w0, w1, w2, a0, a1,
                a2, k_k, k_a, r_k, ln_w, ln_b, out_w, final_g):
    b, l_lat, d = x_lat.shape
    l_ctx = x_ctx.shape[1]
    l = l_ctx + l_lat
    e = in_w.shape[2]
    m = b * l
    o_prev, pm_l, pm_c = prev
    mod_l, mod_c = _modulation(c, c_ctx, ada_w, ada_b)
    tr = math.gcd(l_ctx, l_lat, 256)
    h3, x_lat = norm_rows(x_lat, norm_g, out=BF16, out_row0=l_ctx, out_rows=l, mod=mod_l,
                          res=(o_prev, l_ctx, pm_l), emit_x=True, tr=tr, name="norm1_lat")
    h3, _ = norm_rows(x_ctx, norm_g, out=h3, out_row0=0, out_rows=l, mod=mod_c,
                      res=(o_prev, 0, pm_c), tr=tr, name="norm1_ctx")
    h = h3.reshape(m, d)
    shifted = jnp.concatenate([_seq_shift(h3[:, :l_ctx]), _q_shift(h3[:, l_ctx:])],
                              axis=1).reshape(m, d)

    tm = _pick(m, 1024)
    tn = _pick(e, 1024)
    rkvz = matmul_lerp(h, shifted, mu[:4], in_w, out_dtype=BF16, tm=tm, tn=tn,
                       name="mm_rwkv_in").reshape(4, b, l, e)

    n_dir, _, lora = w1.shape
    lp = (-lora) % 128
    down = jnp.stack([jnp.concatenate([jnp.pad(l1[i], ((0, 0), (0, lp))) for i in range(n_dir)],
                                      axis=1) for l1 in (w1, a1)])
    low = matmul_lerp(h, shifted, mu[4:6], down, out_dtype=F32, tm=tm, tn=down.shape[2],
                      name="mm_lora_down")
    tw, ta = (low[i].reshape(b, l, n_dir, lora + lp).transpose(2, 0, 1, 3) for i in range(2))

    pad_up = lambda l2: jnp.pad(l2, ((0, 0), (0, lp), (0, 0)))
    yg = rwkv_scan(rkvz, tw, ta, pad_up(w2), pad_up(a2), w0, a0, k_k, k_a, r_k.reshape(e),
                   ln_w, ln_b, n_ctx_tokens=l_ctx)
    o = matmul(yg.reshape(b * l_lat, e), out_w, out_dtype=F32,
               tm=_pick(b * l_lat, 1024), tn=_pick(d, 1024), name="mm_rwkv_out").reshape(b, l_lat, d)
    out, _ = norm_rows(x_lat, final_g, out=F32, out_row0=0, out_rows=l_lat, res=(o, 0, mod_l),
                       tr=tr, name="norm_final")
    return out


def kernel(x, c, ctx, c_ctx, l0_norm_g, l0_ada_w, l0_ada_b, l0_in_w, l0_a_re, l0_a_im, l0_log_step, l0_b_re, l0_b_im, l0_c_re, l0_c_im, l0_d, l0_glu_w, l0_glu_b, l0_out_w, l1_norm_g, l1_ada_w, l1_ada_b, l1_mu, l1_in_w, l1_w0, l1_w1, l1_w2, l1_a0, l1_a1, l1_a2, l1_k_k, l1_k_a, l1_r_k, l1_ln_w, l1_ln_b, l1_out_w, final_norm_g):
    prev = _s5_layer(x, ctx, c, c_ctx, l0_norm_g, l0_ada_w, l0_ada_b, l0_in_w, l0_a_re,
                     l0_a_im, l0_log_step, l0_b_re, l0_b_im, l0_c_re, l0_c_im, l0_d,
                     l0_glu_w, l0_glu_b, l0_out_w)
    out = _rwkv_layer(x, ctx, prev, c, c_ctx, l1_norm_g, l1_ada_w, l1_ada_b, l1_mu, l1_in_w,
                      l1_w0, l1_w1, l1_w2, l1_a0, l1_a1, l1_a2, l1_k_k, l1_k_a, l1_r_k,
                      l1_ln_w, l1_ln_b, l1_out_w, final_norm_g)
    return out.astype(x.dtype)
```

```python
import functools
import math

import jax
import jax.numpy as jnp
from jax import lax
from jax.experimental import pallas as pl
from jax.experimental.pallas import tpu as pltpu

F32 = jnp.float32
BF16 = jnp.bfloat16

GRID_W = 64
S5_GROUP = 16
S5_STATE = 64
S5_T = 16
RWKV_HEAD = 64
RWKV_CHUNK = 64
RWKV_BLOCK = 16
assert RWKV_BLOCK == 16 and RWKV_CHUNK == 4 * RWKV_BLOCK
RMS_EPS = 1e-6
GN_EPS = 64e-5
NORM_EPS = 1e-12

VMEM_LIMIT_BYTES = 56 * 1024 * 1024


def _dot(a, b):
    return jnp.dot(a.astype(BF16), b.astype(BF16), preferred_element_type=F32)


def _dot_nt(a, b):
    return lax.dot_general(a.astype(BF16), b.astype(BF16), (((1,), (1,)), ((), ())),
                           preferred_element_type=F32)


def _dot_tn(a, b):
    return lax.dot_general(a.astype(BF16), b.astype(BF16), (((0,), (0,)), ((), ())),
                           preferred_element_type=F32)


def _split2(x):
    hi = x.astype(BF16)
    lo = (x - hi.astype(F32)).astype(BF16)
    return hi, lo


def _split3(x):
    p1 = x.astype(BF16)
    r1 = x - p1.astype(F32)
    p2 = r1.astype(BF16)
    p3 = (r1 - p2.astype(F32)).astype(BF16)
    return p1, p2, p3


def _mm_kernel(a_ref, b_ref, o_ref):
    o_ref[...] = jnp.dot(a_ref[...], b_ref[...],
                         preferred_element_type=F32).astype(o_ref.dtype)


def _mm_bias_kernel(a_ref, b_ref, bias_ref, o_ref):
    acc = jnp.dot(a_ref[...], b_ref[...], preferred_element_type=F32)
    o_ref[...] = (acc + bias_ref[...]).astype(o_ref.dtype)


def matmul(a, b, bias=None, *, out_dtype, tm, tn, name):
    m, kg = a.shape
    k, n = b.shape
    groups = kg // k
    assert kg == groups * k and m % tm == 0 and n % tn == 0, (a.shape, b.shape, tm, tn)
    assert bias is None or groups == 1
    nt = n // tn
    a = a.astype(BF16)
    b = b.astype(BF16)
    in_specs = [pl.BlockSpec((tm, k), lambda j, i: (i, j // nt)),
                pl.BlockSpec((k, tn), lambda j, i: (0, j % nt))]
    args = [a, b]
    kern = _mm_kernel
    if bias is not None:
        in_specs.append(pl.BlockSpec((1, tn), lambda j, i: (0, j)))
        args.append(bias.astype(F32).reshape(1, n))
        kern = _mm_bias_kernel
    return pl.pallas_call(
        kern,
        grid=(groups * nt, m // tm),
        in_specs=in_specs,
        out_specs=pl.BlockSpec((tm, tn), lambda j, i: (i, j)),
        out_shape=jax.ShapeDtypeStruct((m, groups * n), out_dtype),
        name=name,
        compiler_params=pltpu.CompilerParams(
            dimension_semantics=("arbitrary", "arbitrary"),
            vmem_limit_bytes=VMEM_LIMIT_BYTES),
    )(*args)


def _mm_glu_kernel(a_ref, b_ref, bias_ref, z_ref, o_ref, *, tn):
    j = pl.program_id(0)
    acc = jnp.dot(a_ref[...], b_ref[...], preferred_element_type=F32) + bias_ref[...]
    y = a_ref[:, pl.ds(pl.multiple_of(j * tn, tn), tn)].astype(F32)
    z = z_ref[...].astype(F32)
    o_ref[...] = (y / (1.0 + jnp.exp(-acc)) * (z / (1.0 + jnp.exp(-z)))).astype(o_ref.dtype)


def matmul_glu(y, w, bias, uz, *, tm, tn, name):
    m, e = y.shape
    assert w.shape == (e, e) and uz.shape == (m, 2 * e) and m % tm == 0 and e % tn == 0
    n_tiles = e // tn
    return pl.pallas_call(
        functools.partial(_mm_glu_kernel, tn=tn),
        grid=(n_tiles, m // tm),
        in_specs=[pl.BlockSpec((tm, e), lambda j, i: (i, 0)),
                  pl.BlockSpec((e, tn), lambda j, i: (0, j)),
                  pl.BlockSpec((1, tn), lambda j, i: (0, j)),
                  pl.BlockSpec((tm, tn), lambda j, i: (i, j + n_tiles))],
        out_specs=pl.BlockSpec((tm, tn), lambda j, i: (i, j)),
        out_shape=jax.ShapeDtypeStruct((m, e), BF16),
        name=name,
        compiler_params=pltpu.CompilerParams(
            dimension_semantics=("arbitrary", "arbitrary"),
            vmem_limit_bytes=VMEM_LIMIT_BYTES),
    )(y.astype(BF16), w.astype(BF16), bias.astype(F32).reshape(1, e), uz)


def _mm_lerp_kernel(h_ref, s_ref, mu_ref, w_ref, o_ref, lhs_ref):
    @pl.when(pl.program_id(2) == 0)
    def _():
        h = h_ref[...].astype(F32)
        lhs_ref[...] = (h + (s_ref[...].astype(F32) - h) * mu_ref[0]).astype(lhs_ref.dtype)

    o_ref[0] = jnp.dot(lhs_ref[...], w_ref[0], preferred_element_type=F32).astype(o_ref.dtype)


def matmul_lerp(h, shifted, mu, w, *, out_dtype, tm, tn, name):
    m, d = h.shape
    p, d2, n = w.shape
    assert d == d2 and mu.shape == (p, d) and m % tm == 0 and n % tn == 0
    row = pl.BlockSpec((tm, d), lambda i, q, j: (i, 0))
    return pl.pallas_call(
        _mm_lerp_kernel,
        grid=(m // tm, p, n // tn),
        in_specs=[row, row,
                  pl.BlockSpec((1, 1, d), lambda i, q, j: (q, 0, 0)),
                  pl.BlockSpec((1, d, tn), lambda i, q, j: (q, 0, j))],
        out_specs=pl.BlockSpec((1, tm, tn), lambda i, q, j: (q, i, j)),
        out_shape=jax.ShapeDtypeStruct((p, m, n), out_dtype),
        scratch_shapes=[pltpu.VMEM((tm, d), BF16)],
        name=name,
        compiler_params=pltpu.CompilerParams(
            dimension_semantics=("arbitrary", "arbitrary", "arbitrary"),
            vmem_limit_bytes=VMEM_LIMIT_BYTES),
    )(h, shifted, mu.astype(F32).reshape(p, 1, d), w.astype(BF16))


def _norm_kernel(*refs, residual, modulate, emit_x, in_place):
    refs = list(refs)
    x_ref = refs.pop(0)
    o_ref = refs.pop(0) if residual else None
    pm_ref = refs.pop(0) if residual else None
    g_ref = refs.pop(0)
    m_ref = refs.pop(0) if modulate else None
    if in_place:
        refs.pop(0)
    xo_ref = refs.pop(0) if emit_x else None
    h_ref = refs.pop(0)
    x = x_ref[0].astype(F32)
    if residual:
        x = x + pm_ref[0, 2:3, :] * o_ref[0]
    if emit_x:
        xo_ref[0] = x
    y = x * lax.rsqrt(jnp.mean(x * x, axis=-1, keepdims=True) + RMS_EPS) * g_ref[...]
    if modulate:
        y = y * (1.0 + m_ref[0, 1:2, :]) + m_ref[0, 0:1, :]
    h_ref[0] = y.astype(h_ref.dtype)


def norm_rows(x, gain, *, out, out_row0, out_rows, mod=None, res=None, emit_x=False, tr=256,
              name):
    b, l, d = x.shape
    assert l % tr == 0 and out_row0 % tr == 0
    blk = lambda row0, per_batch=True: pl.BlockSpec(
        (1, tr, d), lambda i, j: (i if per_batch else 0, row0 // tr + j, 0))
    modspec = lambda m: pl.BlockSpec((1, 3, d), (lambda i, j: (i, 0, 0)) if m.shape[0] == b
                                     else (lambda i, j: (0, 0, 0)))
    args, specs = [x], [blk(0)]
    if res is not None:
        o, o_row0, prev_mod = res
        args += [o, prev_mod]
        specs += [blk(o_row0), modspec(prev_mod)]
    args.append(gain.astype(F32).reshape(1, d))
    specs.append(pl.BlockSpec((1, d), lambda i, j: (0, 0)))
    if mod is not None:
        args.append(mod)
        specs.append(modspec(mod))
    out_shapes, out_specs, aliases = [], [], {}
    if emit_x:
        out_shapes.append(jax.ShapeDtypeStruct((b, l, d), F32))
        out_specs.append(blk(0))
    if isinstance(out, jax.Array):
        aliases = {len(args): len(out_shapes)}
        args.append(out)
        specs.append(pl.BlockSpec(memory_space=pl.ANY))
        out_shapes.append(jax.ShapeDtypeStruct(out.shape, out.dtype))
    else:
        out_shapes.append(jax.ShapeDtypeStruct((b, out_rows, d), out))
    out_specs.append(blk(out_row0))
    kern = functools.partial(_norm_kernel, residual=res is not None, modulate=mod is not None,
                             emit_x=emit_x, in_place=bool(aliases))
    outs = pl.pallas_call(
        kern, grid=(b, l // tr), in_specs=specs, out_specs=out_specs, out_shape=out_shapes,
        input_output_aliases=aliases, name=name,
        compiler_params=pltpu.CompilerParams(dimension_semantics=("arbitrary", "arbitrary"),
                                             vmem_limit_bytes=VMEM_LIMIT_BYTES),
    )(*args)
    return (outs[1], outs[0]) if emit_x else (outs[0], None)


def _pick(total, pref):
    t = min(pref, total)
    while total % t:
        t //= 2
    return t


def _s5_prep_kernel(are_ref, aim_ref, ls_ref, bre_ref, bim_ref, cre_ref, cim_ref,
                    w1_ref, ot_ref, ar_ref, ai_ref, *, gb):
    T = S5_T
    GC = S5_GROUP
    lane = lax.broadcasted_iota(jnp.int32, (1, 128), 1)
    fwd = lane < S5_STATE
    s_col = lax.broadcasted_iota(jnp.int32, (T, 1), 0).astype(F32)
    ri = lax.broadcasted_iota(jnp.int32, (T * GC, T * GC), 0) % T
    ci = lax.broadcasted_iota(jnp.int32, (T * GC, T * GC), 1) % T
    lane2 = lax.broadcasted_iota(jnp.int32, (1, 256), 1) % 128
    fwd2 = lane2 < S5_STATE
    half = float(T // 2)

    for g in range(gb):
        a_re = are_ref[g]
        a_im = aim_ref[g]
        dt = jnp.exp(ls_ref[g])
        x_re = dt * a_re
        x_im = dt * a_im

        def lam_pow(p):
            mag = jnp.exp(p * x_re)
            ang = p * x_im
            return mag * jnp.cos(ang), mag * jnp.sin(ang)

        l_re, l_im = lam_pow(jnp.ones((1, 1), F32))
        den = a_re * a_re + a_im * a_im
        nr, ni = l_re - 1.0, l_im
        f_re = (nr * a_re + ni * a_im) / den
        f_im = (ni * a_re - nr * a_im) / den
        b_re = bre_ref[g]
        b_im = bim_ref[g]
        bb_re = f_re * b_re - f_im * b_im
        bb_im = f_re * b_im + f_im * b_re
        c_re = cre_ref[g]
        c_im = cim_ref[g]

        def times_pow(v_re, v_im, p):
            p_re, p_im = lam_pow(p)
            o_re = v_re[:, None, :] * p_re[None, :, :] - v_im[:, None, :] * p_im[None, :, :]
            o_im = v_re[:, None, :] * p_im[None, :, :] + v_im[:, None, :] * p_re[None, :, :]
            return o_re.reshape(GC * T, 128), o_im.reshape(GC * T, 128)

        pb = jnp.where(fwd, half - s_col, s_col - half)
        pc = -pb
        bl_re, bl_im = times_pow(bb_re, bb_im, pb)
        cl_re, cl_im = times_pow(c_re, c_im, pc)
        lhs = jnp.concatenate([bl_re, bl_im], axis=1)
        rhs = jnp.concatenate([cl_re, -cl_im], axis=1)
        r_hi, r_lo = _split2(rhs)

        def dot3(l):
            l_hi, l_lo = _split2(l)
            dn = (((1,), (1,)), ((), ()))
            return (lax.dot_general(l_hi, r_hi, dn, preferred_element_type=F32)
                    + lax.dot_general(l_hi, r_lo, dn, preferred_element_type=F32)
                    + lax.dot_general(l_lo, r_hi, dn, preferred_element_type=F32))

        t_f = dot3(jnp.where(fwd2, lhs, 0.0))
        t_b = dot3(jnp.where(fwd2, 0.0, lhs))
        toep = jnp.where(ci >= ri, t_f, 0.0) + jnp.where(ri >= ci, t_b, 0.0)

        ps = jnp.where(fwd, (T - 1.0) - s_col, s_col)
        s_re, s_im = times_pow(bb_re, bb_im, ps)
        w1_ref[g] = jnp.concatenate([toep, s_re, s_im], axis=1).astype(w1_ref.dtype)

        po = jnp.where(fwd, s_col + 1.0, T - s_col)
        o_re, o_im = times_pow(c_re, c_im, po)
        ot_ref[g] = jnp.concatenate([o_re, -o_im], axis=1).astype(ot_ref.dtype)

        pw_re, pw_im = lam_pow(jnp.full((1, 1), float(T), F32))
        ar_ref[g] = pw_re
        ai_ref[g] = pw_im


def _s5_kernel(u_ref, w1_ref, ot_ref, ar_ref, ai_ref, y_ref,
               p_ref, hf_re, hf_im, hb_re, hb_im, *, nb, n_ctx, n_chunks):
    gs = u_ref.shape[0]
    for g in range(gs):
        p_ref[g] = _dot(u_ref[g], w1_ref[g])
    ar = jnp.broadcast_to(ar_ref[...], (gs, nb, 128))
    ai = jnp.broadcast_to(ai_ref[...], (gs, nb, 128))
    fwd = lax.broadcasted_iota(jnp.int32, (gs, nb, 128), 2) < S5_STATE

    def step(i, carry):
        h_re, h_im = carry
        cb = jnp.where(i < n_ctx, n_ctx - 1 - i, n_chunks - 1 - (i - n_ctx))
        rf = pl.ds(pl.multiple_of(i * nb, nb), nb)
        rb = pl.ds(pl.multiple_of(cb * nb, nb), nb)
        hf_re[:, rf, :] = h_re
        hf_im[:, rf, :] = h_im
        hb_re[:, rb, :] = h_re
        hb_im[:, rb, :] = h_im
        x_re = jnp.where(fwd, p_ref[:, rf, 256:384], p_ref[:, rb, 256:384])
        x_im = jnp.where(fwd, p_ref[:, rf, 384:512], p_ref[:, rb, 384:512])
        n_re = ar * h_re - ai * h_im + x_re
        n_im = ar * h_im + ai * h_re + x_im
        return n_re, n_im

    zero = jnp.zeros((gs, nb, 128), F32)
    lax.fori_loop(0, n_chunks, step, (zero, zero))

    rows = p_ref.shape[1]
    fwd_all = lax.broadcasted_iota(jnp.int32, (rows, 128), 1) < S5_STATE
    for g in range(gs):
        h_all = jnp.concatenate([jnp.where(fwd_all, hf_re[g], hb_re[g]),
                                 jnp.where(fwd_all, hf_im[g], hb_im[g])], axis=1)
        y = p_ref[g, :, 0:256] + _dot_nt(h_all, ot_ref[g])
        y_ref[g] = y.astype(y_ref.dtype)


def s5_operators(a_re, a_im, log_step, b_re, b_im, c_re, c_im):
    n_dir, g, n = a_re.shape
    gb = 8

    def lanes(x):
        return jnp.concatenate([x[0], x[1]], axis=-1).astype(F32)

    are = lanes(a_re).reshape(g, 1, 2 * n)
    aim = lanes(a_im).reshape(g, 1, 2 * n)
    ls = lanes(jnp.broadcast_to(log_step[:, :, None], (n_dir, g, n))).reshape(g, 1, 2 * n)
    bre = lanes(jnp.swapaxes(b_re, -1, -2))
    bim = lanes(jnp.swapaxes(b_im, -1, -2))
    cre = lanes(c_re)
    cim = lanes(c_im)
    row = pl.BlockSpec((gb, 1, 2 * n), lambda i: (i, 0, 0))
    mat = pl.BlockSpec((gb, S5_GROUP, 2 * n), lambda i: (i, 0, 0))
    tg = S5_T * S5_GROUP
    return pl.pallas_call(
        functools.partial(_s5_prep_kernel, gb=gb),
        grid=(g // gb,),
        in_specs=[row, row, row, mat, mat, mat, mat],
        out_specs=[pl.BlockSpec((gb, tg, tg + 4 * n), lambda i: (i, 0, 0)),
                   pl.BlockSpec((gb, tg, 4 * n), lambda i: (i, 0, 0)),
                   row, row],
        out_shape=[jax.ShapeDtypeStruct((g, tg, tg + 4 * n), BF16),
                   jax.ShapeDtypeStruct((g, tg, 4 * n), BF16),
                   jax.ShapeDtypeStruct((g, 1, 2 * n), F32),
                   jax.ShapeDtypeStruct((g, 1, 2 * n), F32)],
        name="s5_operators",
        compiler_params=pltpu.CompilerParams(dimension_semantics=("arbitrary",),
                                             vmem_limit_bytes=VMEM_LIMIT_BYTES),
    )(are, aim, ls, bre, bim, cre, cim)


def s5_scan(u_g, w1, ot, ar, ai, *, nb, n_ctx, n_chunks):
    g, rows, tg = u_g.shape
    assert rows == n_chunks * nb and tg == S5_T * S5_GROUP
    gs = _pick(g, 4)
    blk = lambda w: pl.BlockSpec((gs,) + w, lambda i: (i, 0, 0))
    return pl.pallas_call(
        functools.partial(_s5_kernel, nb=nb, n_ctx=n_ctx, n_chunks=n_chunks),
        grid=(g // gs,),
        in_specs=[blk((rows, tg)), blk(w1.shape[1:]), blk(ot.shape[1:]),
                  blk((1, 128)), blk((1, 128))],
        out_specs=blk((rows, tg)),
        out_shape=jax.ShapeDtypeStruct((g, rows, tg), F32),
        scratch_shapes=[pltpu.VMEM((gs, rows, w1.shape[2]), F32)]
                       + [pltpu.VMEM((gs, rows, 128), F32)] * 4,
        name="s5_scan",
        compiler_params=pltpu.CompilerParams(dimension_semantics=("arbitrary",),
                                             vmem_limit_bytes=VMEM_LIMIT_BYTES),
    )(u_g, w1, ot, ar, ai)


def _eye(n):
    r = lax.broadcasted_iota(jnp.int32, (n, n), 0)
    c = lax.broadcasted_iota(jnp.int32, (n, n), 1)
    return jnp.where(r == c, 1.0, 0.0).astype(BF16)


def _to_groups_kernel(u_ref, o_ref, *, nb, n_j):
    g = o_ref.shape[0]
    eye = _eye(g)
    for b in range(nb):
        x = u_ref[b]
        rows = jnp.concatenate([x[:, j * g:(j + 1) * g] for j in range(n_j)], axis=0)
        o_ref[:, b, :] = _dot_nt(eye, rows).astype(o_ref.dtype)


def _from_groups_kernel(y_ref, o_ref, *, nb, n_j):
    g = y_ref.shape[0]
    t = o_ref.shape[1]
    eye = _eye(y_ref.shape[2]).astype(F32)
    for b in range(nb):
        cols = lax.dot_general(eye, y_ref[:, b, :], (((1,), (1,)), ((), ())),
                               preferred_element_type=F32)
        for i in range(n_j):
            o_ref[b, :, i * g:(i + 1) * g] = cols[i * t:(i + 1) * t, :]


def to_groups(uz, *, n_chunks):
    b, l, e2 = uz.shape
    e = e2 // 2
    g = e // S5_GROUP
    return pl.pallas_call(
        functools.partial(_to_groups_kernel, nb=b, n_j=S5_GROUP),
        grid=(n_chunks,),
        in_specs=[pl.BlockSpec((b, S5_T, e), lambda c: (0, c, 0))],
        out_specs=pl.BlockSpec((g, b, S5_T * S5_GROUP), lambda c: (0, c, 0)),
        out_shape=jax.ShapeDtypeStruct((g, n_chunks * b, S5_T * S5_GROUP), F32),
        name="s5_to_groups",
        compiler_params=pltpu.CompilerParams(dimension_semantics=("arbitrary",),
                                             vmem_limit_bytes=VMEM_LIMIT_BYTES),
    )(uz)


def from_groups(y_g, *, nb, n_chunks):
    g, rows, tg = y_g.shape
    e = g * S5_GROUP
    return pl.pallas_call(
        functools.partial(_from_groups_kernel, nb=nb, n_j=S5_GROUP),
        grid=(n_chunks,),
        in_specs=[pl.BlockSpec((g, nb, tg), lambda c: (0, c, 0))],
        out_specs=pl.BlockSpec((nb, S5_T, e), lambda c: (0, c, 0)),
        out_shape=jax.ShapeDtypeStruct((nb, n_chunks * S5_T, e), F32),
        name="s5_from_groups",
        compiler_params=pltpu.CompilerParams(dimension_semantics=("arbitrary",),
                                             vmem_limit_bytes=VMEM_LIMIT_BYTES),
    )(y_g)


def _rwkv_chunk_index(item, n_ctx, n_chunks):
    i = item % n_chunks
    back = jnp.where(i < n_ctx, n_ctx - 1 - i, n_chunks - 1 - (i - n_ctx))
    return jnp.where(item < n_chunks, i, back)


_HB_AC, _HB_RC, _HB_BH, _HB_KH, _HB_V, _HB_RB, _HB_RK, _HB_DM = range(8)
_HF_AKV, _HF_AO, _HF_Q, _HF_ZS = range(4)


def _rwkv_step(r_ref, k_ref, v_ref, z_ref, tw_ref, ta_ref, w2_ref, a2_ref, w0_ref, a0_ref,
               kk_ref, ka_ref, rk_ref, lnw_ref, lnb_ref, o_ref, s_ref, acc_ref,
               hbw, hfw, hww, hbr, hfr, hwr, *, n_ctx, n_chunks, n_pairs):
    C = RWKV_CHUNK
    n_items = 2 * n_chunks
    step = pl.program_id(2)
    pairs = range(n_pairs)

    def each(f, *lists):
        return [f(*args) for args in zip(*lists)]

    def cols(x):
        return [x[:, p * 128:(p + 1) * 128] for p in pairs]

    def join(xs):
        return jnp.concatenate(xs, axis=1)

    qa = jnp.minimum(step, n_items - 1)
    a_fwd = qa < n_chunks
    a_cidx = _rwkv_chunk_index(qa, n_ctx, n_chunks)
    d = jnp.where(a_fwd, 0, 1)
    sgn = jnp.where(a_fwd, 1, -1)
    qb = jnp.maximum(step - 1, 0)
    b_fwd = qb < n_chunks
    b_cidx = _rwkv_chunk_index(qb, n_ctx, n_chunks)
    b_live = step >= 1

    t_i = lax.broadcasted_iota(jnp.int32, (C, 128), 0)
    lane = lax.broadcasted_iota(jnp.int32, (C, 128), 1)
    s_i = lane % C
    lane_head = lane // RWKV_HEAD
    before = (s_i - t_i) * sgn
    strict = before < 0
    incl = before <= 0
    same_blk = (t_i // RWKV_BLOCK) == (s_i // RWKV_BLOCK)
    diag_m = strict & same_blk
    off_m = strict & jnp.logical_not(same_blk)
    tr = lax.broadcasted_iota(jnp.int32, (C, C), 0)
    tc = lax.broadcasted_iota(jnp.int32, (C, C), 1)
    cum_m = jnp.where((tc - tr) * sgn <= 0, 1.0, 0.0).astype(BF16)
    ri = lax.broadcasted_iota(jnp.int32, (128, 128), 0)
    ci = lax.broadcasted_iota(jnp.int32, (128, 128), 1)
    head_ones = jnp.where((ri // RWKV_HEAD) == (ci // RWKV_HEAD), 1.0, 0.0).astype(BF16)

    @pl.when(qb % n_chunks == 0)
    def _():
        s_ref[...] = jnp.zeros(s_ref.shape, F32)

    def stack(x):
        x = x.astype(BF16)
        zero = jnp.zeros_like(x)
        return jnp.concatenate([jnp.where(lane_head == 0, x, zero),
                                jnp.where(lane_head == 1, x, zero)], axis=0)

    def pack(x):
        return jnp.where(lane_head == 0, x[:C], x[C:])

    def head_sum(x):
        tot = _dot(jnp.concatenate(cols(x), axis=0), head_ones)
        return join([tot[p * C:(p + 1) * C] for p in pairs])

    mm = lambda l, r_: jnp.dot(l.astype(BF16), r_, preferred_element_type=F32)
    wide = lambda l, *rs: mm(l, jnp.concatenate(rs, axis=1))

    r = r_ref[0, 0].astype(F32)
    k = k_ref[0, 0].astype(F32)
    v = v_ref[0, 0].astype(F32)
    wp = w0_ref[pl.ds(d, 1), :] + _dot(jnp.tanh(tw_ref[0, 0]), w2_ref[d])
    ap = a0_ref[pl.ds(d, 1), :] + _dot(ta_ref[d, 0], a2_ref[d])
    sp = jnp.maximum(-wp, 0.0) + jnp.log(1.0 + jnp.exp(-jnp.abs(wp)))
    lw = -jnp.exp(-sp - 0.5)
    a_rate = 1.0 / (1.0 + jnp.exp(-ap))
    k_a = ka_ref[...]
    kkr = k * kk_ref[...]
    kd = k * (1.0 + (a_rate - 1.0) * k_a)
    lc = sum(jnp.dot(cum_m, p, preferred_element_type=F32) for p in _split3(lw))
    l_mid = lc[C // 2:C // 2 + 1, :]
    l_end = jnp.where(a_fwd, lc[C - 1:C, :], lc[0:1, :])
    dl = lc - l_mid
    e_in = jnp.exp(dl)
    e_ex = jnp.exp(dl - lw)
    e_iv = jnp.exp(-dl)
    c_row = jnp.exp(l_mid)
    e2 = jnp.exp(l_end - l_mid)
    kk_all = kkr / jnp.maximum(jnp.sqrt(head_sum(kkr * kkr)), NORM_EPS)
    at_all = -kk_all * e_ex
    bt_all = kk_all * (a_rate * e_iv)
    rt_all = r * e_in
    kt_all = kd * e_iv
    hww[...] = jnp.exp(l_end)
    hbw[_HB_AC] = (at_all * c_row).astype(BF16)
    hbw[_HB_RC] = (rt_all * c_row).astype(BF16)
    hbw[_HB_BH] = (bt_all * e2).astype(BF16)
    hbw[_HB_KH] = (kt_all * e2).astype(BF16)
    hbw[_HB_V] = v_ref[0, 0]
    at, bt, rp, kp = cols(at_all), cols(bt_all), cols(rt_all), cols(kt_all)
    vs = each(stack, cols(v))

    s_old = [s_ref[p] for p in pairs]
    asr = each(lambda a, r_, s: _dot_nt(jnp.concatenate([a, r_], axis=0), stack(s)),
               cols(hbr[_HB_AC]), cols(hbr[_HB_RC]), s_old)
    a4 = each(lambda a, r_, b, k_: _dot_nt(jnp.concatenate([a, r_], axis=0),
                                           jnp.concatenate([stack(b), stack(k_)], axis=0)),
              at, rp, bt, kp)
    ak = each(lambda m: jnp.where(strict, m[:C, 128:], 0.0), a4)
    hbw[_HB_RB] = join(each(lambda m: jnp.where(incl, m[C:, :128], 0.0), a4)).astype(BF16)
    hbw[_HB_RK] = join(each(lambda m: jnp.where(incl, m[C:, 128:], 0.0), a4)).astype(BF16)
    ad = each(lambda m: jnp.where(diag_m, m[:C, :128], 0.0), a4)
    ao = each(lambda m: jnp.where(off_m, m[:C, :128], 0.0), a4)
    hfw[_HF_AO] = join(ao)

    x0 = each(lambda m, q: m[:C] + q, asr, cols(hfr[_HF_AKV]))
    ao_b = cols(hfr[_HF_AO])
    dx = each(lambda d_, o, x: wide(d_, stack(o), stack(x)),
              cols(hbr[_HB_DM]), ao_b, x0)
    ad_s = each(stack, ad)
    hfw[_HF_AKV] = join(each(mm, ak, vs))
    p2 = each(mm, ad, ad_s)

    n1 = each(lambda o, m: o + m[:, :128], ao_b, dx)
    xa = each(lambda x, m: x + m[:, 128:], x0, dx)
    xc = each(lambda n_, x: x + mm(n_, stack(x)), n1, xa)
    p43 = each(lambda p, a_s: wide(p, stack(p), a_s), p2, ad_s)
    p4 = [m[:, :128] for m in p43]
    s3 = each(lambda a, p, m: a + p + m[:, 128:], ad, p2, p43)

    w1_ = each(lambda n_, x: mm(n_, stack(x)), n1, xc)
    p8s = each(lambda p, s: wide(p, stack(p), stack(s)), p4, s3)
    p8 = [m[:, :128] for m in p8s]
    s7 = each(lambda s, p, m: s + p + m[:, 128:], s3, p4, p8s)

    u = each(lambda n_, x, w_: x + mm(n_, stack(w_)), n1, xc, w1_)
    hbw[_HB_DM] = join(each(lambda s, p: s + p + mm(p, stack(s)), s7, p8)).astype(BF16)

    v_b = cols(hbr[_HB_V])
    ys = each(lambda m, rb_, rk_, u_, v_: m[C:] + mm(
        jnp.concatenate([rb_, rk_], axis=1), jnp.concatenate([stack(u_), stack(v_)], axis=0)),
        asr, cols(hbr[_HB_RB]), cols(hbr[_HB_RK]), u, v_b)
    upd = each(lambda u_, v_, b, k_: pack(_dot_tn(
        jnp.concatenate([u_, v_], axis=0), jnp.concatenate([b, k_], axis=0))),
        u, v_b, cols(hbr[_HB_BH]), cols(hbr[_HB_KH]))
    wc_b = hwr[...]
    for p in pairs:
        s_ref[p] = s_old[p] * wc_b[:, p * 128:(p + 1) * 128] + upd[p]

    y = join(ys)
    a_lat = a_cidx >= n_ctx
    b_lat = jnp.logical_and(b_cidx >= n_ctx, b_live)
    orow = pl.multiple_of(jnp.maximum(b_cidx - n_ctx, 0) * C, C)

    @pl.when(jnp.logical_and(a_lat, jnp.logical_not(a_fwd)))
    def _():
        ar_f = 1.0 / (1.0 + jnp.exp(-(a0_ref[0:1, :] + _dot(ta_ref[0, 0], a2_ref[0]))))
        k_sum = k * (2.0 + (ar_f + a_rate - 2.0) * k_a)
        hfw[_HF_Q] = r * k_sum * rk_ref[...]
        zg = z_ref[0, 0].astype(F32)
        hfw[_HF_ZS] = zg / (1.0 + jnp.exp(-zg))

    @pl.when(jnp.logical_and(b_lat, b_fwd))
    def _():
        acc_ref[pl.ds(orow, C), :] = y

    @pl.when(jnp.logical_and(b_lat, jnp.logical_not(b_fwd)))
    def _():
        y_sum = acc_ref[pl.ds(orow, C), :] + y
        inv = 1.0 / RWKV_HEAD
        dev = y_sum - head_sum(y_sum) * inv
        yn = dev * lax.rsqrt(head_sum(dev * dev) * inv + GN_EPS)
        yn = yn * lnw_ref[...] + lnb_ref[...]
        bonus = head_sum(hfr[_HF_Q]) * hbr[_HB_V].astype(F32)
        o_ref[0] = ((yn + bonus) * hfr[_HF_ZS]).astype(o_ref.dtype)


def _rwkv_kernel(*refs, n_ctx, n_chunks, n_pairs):
    *io, s_ref, acc_ref, hb0, hb1, hf0, hf1, hw0, hw1 = refs
    step = pl.program_id(2)
    run = functools.partial(_rwkv_step, *io, s_ref, acc_ref,
                            n_ctx=n_ctx, n_chunks=n_chunks, n_pairs=n_pairs)

    @pl.when(step == 0)
    def _():
        hb1[...] = jnp.zeros(hb1.shape, hb1.dtype)
        hf1[...] = jnp.zeros(hf1.shape, hf1.dtype)
        hw1[...] = jnp.zeros(hw1.shape, hw1.dtype)

    @pl.when(step % 2 == 0)
    def _():
        run(hb0, hf0, hw0, hb1, hf1, hw1)

    @pl.when(step % 2 == 1)
    def _():
        run(hb1, hf1, hw1, hb0, hf0, hw0)


def rwkv_scan(rkvz, tw, ta, w2, a2, w0, a0, k_k, k_a, r_k, ln_w, ln_b, *, n_ctx_tokens):
    _, b, l, e = rkvz.shape
    rank = tw.shape[-1]
    C = RWKV_CHUNK
    pw = min(e, 2048)
    n_pairs = pw // 128
    n_chunks = l // C
    n_ctx = n_ctx_tokens // C
    n_items = 2 * n_chunks
    l_out = l - n_ctx_tokens
    cix = functools.partial(_rwkv_chunk_index, n_ctx=n_ctx, n_chunks=n_chunks)
    item_a = lambda t: jnp.minimum(t, n_items - 1)
    item_b = lambda t: jnp.maximum(t - 1, 0)

    def out_chunk(t):
        q = item_b(t)
        c = cix(q)
        return jnp.where(jnp.logical_and(q >= n_chunks, c >= n_ctx), c, n_chunks - 1) - n_ctx

    seq = [pl.BlockSpec((1, 1, C, pw), functools.partial(
        lambda i, j, t, p: (p, i, cix(item_a(t)), j), p=p)) for p in range(4)]
    low_d = pl.BlockSpec((1, 1, C, rank),
                         lambda i, j, t: (item_a(t) // n_chunks, i, cix(item_a(t)), 0))
    low_2 = pl.BlockSpec((2, 1, C, rank), lambda i, j, t: (0, i, cix(item_a(t)), 0))
    up = pl.BlockSpec((2, rank, pw), lambda i, j, t: (0, 0, j))
    vec2 = pl.BlockSpec((2, pw), lambda i, j, t: (0, j))
    vec = pl.BlockSpec((1, pw), lambda i, j, t: (0, j))
    row = lambda x: x.astype(F32).reshape(1, e)
    return pl.pallas_call(
        functools.partial(_rwkv_kernel, n_ctx=n_ctx, n_chunks=n_chunks, n_pairs=n_pairs),
        grid=(b, e // pw, n_items + 1),
        in_specs=seq + [low_d, low_2, up, up, vec2, vec2, vec, vec, vec, vec, vec],
        out_specs=pl.BlockSpec((1, C, pw), lambda i, j, t: (i, out_chunk(t), j)),
        out_shape=jax.ShapeDtypeStruct((b, l_out, e), BF16),
        scratch_shapes=[pltpu.VMEM((n_pairs, C, 128), F32),
                        pltpu.VMEM((l_out, pw), F32),
                        pltpu.VMEM((8, C, pw), BF16), pltpu.VMEM((8, C, pw), BF16),
                        pltpu.VMEM((4, C, pw), F32), pltpu.VMEM((4, C, pw), F32),
                        pltpu.VMEM((1, pw), F32), pltpu.VMEM((1, pw), F32)],
        name="rwkv_scan",
        compiler_params=pltpu.CompilerParams(
            dimension_semantics=("arbitrary", "arbitrary", "arbitrary"),
            vmem_limit_bytes=VMEM_LIMIT_BYTES),
    )(rkvz, rkvz, rkvz, rkvz, tw, ta, w2.astype(BF16), a2.astype(BF16), w0.astype(F32),
      a0.astype(F32),
      row(k_k), row(k_a), row(r_k), row(ln_w), row(ln_b))


def _modulation(c, c_ctx, ada_w, ada_b):
    d = c.shape[-1]
    cond = jnp.concatenate([c, c_ctx[None]], axis=0)
    rows = cond.shape[0]
    pad = (-rows) % 16
    act = jnp.pad(jax.nn.silu(cond), ((0, pad), (0, 0)))
    m = matmul(act, ada_w, ada_b, out_dtype=F32, tm=rows + pad, tn=_pick(3 * d, 1024),
               name="mm_modulation")
    m = m[:rows].reshape(rows, 3, d)
    return m[:rows - 1], m[rows - 1:]


def _q_shift(h):
    b, l, d = h.shape
    rows = l // GRID_W
    g = h.reshape(b, rows, GRID_W, d)
    q = d // 4
    left = jnp.pad(g[:, :, :-1, :q], ((0, 0), (0, 0), (1, 0), (0, 0)))
    right = jnp.pad(g[:, :, 1:, q:2 * q], ((0, 0), (0, 0), (0, 1), (0, 0)))
    up = jnp.pad(g[:, :-1, :, 2 * q:3 * q], ((0, 0), (1, 0), (0, 0), (0, 0)))
    down = jnp.pad(g[:, 1:, :, 3 * q:], ((0, 0), (0, 1), (0, 0), (0, 0)))
    return jnp.concatenate([left, right, up, down], axis=-1).reshape(b, l, d)


def _seq_shift(h):
    half = h.shape[-1] // 2
    prev = jnp.pad(h[:, :-1, :half], ((0, 0), (1, 0), (0, 0)))
    nxt = jnp.pad(h[:, 1:, half:], ((0, 0), (0, 1), (0, 0)))
    return jnp.concatenate([prev, nxt], axis=-1)


def _s5_layer(x_lat, x_ctx, c, c_ctx, norm_g, ada_w, ada_b, in_w, a_re, a_im, log_step,
              b_re, b_im, c_re, c_im, d_skip, glu_w, glu_b, out_w):
    b, l_lat, d = x_lat.shape
    l_ctx = x_ctx.shape[1]
    l = l_ctx + l_lat
    e = in_w.shape[1] // 2
    g = e // S5_GROUP
    mod_l, mod_c = _modulation(c, c_ctx, ada_w, ada_b)
    tr = math.gcd(l_ctx, l_lat, 256)
    h, _ = norm_rows(x_lat, norm_g, out=BF16, out_row0=l_ctx, out_rows=l, mod=mod_l, tr=tr,
                     name="norm0_lat")
    h, _ = norm_rows(x_ctx, norm_g, out=h, out_row0=0, out_rows=l, mod=mod_c, tr=tr,
                     name="norm0_ctx")
    h = h.reshape(b * l, d)
    m = b * l
    tm = _pick(m, 1024)
    te = _pick(e, 1024)

    lane = jnp.arange(e)
    old = (lane % g) * S5_GROUP + lane // g
    perm = (lane[:, None] == old[None, :]).astype(BF16)
    in_w_p = matmul(in_w, perm, out_dtype=BF16, tm=_pick(d, 1024), tn=te, name="mm_perm_in")
    rows_p = lambda w: (w.astype(BF16).reshape(g, S5_GROUP, w.shape[1]).transpose(1, 0, 2)
                        .reshape(w.shape))
    glu_w_p = matmul(rows_p(glu_w), perm, out_dtype=BF16, tm=te, tn=te, name="mm_perm_glu")
    out_w_p = rows_p(out_w)
    d_p = d_skip.astype(F32)[old]
    glu_b_p = glu_b[old]

    uz = matmul(h, in_w_p, out_dtype=BF16, tm=tm, tn=_pick(2 * e, 1024), name="mm_s5_in")
    u = uz[:, :e]

    n_chunks = l // S5_T
    u_g = to_groups(uz.reshape(b, l, 2 * e), n_chunks=n_chunks)
    w1, ot, ar, ai = s5_operators(a_re, a_im, log_step, b_re, b_im, c_re, c_im)
    y_g = s5_scan(u_g, w1, ot, ar, ai, nb=b, n_ctx=l_ctx // S5_T, n_chunks=n_chunks)
    y_ssm = from_groups(y_g, nb=b, n_chunks=n_chunks).reshape(m, e)

    y1 = jax.nn.gelu(y_ssm.astype(F32) + d_p * u.astype(F32)).astype(BF16)
    y2 = matmul_glu(y1, glu_w_p, glu_b_p, uz, tm=tm, tn=te, name="mm_s5_glu")
    o = matmul(y2, out_w_p, out_dtype=F32, tm=tm, tn=_pick(d, 1024),
               name="mm_s5_out").reshape(b, l, d)
    return o, mod_l, mod_c


def _rwkv_layer(x_lat, x_ctx, prev, c, c_ctx, norm_g, ada_w, ada_b, mu, in_w, w0, w1, w2, a0, a1,
                a2, k_k, k_a, r_k, ln_w, ln_b, out_w, final_g):
    b, l_lat, d = x_lat.shape
    l_ctx = x_ctx.shape[1]
    l = l_ctx + l_lat
    e = in_w.shape[2]
    m = b * l
    o_prev, pm_l, pm_c = prev
    mod_l, mod_c = _modulation(c, c_ctx, ada_w, ada_b)
    tr = math.gcd(l_ctx, l_lat, 256)
    h3, x_lat = norm_rows(x_lat, norm_g, out=BF16, out_row0=l_ctx, out_rows=l, mod=mod_l,
                          res=(o_prev, l_ctx, pm_l), emit_x=True, tr=tr, name="norm1_lat")
    h3, _ = norm_rows(x_ctx, norm_g, out=h3, out_row0=0, out_rows=l, mod=mod_c,
                      res=(o_prev, 0, pm_c), tr=tr, name="norm1_ctx")
    h = h3.reshape(m, d)
    shifted = jnp.concatenate([_seq_shift(h3[:, :l_ctx]), _q_shift(h3[:, l_ctx:])],
                              axis=1).reshape(m, d)

    tm = _pick(m, 1024)
    tn = _pick(e, 1024)
    rkvz = matmul_lerp(h, shifted, mu[:4], in_w, out_dtype=BF16, tm=tm, tn=tn,
                       name="mm_rwkv_in").reshape(4, b, l, e)

    n_dir, _, lora = w1.shape
    lp = (-lora) % 128
    down = jnp.stack([jnp.concatenate([jnp.pad(l1[i], ((0, 0), (0, lp))) for i in range(n_dir)],
                                      axis=1) for l1 in (w1, a1)])
    low = matmul_lerp(h, shifted, mu[4:6], down, out_dtype=F32, tm=tm, tn=down.shape[2],
                      name="mm_lora_down")
    tw, ta = (low[i].reshape(b, l, n_dir, lora + lp).transpose(2, 0, 1, 3) for i in range(2))

    pad_up = lambda l2: jnp.pad(l2, ((0, 0), (0, lp), (0, 0)))
    yg = rwkv_scan(rkvz, tw, ta, pad_up(w2), pad_up(a2), w0, a0, k_k, k_a, r_k.reshape(e),
                   ln_w, ln_b, n_ctx_tokens=l_ctx)
    o = matmul(yg.reshape(b * l_lat, e), out_w, out_dtype=F32,
               tm=_pick(b * l_lat, 1024), tn=_pick(d, 1024), name="mm_rwkv_out").reshape(b, l_lat, d)
    out, _ = norm_rows(x_lat, final_g, out=F32, out_row0=0, out_rows=l_lat, res=(o, 0, mod_l),
                       tr=tr, name="norm_final")
    return out


def kernel(x, c, ctx, c_ctx, l0_norm_g, l0_ada_w, l0_ada_b, l0_in_w, l0_a_re, l0_a_im, l0_log_step, l0_b_re, l0_b_im, l0_c_re, l0_c_im, l0_d, l0_glu_w, l0_glu_b, l0_out_w, l1_norm_g, l1_ada_w, l1_ada_b, l1_mu, l1_in_w, l1_w0, l1_w1, l1_w2, l1_a0, l1_a1, l1_a2, l1_k_k, l1_k_a, l1_r_k, l1_ln_w, l1_ln_b, l1_out_w, final_norm_g):
    prev = _s5_layer(x, ctx, c, c_ctx, l0_norm_g, l0_ada_w, l0_ada_b, l0_in_w, l0_a_re,
                     l0_a_im, l0_log_step, l0_b_re, l0_b_im, l0_c_re, l0_c_im, l0_d,
                     l0_glu_w, l0_glu_b, l0_out_w)
    out = _rwkv_layer(x, ctx, prev, c, c_ctx, l1_norm_g, l1_ada_w, l1_ada_b, l1_mu, l1_in_w,
                      l1_w0, l1_w1, l1_w2, l1_a0, l1_a1, l1_a2, l1_k_k, l1_k_a, l1_r_k,
                      l1_ln_w, l1_ln_b, l1_out_w, final_norm_g)
    return out.astype(x.dtype)
```

```python
import functools
import math

import jax
import jax.numpy as jnp
from jax import lax
from jax.experimental import pallas as pl
from jax.experimental.pallas import tpu as pltpu

F32 = jnp.float32
BF16 = jnp.bfloat16

GRID_W = 64
S5_GROUP = 16
S5_STATE = 64
S5_T = 16
RWKV_HEAD = 64
RWKV_CHUNK = 64
RWKV_BLOCK = 16
assert RWKV_BLOCK == 16 and RWKV_CHUNK == 4 * RWKV_BLOCK
RMS_EPS = 1e-6
GN_EPS = 64e-5
NORM_EPS = 1e-12

VMEM_LIMIT_BYTES = 56 * 1024 * 1024


def _dot(a, b):
    return jnp.dot(a.astype(BF16), b.astype(BF16), preferred_element_type=F32)


def _dot_nt(a, b):
    return lax.dot_general(a.astype(BF16), b.astype(BF16), (((1,), (1,)), ((), ())),
                           preferred_element_type=F32)


def _dot_tn(a, b):
    return lax.dot_general(a.astype(BF16), b.astype(BF16), (((0,), (0,)), ((), ())),
                           preferred_element_type=F32)


def _split2(x):
    hi = x.astype(BF16)
    lo = (x - hi.astype(F32)).astype(BF16)
    return hi, lo


def _split3(x):
    p1 = x.astype(BF16)
    r1 = x - p1.astype(F32)
    p2 = r1.astype(BF16)
    p3 = (r1 - p2.astype(F32)).astype(BF16)
    return p1, p2, p3


def _mm_kernel(a_ref, b_ref, o_ref):
    o_ref[...] = jnp.dot(a_ref[...], b_ref[...],
                         preferred_element_type=F32).astype(o_ref.dtype)


def _mm_bias_kernel(a_ref, b_ref, bias_ref, o_ref):
    acc = jnp.dot(a_ref[...], b_ref[...], preferred_element_type=F32)
    o_ref[...] = (acc + bias_ref[...]).astype(o_ref.dtype)


def matmul(a, b, bias=None, *, out_dtype, tm, tn, name):
    m, kg = a.shape
    k, n = b.shape
    groups = kg // k
    assert kg == groups * k and m % tm == 0 and n % tn == 0, (a.shape, b.shape, tm, tn)
    assert bias is None or groups == 1
    nt = n // tn
    a = a.astype(BF16)
    b = b.astype(BF16)
    in_specs = [pl.BlockSpec((tm, k), lambda j, i: (i, j // nt)),
                pl.BlockSpec((k, tn), lambda j, i: (0, j % nt))]
    args = [a, b]
    kern = _mm_kernel
    if bias is not None:
        in_specs.append(pl.BlockSpec((1, tn), lambda j, i: (0, j)))
        args.append(bias.astype(F32).reshape(1, n))
        kern = _mm_bias_kernel
    return pl.pallas_call(
        kern,
        grid=(groups * nt, m // tm),
        in_specs=in_specs,
        out_specs=pl.BlockSpec((tm, tn), lambda j, i: (i, j)),
        out_shape=jax.ShapeDtypeStruct((m, groups * n), out_dtype),
        name=name,
        compiler_params=pltpu.CompilerParams(
            dimension_semantics=("arbitrary", "arbitrary"),
            vmem_limit_bytes=VMEM_LIMIT_BYTES),
    )(*args)


def _mm_glu_kernel(a_ref, b_ref, bias_ref, z_ref, o_ref, *, tn):
    j = pl.program_id(0)
    acc = jnp.dot(a_ref[...], b_ref[...], preferred_element_type=F32) + bias_ref[...]
    y = a_ref[:, pl.ds(pl.multiple_of(j * tn, tn), tn)].astype(F32)
    z = z_ref[...].astype(F32)
    o_ref[...] = (y / (1.0 + jnp.exp(-acc)) * (z / (1.0 + jnp.exp(-z)))).astype(o_ref.dtype)


def matmul_glu(y, w, bias, uz, *, tm, tn, name):
    m, e = y.shape
    assert w.shape == (e, e) and uz.shape == (m, 2 * e) and m % tm == 0 and e % tn == 0
    n_tiles = e // tn
    return pl.pallas_call(
        functools.partial(_mm_glu_kernel, tn=tn),
        grid=(n_tiles, m // tm),
        in_specs=[pl.BlockSpec((tm, e), lambda j, i: (i, 0)),
                  pl.BlockSpec((e, tn), lambda j, i: (0, j)),
                  pl.BlockSpec((1, tn), lambda j, i: (0, j)),
                  pl.BlockSpec((tm, tn), lambda j, i: (i, j + n_tiles))],
        out_specs=pl.BlockSpec((tm, tn), lambda j, i: (i, j)),
        out_shape=jax.ShapeDtypeStruct((m, e), BF16),
        name=name,
        compiler_params=pltpu.CompilerParams(
            dimension_semantics=("arbitrary", "arbitrary"),
            vmem_limit_bytes=VMEM_LIMIT_BYTES),
    )(y.astype(BF16), w.astype(BF16), bias.astype(F32).reshape(1, e), uz)


def _mm_lerp_kernel(h_ref, s_ref, mu_ref, w_ref, o_ref, lhs_ref):
    @pl.when(pl.program_id(2) == 0)
    def _():
        h = h_ref[...].astype(F32)
        lhs_ref[...] = (h + (s_ref[...].astype(F32) - h) * mu_ref[0]).astype(lhs_ref.dtype)

    o_ref[0] = jnp.dot(lhs_ref[...], w_ref[0], preferred_element_type=F32).astype(o_ref.dtype)


def matmul_lerp(h, shifted, mu, w, *, out_dtype, tm, tn, name):
    m, d = h.shape
    p, d2, n = w.shape
    assert d == d2 and mu.shape == (p, d) and m % tm == 0 and n % tn == 0
    row = pl.BlockSpec((tm, d), lambda i, q, j: (i, 0))
    return pl.pallas_call(
        _mm_lerp_kernel,
        grid=(m // tm, p, n // tn),
        in_specs=[row, row,
                  pl.BlockSpec((1, 1, d), lambda i, q, j: (q, 0, 0)),
                  pl.BlockSpec((1, d, tn), lambda i, q, j: (q, 0, j))],
        out_specs=pl.BlockSpec((1, tm, tn), lambda i, q, j: (q, i, j)),
        out_shape=jax.ShapeDtypeStruct((p, m, n), out_dtype),
        scratch_shapes=[pltpu.VMEM((tm, d), BF16)],
        name=name,
        compiler_params=pltpu.CompilerParams(
            dimension_semantics=("arbitrary", "arbitrary", "arbitrary"),
            vmem_limit_bytes=VMEM_LIMIT_BYTES),
    )(h, shifted, mu.astype(F32).reshape(p, 1, d), w.astype(BF16))


def _norm_kernel(*refs, residual, modulate, emit_x, in_place):
    refs = list(refs)
    x_ref = refs.pop(0)
    o_ref = refs.pop(0) if residual else None
    pm_ref = refs.pop(0) if residual else None
    g_ref = refs.pop(0)
    m_ref = refs.pop(0) if modulate else None
    if in_place:
        refs.pop(0)
    xo_ref = refs.pop(0) if emit_x else None
    h_ref = refs.pop(0)
    x = x_ref[0].astype(F32)
    if residual:
        x = x + pm_ref[0, 2:3, :] * o_ref[0]
    if emit_x:
        xo_ref[0] = x
    y = x * lax.rsqrt(jnp.mean(x * x, axis=-1, keepdims=True) + RMS_EPS) * g_ref[...]
    if modulate:
        y = y * (1.0 + m_ref[0, 1:2, :]) + m_ref[0, 0:1, :]
    h_ref[0] = y.astype(h_ref.dtype)


def norm_rows(x, gain, *, out, out_row0, out_rows, mod=None, res=None, emit_x=False, tr=256,
              name):
    b, l, d = x.shape
    assert l % tr == 0 and out_row0 % tr == 0
    blk = lambda row0, per_batch=True: pl.BlockSpec(
        (1, tr, d), lambda i, j: (i if per_batch else 0, row0 // tr + j, 0))
    modspec = lambda m: pl.BlockSpec((1, 3, d), (lambda i, j: (i, 0, 0)) if m.shape[0] == b
                                     else (lambda i, j: (0, 0, 0)))
    args, specs = [x], [blk(0)]
    if res is not None:
        o, o_row0, prev_mod = res
        args += [o, prev_mod]
        specs += [blk(o_row0), modspec(prev_mod)]
    args.append(gain.astype(F32).reshape(1, d))
    specs.append(pl.BlockSpec((1, d), lambda i, j: (0, 0)))
    if mod is not None:
        args.append(mod)
        specs.append(modspec(mod))
    out_shapes, out_specs, aliases = [], [], {}
    if emit_x:
        out_shapes.append(jax.ShapeDtypeStruct((b, l, d), F32))
        out_specs.append(blk(0))
    if isinstance(out, jax.Array):
        aliases = {len(args): len(out_shapes)}
        args.append(out)
        specs.append(pl.BlockSpec(memory_space=pl.ANY))
        out_shapes.append(jax.ShapeDtypeStruct(out.shape, out.dtype))
    else:
        out_shapes.append(jax.ShapeDtypeStruct((b, out_rows, d), out))
    out_specs.append(blk(out_row0))
    kern = functools.partial(_norm_kernel, residual=res is not None, modulate=mod is not None,
                             emit_x=emit_x, in_place=bool(aliases))
    outs = pl.pallas_call(
        kern, grid=(b, l // tr), in_specs=specs, out_specs=out_specs, out_shape=out_shapes,
        input_output_aliases=aliases, name=name,
        compiler_params=pltpu.CompilerParams(dimension_semantics=("arbitrary", "arbitrary"),
                                             vmem_limit_bytes=VMEM_LIMIT_BYTES),
    )(*args)
    return (outs[1], outs[0]) if emit_x else (outs[0], None)


def _pick(total, pref):
    t = min(pref, total)
    while total % t:
        t //= 2
    return t


def _s5_prep_kernel(are_ref, aim_ref, ls_ref, bre_ref, bim_ref, cre_ref, cim_ref,
                    w1_ref, ot_ref, ar_ref, ai_ref, *, gb):
    T = S5_T
    GC = S5_GROUP
    lane = lax.broadcasted_iota(jnp.int32, (1, 128), 1)
    fwd = lane < S5_STATE
    s_col = lax.broadcasted_iota(jnp.int32, (T, 1), 0).astype(F32)
    ri = lax.broadcasted_iota(jnp.int32, (T * GC, T * GC), 0) % T
    ci = lax.broadcasted_iota(jnp.int32, (T * GC, T * GC), 1) % T
    lane2 = lax.broadcasted_iota(jnp.int32, (1, 256), 1) % 128
    fwd2 = lane2 < S5_STATE
    half = float(T // 2)

    for g in range(gb):
        a_re = are_ref[g]
        a_im = aim_ref[g]
        dt = jnp.exp(ls_ref[g])
        x_re = dt * a_re
        x_im = dt * a_im

        def lam_pow(p):
            mag = jnp.exp(p * x_re)
            ang = p * x_im
            return mag * jnp.cos(ang), mag * jnp.sin(ang)

        l_re, l_im = lam_pow(jnp.ones((1, 1), F32))
        den = a_re * a_re + a_im * a_im
        nr, ni = l_re - 1.0, l_im
        f_re = (nr * a_re + ni * a_im) / den
        f_im = (ni * a_re - nr * a_im) / den
        b_re = bre_ref[g]
        b_im = bim_ref[g]
        bb_re = f_re * b_re - f_im * b_im
        bb_im = f_re * b_im + f_im * b_re
        c_re = cre_ref[g]
        c_im = cim_ref[g]

        def times_pow(v_re, v_im, p):
            p_re, p_im = lam_pow(p)
            o_re = v_re[:, None, :] * p_re[None, :, :] - v_im[:, None, :] * p_im[None, :, :]
            o_im = v_re[:, None, :] * p_im[None, :, :] + v_im[:, None, :] * p_re[None, :, :]
            return o_re.reshape(GC * T, 128), o_im.reshape(GC * T, 128)

        pb = jnp.where(fwd, half - s_col, s_col - half)
        pc = -pb
        bl_re, bl_im = times_pow(bb_re, bb_im, pb)
        cl_re, cl_im = times_pow(c_re, c_im, pc)
        lhs = jnp.concatenate([bl_re, bl_im], axis=1)
        rhs = jnp.concatenate([cl_re, -cl_im], axis=1)
        r_hi, r_lo = _split2(rhs)

        def dot3(l):
            l_hi, l_lo = _split2(l)
            dn = (((1,), (1,)), ((), ()))
            return (lax.dot_general(l_hi, r_hi, dn, preferred_element_type=F32)
                    + lax.dot_general(l_hi, r_lo, dn, preferred_element_type=F32)
                    + lax.dot_general(l_lo, r_hi, dn, preferred_element_type=F32))

        t_f = dot3(jnp.where(fwd2, lhs, 0.0))
        t_b = dot3(jnp.where(fwd2, 0.0, lhs))
        toep = jnp.where(ci >= ri, t_f, 0.0) + jnp.where(ri >= ci, t_b, 0.0)

        ps = jnp.where(fwd, (T - 1.0) - s_col, s_col)
        s_re, s_im = times_pow(bb_re, bb_im, ps)
        w1_ref[g] = jnp.concatenate([toep, s_re, s_im], axis=1).astype(w1_ref.dtype)

        po = jnp.where(fwd, s_col + 1.0, T - s_col)
        o_re, o_im = times_pow(c_re, c_im, po)
        ot_ref[g] = jnp.concatenate([o_re, -o_im], axis=1).astype(ot_ref.dtype)

        pw_re, pw_im = lam_pow(jnp.full((1, 1), float(T), F32))
        ar_ref[g] = pw_re
        ai_ref[g] = pw_im


def _s5_kernel(u_ref, w1_ref, ot_ref, ar_ref, ai_ref, y_ref,
               p_ref, hf_re, hf_im, hb_re, hb_im, *, nb, n_ctx, n_chunks):
    gs = u_ref.shape[0]
    for g in range(gs):
        p_ref[g] = _dot(u_ref[g], w1_ref[g])
    ar = jnp.broadcast_to(ar_ref[...], (gs, nb, 128))
    ai = jnp.broadcast_to(ai_ref[...], (gs, nb, 128))
    fwd = lax.broadcasted_iota(jnp.int32, (gs, nb, 128), 2) < S5_STATE

    def step(i, carry):
        h_re, h_im = carry
        cb = jnp.where(i < n_ctx, n_ctx - 1 - i, n_chunks - 1 - (i - n_ctx))
        rf = pl.ds(pl.multiple_of(i * nb, nb), nb)
        rb = pl.ds(pl.multiple_of(cb * nb, nb), nb)
        hf_re[:, rf, :] = h_re
        hf_im[:, rf, :] = h_im
        hb_re[:, rb, :] = h_re
        hb_im[:, rb, :] = h_im
        x_re = jnp.where(fwd, p_ref[:, rf, 256:384], p_ref[:, rb, 256:384])
        x_im = jnp.where(fwd, p_ref[:, rf, 384:512], p_ref[:, rb, 384:512])
        n_re = ar * h_re - ai * h_im + x_re
        n_im = ar * h_im + ai * h_re + x_im
        return n_re, n_im

    zero = jnp.zeros((gs, nb, 128), F32)
    lax.fori_loop(0, n_chunks, step, (zero, zero))

    rows = p_ref.shape[1]
    fwd_all = lax.broadcasted_iota(jnp.int32, (rows, 128), 1) < S5_STATE
    for g in range(gs):
        h_all = jnp.concatenate([jnp.where(fwd_all, hf_re[g], hb_re[g]),
                                 jnp.where(fwd_all, hf_im[g], hb_im[g])], axis=1)
        y = p_ref[g, :, 0:256] + _dot_nt(h_all, ot_ref[g])
        y_ref[g] = y.astype(y_ref.dtype)


def s5_operators(a_re, a_im, log_step, b_re, b_im, c_re, c_im):
    n_dir, g, n = a_re.shape
    gb = 8

    def lanes(x):
        return jnp.concatenate([x[0], x[1]], axis=-1).astype(F32)

    are = lanes(a_re).reshape(g, 1, 2 * n)
    aim = lanes(a_im).reshape(g, 1, 2 * n)
    ls = lanes(jnp.broadcast_to(log_step[:, :, None], (n_dir, g, n))).reshape(g, 1, 2 * n)
    bre = lanes(jnp.swapaxes(b_re, -1, -2))
    bim = lanes(jnp.swapaxes(b_im, -1, -2))
    cre = lanes(c_re)
    cim = lanes(c_im)
    row = pl.BlockSpec((gb, 1, 2 * n), lambda i: (i, 0, 0))
    mat = pl.BlockSpec((gb, S5_GROUP, 2 * n), lambda i: (i, 0, 0))
    tg = S5_T * S5_GROUP
    return pl.pallas_call(
        functools.partial(_s5_prep_kernel, gb=gb),
        grid=(g // gb,),
        in_specs=[row, row, row, mat, mat, mat, mat],
        out_specs=[pl.BlockSpec((gb, tg, tg + 4 * n), lambda i: (i, 0, 0)),
                   pl.BlockSpec((gb, tg, 4 * n), lambda i: (i, 0, 0)),
                   row, row],
        out_shape=[jax.ShapeDtypeStruct((g, tg, tg + 4 * n), BF16),
                   jax.ShapeDtypeStruct((g, tg, 4 * n), BF16),
                   jax.ShapeDtypeStruct((g, 1, 2 * n), F32),
                   jax.ShapeDtypeStruct((g, 1, 2 * n), F32)],
        name="s5_operators",
        compiler_params=pltpu.CompilerParams(dimension_semantics=("arbitrary",),
                                             vmem_limit_bytes=VMEM_LIMIT_BYTES),
    )(are, aim, ls, bre, bim, cre, cim)


def s5_scan(u_g, w1, ot, ar, ai, *, nb, n_ctx, n_chunks):
    g, rows, tg = u_g.shape
    assert rows == n_chunks * nb and tg == S5_T * S5_GROUP
    gs = _pick(g, 4)
    blk = lambda w: pl.BlockSpec((gs,) + w, lambda i: (i, 0, 0))
    return pl.pallas_call(
        functools.partial(_s5_kernel, nb=nb, n_ctx=n_ctx, n_chunks=n_chunks),
        grid=(g // gs,),
        in_specs=[blk((rows, tg)), blk(w1.shape[1:]), blk(ot.shape[1:]),
                  blk((1, 128)), blk((1, 128))],
        out_specs=blk((rows, tg)),
        out_shape=jax.ShapeDtypeStruct((g, rows, tg), F32),
        scratch_shapes=[pltpu.VMEM((gs, rows, w1.shape[2]), F32)]
                       + [pltpu.VMEM((gs, rows, 128), F32)] * 4,
        name="s5_scan",
        compiler_params=pltpu.CompilerParams(dimension_semantics=("arbitrary",),
                                             vmem_limit_bytes=VMEM_LIMIT_BYTES),
    )(u_g, w1, ot, ar, ai)


def _eye(n):
    r = lax.broadcasted_iota(jnp.int32, (n, n), 0)
    c = lax.broadcasted_iota(jnp.int32, (n, n), 1)
    return jnp.where(r == c, 1.0, 0.0).astype(BF16)


def _to_groups_kernel(u_ref, o_ref, *, nb, n_j):
    g = o_ref.shape[0]
    eye = _eye(g)
    for b in range(nb):
        x = u_ref[b]
        rows = jnp.concatenate([x[:, j * g:(j + 1) * g] for j in range(n_j)], axis=0)
        o_ref[:, b, :] = _dot_nt(eye, rows).astype(o_ref.dtype)


def _from_groups_kernel(y_ref, o_ref, *, nb, n_j):
    g = y_ref.shape[0]
    t = o_ref.shape[1]
    eye = _eye(y_ref.shape[2]).astype(F32)
    for b in range(nb):
        cols = lax.dot_general(eye, y_ref[:, b, :], (((1,), (1,)), ((), ())),
                               preferred_element_type=F32)
        for i in range(n_j):
            o_ref[b, :, i * g:(i + 1) * g] = cols[i * t:(i + 1) * t, :]


def to_groups(uz, *, n_chunks):
    b, l, e2 = uz.shape
    e = e2 // 2
    g = e // S5_GROUP
    return pl.pallas_call(
        functools.partial(_to_groups_kernel, nb=b, n_j=S5_GROUP),
        grid=(n_chunks,),
        in_specs=[pl.BlockSpec((b, S5_T, e), lambda c: (0, c, 0))],
        out_specs=pl.BlockSpec((g, b, S5_T * S5_GROUP), lambda c: (0, c, 0)),
        out_shape=jax.ShapeDtypeStruct((g, n_chunks * b, S5_T * S5_GROUP), F32),
        name="s5_to_groups",
        compiler_params=pltpu.CompilerParams(dimension_semantics=("arbitrary",),
                                             vmem_limit_bytes=VMEM_LIMIT_BYTES),
    )(uz)


def from_groups(y_g, *, nb, n_chunks):
    g, rows, tg = y_g.shape
    e = g * S5_GROUP
    return pl.pallas_call(
        functools.partial(_from_groups_kernel, nb=nb, n_j=S5_GROUP),
        grid=(n_chunks,),
        in_specs=[pl.BlockSpec((g, nb, tg), lambda c: (0, c, 0))],
        out_specs=pl.BlockSpec((nb, S5_T, e), lambda c: (0, c, 0)),
        out_shape=jax.ShapeDtypeStruct((nb, n_chunks * S5_T, e), F32),
        name="s5_from_groups",
        compiler_params=pltpu.CompilerParams(dimension_semantics=("arbitrary",),
                                             vmem_limit_bytes=VMEM_LIMIT_BYTES),
    )(y_g)


def _rwkv_chunk_index(item, n_ctx, n_chunks):
    i = item % n_chunks
    back = jnp.where(i < n_ctx, n_ctx - 1 - i, n_chunks - 1 - (i - n_ctx))
    return jnp.where(item < n_chunks, i, back)


_HB_AC, _HB_RC, _HB_BH, _HB_KH, _HB_V, _HB_RB, _HB_RK, _HB_DM = range(8)
_HF_AKV, _HF_AO, _HF_Q, _HF_ZS = range(4)


def _rwkv_step(r_ref, k_ref, v_ref, z_ref, tw_ref, ta_ref, w2_ref, a2_ref, w0_ref, a0_ref,
               kk_ref, ka_ref, rk_ref, lnw_ref, lnb_ref, o_ref, s_ref, acc_ref,
               hbw, hfw, hww, hbr, hfr, hwr, *, n_ctx, n_chunks, n_pairs):
    C = RWKV_CHUNK
    n_items = 2 * n_chunks
    step = pl.program_id(2)
    pairs = range(n_pairs)

    def each(f, *lists):
        return [f(*args) for args in zip(*lists)]

    def cols(x):
        return [x[:, p * 128:(p + 1) * 128] for p in pairs]

    def join(xs):
        return jnp.concatenate(xs, axis=1)

    qa = jnp.minimum(step, n_items - 1)
    a_fwd = qa < n_chunks
    a_cidx = _rwkv_chunk_index(qa, n_ctx, n_chunks)
    d = jnp.where(a_fwd, 0, 1)
    sgn = jnp.where(a_fwd, 1, -1)
    qb = jnp.maximum(step - 1, 0)
    b_fwd = qb < n_chunks
    b_cidx = _rwkv_chunk_index(qb, n_ctx, n_chunks)
    b_live = step >= 1

    t_i = lax.broadcasted_iota(jnp.int32, (C, 128), 0)
    lane = lax.broadcasted_iota(jnp.int32, (C, 128), 1)
    s_i = lane % C
    lane_head = lane // RWKV_HEAD
    before = (s_i - t_i) * sgn
    strict = before < 0
    incl = before <= 0
    same_blk = (t_i // RWKV_BLOCK) == (s_i // RWKV_BLOCK)
    diag_m = strict & same_blk
    off_m = strict & jnp.logical_not(same_blk)
    tr = lax.broadcasted_iota(jnp.int32, (C, C), 0)
    tc = lax.broadcasted_iota(jnp.int32, (C, C), 1)
    cum_m = jnp.where((tc - tr) * sgn <= 0, 1.0, 0.0).astype(BF16)
    ri = lax.broadcasted_iota(jnp.int32, (128, 128), 0)
    ci = lax.broadcasted_iota(jnp.int32, (128, 128), 1)
    head_ones = jnp.where((ri // RWKV_HEAD) == (ci // RWKV_HEAD), 1.0, 0.0).astype(BF16)

    @pl.when(qb % n_chunks == 0)
    def _():
        s_ref[...] = jnp.zeros(s_ref.shape, F32)

    def stack(x):
        x = x.astype(BF16)
        zero = jnp.zeros_like(x)
        return jnp.concatenate([jnp.where(lane_head == 0, x, zero),
                                jnp.where(lane_head == 1, x, zero)], axis=0)

    def pack(x):
        return jnp.where(lane_head == 0, x[:C], x[C:])

    def head_sum(x):
        tot = _dot(jnp.concatenate(cols(x), axis=0), head_ones)
        return join([tot[p * C:(p + 1) * C] for p in pairs])

    mm = lambda l, r_: jnp.dot(l.astype(BF16), r_, preferred_element_type=F32)
    wide = lambda l, *rs: mm(l, jnp.concatenate(rs, axis=1))

    r = r_ref[0, 0].astype(F32)
    k = k_ref[0, 0].astype(F32)
    v = v_ref[0, 0].astype(F32)
    wp = w0_ref[pl.ds(d, 1), :] + _dot(jnp.tanh(tw_ref[0, 0]), w2_ref[d])
    ap = a0_ref[pl.ds(d, 1), :] + _dot(ta_ref[d, 0], a2_ref[d])
    sp = jnp.maximum(-wp, 0.0) + jnp.log(1.0 + jnp.exp(-jnp.abs(wp)))
    lw = -jnp.exp(-sp - 0.5)
    a_rate = 1.0 / (1.0 + jnp.exp(-ap))
    k_a = ka_ref[...]
    kkr = k * kk_ref[...]
    kd = k * (1.0 + (a_rate - 1.0) * k_a)
    lc = sum(jnp.dot(cum_m, p, preferred_element_type=F32) for p in _split3(lw))
    l_mid = lc[C // 2:C // 2 + 1, :]
    l_end = jnp.where(a_fwd, lc[C - 1:C, :], lc[0:1, :])
    dl = lc - l_mid
    e_in = jnp.exp(dl)
    e_ex = jnp.exp(dl - lw)
    e_iv = jnp.exp(-dl)
    c_row = jnp.exp(l_mid)
    e2 = jnp.exp(l_end - l_mid)
    kk_all = kkr / jnp.maximum(jnp.sqrt(head_sum(kkr * kkr)), NORM_EPS)
    at_all = -kk_all * e_ex
    bt_all = kk_all * (a_rate * e_iv)
    rt_all = r * e_in
    kt_all = kd * e_iv
    hww[...] = jnp.exp(l_end)
    hbw[_HB_AC] = (at_all * c_row).astype(BF16)
    hbw[_HB_RC] = (rt_all * c_row).astype(BF16)
    hbw[_HB_BH] = (bt_all * e2).astype(BF16)
    hbw[_HB_KH] = (kt_all * e2).astype(BF16)
    hbw[_HB_V] = v_ref[0, 0]
    at, bt, rp, kp = cols(at_all), cols(bt_all), cols(rt_all), cols(kt_all)
    vs = each(stack, cols(v))

    s_old = [s_ref[p] for p in pairs]
    asr = each(lambda a, r_, s: _dot_nt(jnp.concatenate([a, r_], axis=0), stack(s)),
               cols(hbr[_HB_AC]), cols(hbr[_HB_RC]), s_old)
    a4 = each(lambda a, r_, b, k_: _dot_nt(jnp.concatenate([a, r_], axis=0),
                                           jnp.concatenate([stack(b), stack(k_)], axis=0)),
              at, rp, bt, kp)
    ak = each(lambda m: jnp.where(strict, m[:C, 128:], 0.0), a4)
    hbw[_HB_RB] = join(each(lambda m: jnp.where(incl, m[C:, :128], 0.0), a4)).astype(BF16)
    hbw[_HB_RK] = join(each(lambda m: jnp.where(incl, m[C:, 128:], 0.0), a4)).astype(BF16)
    ad = each(lambda m: jnp.where(diag_m, m[:C, :128], 0.0), a4)
    ao = each(lambda m: jnp.where(off_m, m[:C, :128], 0.0), a4)
    hfw[_HF_AO] = join(ao)

    x0 = each(lambda m, q: m[:C] + q, asr, cols(hfr[_HF_AKV]))
    ao_b = cols(hfr[_HF_AO])
    dx = each(lambda d_, o, x: wide(d_, stack(o), stack(x)),
              cols(hbr[_HB_DM]), ao_b, x0)
    ad_s = each(stack, ad)
    hfw[_HF_AKV] = join(each(mm, ak, vs))
    p2 = each(mm, ad, ad_s)

    n1 = each(lambda o, m: o + m[:, :128], ao_b, dx)
    xa = each(lambda x, m: x + m[:, 128:], x0, dx)
    xc = each(lambda n_, x: x + mm(n_, stack(x)), n1, xa)
    p43 = each(lambda p, a_s: wide(p, stack(p), a_s), p2, ad_s)
    p4 = [m[:, :128] for m in p43]
    s3 = each(lambda a, p, m: a + p + m[:, 128:], ad, p2, p43)

    w1_ = each(lambda n_, x: mm(n_, stack(x)), n1, xc)
    p8s = each(lambda p, s: wide(p, stack(p), stack(s)), p4, s3)
    p8 = [m[:, :128] for m in p8s]
    s7 = each(lambda s, p, m: s + p + m[:, 128:], s3, p4, p8s)

    u = each(lambda n_, x, w_: x + mm(n_, stack(w_)), n1, xc, w1_)
    hbw[_HB_DM] = join(each(lambda s, p: s + p + mm(p, stack(s)), s7, p8)).astype(BF16)

    v_b = cols(hbr[_HB_V])
    ys = each(lambda m, rb_, rk_, u_, v_: m[C:] + mm(
        jnp.concatenate([rb_, rk_], axis=1), jnp.concatenate([stack(u_), stack(v_)], axis=0)),
        asr, cols(hbr[_HB_RB]), cols(hbr[_HB_RK]), u, v_b)
    upd = each(lambda u_, v_, b, k_: pack(_dot_tn(
        jnp.concatenate([u_, v_], axis=0), jnp.concatenate([b, k_], axis=0))),
        u, v_b, cols(hbr[_HB_BH]), cols(hbr[_HB_KH]))
    wc_b = hwr[...]
    for p in pairs:
        s_ref[p] = s_old[p] * wc_b[:, p * 128:(p + 1) * 128] + upd[p]

    y = join(ys)
    a_lat = a_cidx >= n_ctx
    b_lat = jnp.logical_and(b_cidx >= n_ctx, b_live)
    orow = pl.multiple_of(jnp.maximum(b_cidx - n_ctx, 0) * C, C)

    @pl.when(jnp.logical_and(a_lat, jnp.logical_not(a_fwd)))
    def _():
        ar_f = 1.0 / (1.0 + jnp.exp(-(a0_ref[0:1, :] + _dot(ta_ref[0, 0], a2_ref[0]))))
        k_sum = k * (2.0 + (ar_f + a_rate - 2.0) * k_a)
        hfw[_HF_Q] = r * k_sum * rk_ref[...]
        zg = z_ref[0, 0].astype(F32)
        hfw[_HF_ZS] = zg / (1.0 + jnp.exp(-zg))

    @pl.when(jnp.logical_and(b_lat, b_fwd))
    def _():
        acc_ref[pl.ds(orow, C), :] = y

    @pl.when(jnp.logical_and(b_lat, jnp.logical_not(b_fwd)))
    def _():
        y_sum = acc_ref[pl.ds(orow, C), :] + y
        inv = 1.0 / RWKV_HEAD
        dev = y_sum - head_sum(y_sum) * inv
        yn = dev * lax.rsqrt(head_sum(dev * dev) * inv + GN_EPS)
        yn = yn * lnw_ref[...] + lnb_ref[...]
        bonus = head_sum(hfr[_HF_Q]) * hbr[_HB_V].astype(F32)
        o_ref[0] = ((yn + bonus) * hfr[_HF_ZS]).astype(o_ref.dtype)


def _rwkv_kernel(*refs, n_ctx, n_chunks, n_pairs):
    *io, s_ref, acc_ref, hb0, hb1, hf0, hf1, hw0, hw1 = refs
    step = pl.program_id(2)
    run = functools.partial(_rwkv_step, *io, s_ref, acc_ref,
                            n_ctx=n_ctx, n_chunks=n_chunks, n_pairs=n_pairs)

    @pl.when(step == 0)
    def _():
        hb1[...] = jnp.zeros(hb1.shape, hb1.dtype)
        hf1[...] = jnp.zeros(hf1.shape, hf1.dtype)
        hw1[...] = jnp.zeros(hw1.shape, hw1.dtype)

    @pl.when(step % 2 == 0)
    def _():
        run(hb0, hf0, hw0, hb1, hf1, hw1)

    @pl.when(step % 2 == 1)
    def _():
        run(hb1, hf1, hw1, hb0, hf0, hw0)


def rwkv_scan(rkvz, tw, ta, w2, a2, w0, a0, k_k, k_a, r_k, ln_w, ln_b, *, n_ctx_tokens):
    _, b, l, e = rkvz.shape
    rank = tw.shape[-1]
    C = RWKV_CHUNK
    pw = min(e, 2048)
    n_pairs = pw // 128
    n_chunks = l // C
    n_ctx = n_ctx_tokens // C
    n_items = 2 * n_chunks
    l_out = l - n_ctx_tokens
    cix = functools.partial(_rwkv_chunk_index, n_ctx=n_ctx, n_chunks=n_chunks)
    item_a = lambda t: jnp.minimum(t, n_items - 1)
    item_b = lambda t: jnp.maximum(t - 1, 0)

    def out_chunk(t):
        q = item_b(t)
        c = cix(q)
        return jnp.where(jnp.logical_and(q >= n_chunks, c >= n_ctx), c, n_chunks - 1) - n_ctx

    seq = [pl.BlockSpec((1, 1, C, pw), functools.partial(
        lambda i, j, t, p: (p, i, cix(item_a(t)), j), p=p)) for p in range(4)]
    low_d = pl.BlockSpec((1, 1, C, rank),
                         lambda i, j, t: (item_a(t) // n_chunks, i, cix(item_a(t)), 0))
    low_2 = pl.BlockSpec((2, 1, C, rank), lambda i, j, t: (0, i, cix(item_a(t)), 0))
    up = pl.BlockSpec((2, rank, pw), lambda i, j, t: (0, 0, j))
    vec2 = pl.BlockSpec((2, pw), lambda i, j, t: (0, j))
    vec = pl.BlockSpec((1, pw), lambda i, j, t: (0, j))
    row = lambda x: x.astype(F32).reshape(1, e)
    return pl.pallas_call(
        functools.partial(_rwkv_kernel, n_ctx=n_ctx, n_chunks=n_chunks, n_pairs=n_pairs),
        grid=(b, e // pw, n_items + 1),
        in_specs=seq + [low_d, low_2, up, up, vec2, vec2, vec, vec, vec, vec, vec],
        out_specs=pl.BlockSpec((1, C, pw), lambda i, j, t: (i, out_chunk(t), j)),
        out_shape=jax.ShapeDtypeStruct((b, l_out, e), BF16),
        scratch_shapes=[pltpu.VMEM((n_pairs, C, 128), F32),
                        pltpu.VMEM((l_out, pw), F32),
                        pltpu.VMEM((8, C, pw), BF16), pltpu.VMEM((8, C, pw), BF16),
                        pltpu.VMEM((4, C, pw), F32), pltpu.VMEM((4, C, pw), F32),
                        pltpu.VMEM((1, pw), F32), pltpu.VMEM((1, pw), F32)],
        name="rwkv_scan",
        compiler_params=pltpu.CompilerParams(
            dimension_semantics=("arbitrary", "arbitrary", "arbitrary"),
            vmem_limit_bytes=VMEM_LIMIT_BYTES),
    )(rkvz, rkvz, rkvz, rkvz, tw, ta, w2.astype(BF16), a2.astype(BF16), w0.astype(F32),
      a0.astype(F32),
      row(k_k), row(k_a), row(r_k), row(ln_w), row(ln_b))


def _modulation(c, c_ctx, ada_w, ada_b):
    d = c.shape[-1]
    cond = jnp.concatenate([c, c_ctx[None]], axis=0)
    rows = cond.shape[0]
    pad = (-rows) % 16
    act = jnp.pad(jax.nn.silu(cond), ((0, pad), (0, 0)))
    m = matmul(act, ada_w, ada_b, out_dtype=F32, tm=rows + pad, tn=_pick(3 * d, 1024),
               name="mm_modulation")
    m = m[:rows].reshape(rows, 3, d)
    return m[:rows - 1], m[rows - 1:]


def _q_shift(h):
    b, l, d = h.shape
    rows = l // GRID_W
    g = h.reshape(b, rows, GRID_W, d)
    q = d // 4
    left = jnp.pad(g[:, :, :-1, :q], ((0, 0), (0, 0), (1, 0), (0, 0)))
    right = jnp.pad(g[:, :, 1:, q:2 * q], ((0, 0), (0, 0), (0, 1), (0, 0)))
    up = jnp.pad(g[:, :-1, :, 2 * q:3 * q], ((0, 0), (1, 0), (0, 0), (0, 0)))
    down = jnp.pad(g[:, 1:, :, 3 * q:], ((0, 0), (0, 1), (0, 0), (0, 0)))
    return jnp.concatenate([left, right, up, down], axis=-1).reshape(b, l, d)


def _seq_shift(h):
    half = h.shape[-1] // 2
    prev = jnp.pad(h[:, :-1, :half], ((0, 0), (1, 0), (0, 0)))
    nxt = jnp.pad(h[:, 1:, half:], ((0, 0), (0, 1), (0, 0)))
    return jnp.concatenate([prev, nxt], axis=-1)


def _s5_layer(x_lat, x_ctx, c, c_ctx, norm_g, ada_w, ada_b, in_w, a_re, a_im, log_step,
              b_re, b_im, c_re, c_im, d_skip, glu_w, glu_b, out_w):
    b, l_lat, d = x_lat.shape
    l_ctx = x_ctx.shape[1]
    l = l_ctx + l_lat
    e = in_w.shape[1] // 2
    g = e // S5_GROUP
    mod_l, mod_c = _modulation(c, c_ctx, ada_w, ada_b)
    tr = math.gcd(l_ctx, l_lat, 256)
    h, _ = norm_rows(x_lat, norm_g, out=jnp.zeros((b, l, d), BF16), out_row0=l_ctx, out_rows=l,
                     mod=mod_l, tr=tr, name="norm0_lat")
    h, _ = norm_rows(x_ctx, norm_g, out=h, out_row0=0, out_rows=l, mod=mod_c, tr=tr,
                     name="norm0_ctx")
    h = h.reshape(b * l, d)
    m = b * l
    tm = _pick(m, 1024)
    te = _pick(e, 1024)

    lane = jnp.arange(e)
    old = (lane % g) * S5_GROUP + lane // g
    perm = (lane[:, None] == old[None, :]).astype(BF16)
    in_w_p = matmul(in_w, perm, out_dtype=BF16, tm=_pick(d, 1024), tn=te, name="mm_perm_in")
    rows_p = lambda w: (w.astype(BF16).reshape(g, S5_GROUP, w.shape[1]).transpose(1, 0, 2)
                        .reshape(w.shape))
    glu_w_p = matmul(rows_p(glu_w), perm, out_dtype=BF16, tm=te, tn=te, name="mm_perm_glu")
    out_w_p = rows_p(out_w)
    d_p = d_skip.astype(F32)[old]
    glu_b_p = glu_b[old]

    uz = matmul(h, in_w_p, out_dtype=BF16, tm=tm, tn=_pick(2 * e, 1024), name="mm_s5_in")
    u = uz[:, :e]

    n_chunks = l // S5_T
    u_g = to_groups(uz.reshape(b, l, 2 * e), n_chunks=n_chunks)
    w1, ot, ar, ai = s5_operators(a_re, a_im, log_step, b_re, b_im, c_re, c_im)
    y_g = s5_scan(u_g, w1, ot, ar, ai, nb=b, n_ctx=l_ctx // S5_T, n_chunks=n_chunks)
    y_ssm = from_groups(y_g, nb=b, n_chunks=n_chunks).reshape(m, e)

    y1 = jax.nn.gelu(y_ssm.astype(F32) + d_p * u.astype(F32)).astype(BF16)
    y2 = matmul_glu(y1, glu_w_p, glu_b_p, uz, tm=tm, tn=te, name="mm_s5_glu")
    o = matmul(y2, out_w_p, out_dtype=F32, tm=tm, tn=_pick(d, 1024),
               name="mm_s5_out").reshape(b, l, d)
    return o, mod_l, mod_c


def _rwkv_layer(x_lat, x_ctx, prev, c, c_ctx, norm_g, ada_w, ada_b, mu, in_w, w0, w1, w2, a0, a1,
                a2, k_k, k_a, r_k, ln_w, ln_b, out_w, final_g):
    b, l_lat, d = x_lat.shape
    l_ctx = x_ctx.shape[1]
    l = l_ctx + l_lat
    e = in_w.shape[2]
    m = b * l
    o_prev, pm_l, pm_c = prev
    mod_l, mod_c = _modulation(c, c_ctx, ada_w, ada_b)
    tr = math.gcd(l_ctx, l_lat, 256)
    h3, x_lat = norm_rows(x_lat, norm_g, out=jnp.zeros((b, l, d), BF16), out_row0=l_ctx,
                          out_rows=l, mod=mod_l, res=(o_prev, l_ctx, pm_l), emit_x=True, tr=tr,
                          name="norm1_lat")
    h3, _ = norm_rows(x_ctx, norm_g, out=h3, out_row0=0, out_rows=l, mod=mod_c,
                      res=(o_prev, 0, pm_c), tr=tr, name="norm1_ctx")
    h = h3.reshape(m, d)
    shifted = jnp.concatenate([_seq_shift(h3[:, :l_ctx]), _q_shift(h3[:, l_ctx:])],
                              axis=1).reshape(m, d)

    tm = _pick(m, 1024)
    tn = _pick(e, 1024)
    rkvz = matmul_lerp(h, shifted, mu[:4], in_w, out_dtype=BF16, tm=tm, tn=tn,
                       name="mm_rwkv_in").reshape(4, b, l, e)

    n_dir, _, lora = w1.shape
    lp = (-lora) % 128
    down = jnp.stack([jnp.concatenate([jnp.pad(l1[i], ((0, 0), (0, lp))) for i in range(n_dir)],
                                      axis=1) for l1 in (w1, a1)])
    low = matmul_lerp(h, shifted, mu[4:6], down, out_dtype=F32, tm=tm, tn=down.shape[2],
                      name="mm_lora_down")
    tw, ta = (low[i].reshape(b, l, n_dir, lora + lp).transpose(2, 0, 1, 3) for i in range(2))

    pad_up = lambda l2: jnp.pad(l2, ((0, 0), (0, lp), (0, 0)))
    yg = rwkv_scan(rkvz, tw, ta, pad_up(w2), pad_up(a2), w0, a0, k_k, k_a, r_k.reshape(e),
                   ln_w, ln_b, n_ctx_tokens=l_ctx)
    o = matmul(yg.reshape(b * l_lat, e), out_w, out_dtype=F32,
               tm=_pick(b * l_lat, 1024), tn=_pick(d, 1024), name="mm_rwkv_out").reshape(b, l_lat, d)
    out, _ = norm_rows(x_lat, final_g, out=F32, out_row0=0, out_rows=l_lat, res=(o, 0, mod_l),
                       tr=tr, name="norm_final")
    return out


def kernel(x, c, ctx, c_ctx, l0_norm_g, l0_ada_w, l0_ada_b, l0_in_w, l0_a_re, l0_a_im, l0_log_step, l0_b_re, l0_b_im, l0_c_re, l0_c_im, l0_d, l0_glu_w, l0_glu_b, l0_out_w, l1_norm_g, l1_ada_w, l1_ada_b, l1_mu, l1_in_w, l1_w0, l1_w1, l1_w2, l1_a0, l1_a1, l1_a2, l1_k_k, l1_k_a, l1_r_k, l1_ln_w, l1_ln_b, l1_out_w, final_norm_g):
    prev = _s5_layer(x, ctx, c, c_ctx, l0_norm_g, l0_ada_w, l0_ada_b, l0_in_w, l0_a_re,
                     l0_a_im, l0_log_step, l0_b_re, l0_b_im, l0_c_re, l0_c_im, l0_d,
                     l0_glu_w, l0_glu_b, l0_out_w)
    out = _rwkv_layer(x, ctx, prev, c, c_ctx, l1_norm_g, l1_ada_w, l1_ada_b, l1_mu, l1_in_w,
                      l1_w0, l1_w1, l1_w2, l1_a0, l1_a1, l1_a2, l1_k_k, l1_k_a, l1_r_k,
                      l1_ln_w, l1_ln_b, l1_out_w, final_norm_g)
    return out.astype(x.dtype)
```
